```python
import math, functools
import jax, jax.numpy as jnp
from jax import lax
import numpy as np

D_MODEL = 2048
BATCH = 2
SEQ = 4096
DEPTH = 4
DEC_BATCH = 32
DEC_SEQ = 8
PAST_LEN = 16384
PAGE_SIZE = 128

HEAD_DIM = 64
N_HEADS = D_MODEL // 128
N_KV_HEADS = N_HEADS // 4
Q_GROUP = N_HEADS // N_KV_HEADS
ATTN_W = N_HEADS * HEAD_DIM
KV_W = N_KV_HEADS * HEAD_DIM
WINDOW = 128
ATTN_BLOCK = 128
ROPE_THETA = 10000.0
SGU_CHUNK = 128
SGU_W = D_MODEL // 2
SGU_GROUPS = 8
SGU_CH = SGU_W // SGU_GROUPS
D_FF = 4 * D_MODEL
RMS_EPS = 1e-6
LN_EPS = 1e-5
NEG_INF = -1e30
SPLITS = (ATTN_W, ATTN_W + KV_W, ATTN_W + 2 * KV_W, ATTN_W + 2 * KV_W + SGU_W,
          ATTN_W + 2 * KV_W + 2 * SGU_W, ATTN_W + 2 * KV_W + 2 * SGU_W + D_MODEL)
IN_COLS = ATTN_W + 2 * KV_W + 2 * SGU_W + 2 * D_MODEL

kernel_name = "hybrid_swa_sink_sgu_gated_decoder_step"


def rms_norm(x, g):
    xf = x.astype(jnp.float32)
    y = xf * lax.rsqrt(jnp.mean(xf * xf, axis=-1, keepdims=True) + RMS_EPS)
    return (y * g.astype(jnp.float32)).astype(x.dtype)


def layer_norm(x, g, b):
    xf = x.astype(jnp.float32)
    mu = jnp.mean(xf, axis=-1, keepdims=True)
    var = jnp.mean(jnp.square(xf - mu), axis=-1, keepdims=True)
    y = (xf - mu) * lax.rsqrt(var + LN_EPS)
    return (y * g.astype(jnp.float32) + b.astype(jnp.float32)).astype(x.dtype)


def rope(x, pos):
    half = HEAD_DIM // 2
    inv = jnp.power(jnp.float32(ROPE_THETA), -jnp.arange(half, dtype=jnp.float32) / half)
    ang = pos.astype(jnp.float32)[:, None] * inv[None, :]
    cos = jnp.cos(ang)[:, None, :]
    sin = jnp.sin(ang)[:, None, :]
    xf = x.astype(jnp.float32)
    x1, x2 = xf[..., :half], xf[..., half:]
    return jnp.concatenate([x1 * cos - x2 * sin, x2 * cos + x1 * sin], axis=-1).astype(x.dtype)


def sink_attention(q, k, v, mask, sink):
    scale = HEAD_DIM ** -0.5
    s = jnp.einsum('...qhgd,...khd->...hgqk', q, k).astype(jnp.float32) * scale
    s = jnp.where(mask, s, NEG_INF)
    sk = sink.astype(jnp.float32).reshape(N_KV_HEADS, Q_GROUP, 1, 1)
    m = jnp.maximum(jnp.max(s, axis=-1, keepdims=True), sk)
    p = jnp.exp(s - m)
    p = p / (jnp.sum(p, axis=-1, keepdims=True) + jnp.exp(sk - m))
    return jnp.einsum('...hgqk,...khd->...qhgd', p.astype(v.dtype), v)


def swa_prompt(q, k, v, sink):
    B, L = q.shape[0], q.shape[1]
    nb = L // ATTN_BLOCK
    qb = q.reshape(B, nb, ATTN_BLOCK, N_KV_HEADS, Q_GROUP, HEAD_DIM)

    def with_prev(t):
        tb = t.reshape(B, nb, ATTN_BLOCK, N_KV_HEADS, HEAD_DIM)
        prev = jnp.pad(tb[:, :-1], ((0, 0), (1, 0), (0, 0), (0, 0), (0, 0)))
        return jnp.concatenate([prev, tb], axis=2)

    kk, vv = with_prev(k), with_prev(v)
    qi = jnp.arange(ATTN_BLOCK)[:, None]
    kj = jnp.arange(2 * ATTN_BLOCK)[None, :]
    diff = qi + ATTN_BLOCK - kj
    band = (diff >= 0) & (diff < WINDOW)
    blk = jnp.arange(nb)[:, None, None]
    key_valid = (blk - 1) * ATTN_BLOCK + kj[None] >= 0
    mask = (band[None] & key_valid)[:, None, None]
    o = sink_attention(qb, kk, vv, mask, sink)
    return o.reshape(B, L, ATTN_W), k[:, -WINDOW:], v[:, -WINDOW:]


def swa_sample(q, k, v, sink, ck, cv):
    Bd, S = q.shape[0], q.shape[1]
    kk = jnp.concatenate([ck.astype(k.dtype), k], axis=1)
    vv = jnp.concatenate([cv.astype(v.dtype), v], axis=1)
    q_pos = PAST_LEN + jnp.arange(S)
    k_pos = PAST_LEN - WINDOW + jnp.arange(WINDOW + S)
    diff = q_pos[:, None] - k_pos[None, :]
    mask = (diff >= 0) & (diff < WINDOW)
    qg = q.reshape(Bd, S, N_KV_HEADS, Q_GROUP, HEAD_DIM)
    o = sink_attention(qg, kk, vv, mask, sink)
    return o.reshape(Bd, S, ATTN_W), kk[:, -WINDOW:], vv[:, -WINDOW:]


def spatial_gate(u, v, w_s, b_s):
    B, L = v.shape[0], v.shape[1]
    csz = min(L, SGU_CHUNK)
    pad = (-L) % csz
    vp = jnp.pad(v, ((0, 0), (0, pad), (0, 0), (0, 0)))
    n = (L + pad) // csz
    vc = vp.reshape(B, n, csz, SGU_GROUPS, SGU_CH)
    tril = jnp.tril(jnp.ones((csz, csz), dtype=bool))
    w = jnp.where(tril[None], w_s[:, :csz, :csz], jnp.zeros((), w_s.dtype))
    mixed = jnp.einsum('gts,bnsgc->bntgc', w, vc) + b_s[:, :csz].T[None, None, :, :, None]
    mixed = mixed.reshape(B, n * csz, SGU_GROUPS, SGU_CH)[:, :L]
    return u * mixed


def block(x, pos, attend, n1, w_in, qg, kg, sink, ln_g, ln_b, sgu_w, sgu_b,
          w_au, w_su, w_out, n2, w1, w2):
    B, L = x.shape[0], x.shape[1]
    xn = rms_norm(x, n1)
    q, k, v, u, vs, ga, gm = jnp.split(xn @ w_in, SPLITS, axis=-1)
    q = rope(rms_norm(q.reshape(B, L, N_HEADS, HEAD_DIM), qg), pos)
    k = rope(rms_norm(k.reshape(B, L, N_KV_HEADS, HEAD_DIM), kg), pos)
    v = v.reshape(B, L, N_KV_HEADS, HEAD_DIM)
    a, k_state, v_state = attend(q, k, v, sink)
    u = jax.nn.gelu(u, approximate=False).reshape(B, L, SGU_GROUPS, SGU_CH)
    vs = layer_norm(jax.nn.gelu(vs, approximate=False), ln_g, ln_b).reshape(B, L, SGU_GROUPS, SGU_CH)
    m = spatial_gate(u, vs, sgu_w, sgu_b).reshape(B, L, SGU_W)
    merged = jax.nn.sigmoid(ga) * (a @ w_au) + jax.nn.sigmoid(gm) * (m @ w_su)
    h = x + merged @ w_out
    y = h + jnp.square(jax.nn.relu(rms_norm(h, n2) @ w1)) @ w2
    return y, k_state, v_state, vs


def setup_inputs(seed: int = 0) -> dict:
    key = jax.random.key(seed)
    ks = jax.random.split(key, 20)
    f32 = jnp.float32
    nrm = lambda k, shape, s: jax.random.normal(k, shape, f32) * s
    return {
        "x_prompt": nrm(ks[0], (BATCH, SEQ, D_MODEL), 1.0),
        "x_sample": nrm(ks[1], (DEC_BATCH, DEC_SEQ, D_MODEL), 1.0),
        "cache_k": nrm(ks[2], (DEPTH, DEC_BATCH, WINDOW, N_KV_HEADS, HEAD_DIM), 1.0),
        "cache_v": nrm(ks[3], (DEPTH, DEC_BATCH, WINDOW, N_KV_HEADS, HEAD_DIM), 1.0),
        "norm1_g": 1.0 + nrm(ks[4], (DEPTH, D_MODEL), 0.02),
        "w_in": nrm(ks[5], (DEPTH, D_MODEL, IN_COLS), D_MODEL ** -0.5),
        "q_norm_g": 1.0 + nrm(ks[6], (DEPTH, HEAD_DIM), 0.02),
        "k_norm_g": 1.0 + nrm(ks[7], (DEPTH, HEAD_DIM), 0.02),
        "attn_sinks": nrm(ks[8], (DEPTH, N_HEADS), 1.0),
        "sgu_ln_g": 1.0 + nrm(ks[9], (DEPTH, SGU_W), 0.02),
        "sgu_ln_b": nrm(ks[10], (DEPTH, SGU_W), 0.02),
        "sgu_w": nrm(ks[11], (DEPTH, SGU_GROUPS, SGU_CHUNK, SGU_CHUNK), SGU_CHUNK ** -0.5),
        "sgu_b": 1.0 + nrm(ks[12], (DEPTH, SGU_GROUPS, SGU_CHUNK), 0.02),
        "w_attn_up": nrm(ks[13], (DEPTH, ATTN_W, D_MODEL), ATTN_W ** -0.5),
        "w_sgu_up": nrm(ks[14], (DEPTH, SGU_W, D_MODEL), SGU_W ** -0.5),
        "w_out": nrm(ks[15], (DEPTH, D_MODEL, D_MODEL), D_MODEL ** -0.5),
        "norm2_g": 1.0 + nrm(ks[16], (DEPTH, D_MODEL), 0.02),
        "w_ff1": nrm(ks[17], (DEPTH, D_MODEL, D_FF), D_MODEL ** -0.5),
        "w_ff2": nrm(ks[18], (DEPTH, D_FF, D_MODEL), D_FF ** -0.5),
    }


def reference(x_prompt, x_sample, cache_k, cache_v, norm1_g, w_in, q_norm_g, k_norm_g,
              attn_sinks, sgu_ln_g, sgu_ln_b, sgu_w, sgu_b, w_attn_up, w_sgu_up, w_out,
              norm2_g, w_ff1, w_ff2):
    pos_p = jnp.arange(x_prompt.shape[1])
    pos_s = PAST_LEN + jnp.arange(x_sample.shape[1])
    xp, xs = x_prompt, x_sample
    kp_l, vp_l, ks_l, vs_l, sv_l = [], [], [], [], []
    for l in range(DEPTH):
        params = (norm1_g[l], w_in[l], q_norm_g[l], k_norm_g[l], attn_sinks[l], sgu_ln_g[l],
                  sgu_ln_b[l], sgu_w[l], sgu_b[l], w_attn_up[l], w_sgu_up[l], w_out[l],
                  norm2_g[l], w_ff1[l], w_ff2[l])
        xp, kp, vp, _ = block(xp, pos_p, swa_prompt, *params)
        sample_attend = functools.partial(swa_sample, ck=cache_k[l], cv=cache_v[l])
        xs, ksn, vsn, sgu_v = block(xs, pos_s, sample_attend, *params)
        kp_l.append(kp); vp_l.append(vp); ks_l.append(ksn); vs_l.append(vsn); sv_l.append(sgu_v)
    new_k_prompt = jnp.stack(kp_l)
    new_v_prompt = jnp.stack(vp_l)
    new_k_sample = jnp.stack(ks_l)
    new_v_sample = jnp.stack(vs_l)
    new_sgu_v_sample = jnp.stack(sv_l)
    return (xp, xs, new_k_prompt, new_v_prompt, new_k_sample, new_v_sample, new_sgu_v_sample)
```

```python
import functools
import math

import jax
import jax.numpy as jnp
import numpy as np
from jax import lax
from jax.experimental import pallas as pl
from jax.experimental.pallas import tpu as pltpu

D_MODEL = 2048
DEPTH = 4
PAST_LEN = 16384
HEAD_DIM = 64
N_HEADS = 16
N_KV_HEADS = 4
Q_GROUP = 4
ATTN_W = N_HEADS * HEAD_DIM
KV_W = N_KV_HEADS * HEAD_DIM
WINDOW = 128
ROPE_THETA = 10000.0
SGU_CHUNK = 128
SGU_W = D_MODEL // 2
SGU_GROUPS = 8
SGU_CH = SGU_W // SGU_GROUPS
D_FF = 4 * D_MODEL
RMS_EPS = 1e-6
LN_EPS = 1e-5
NEG_INF = -1e30

QKV_W = ATTN_W + 2 * KV_W
UV_OFF = QKV_W
GATE_OFF = QKV_W + 2 * SGU_W

LANES = 128
VMEM_LIMIT = 56 * 1024 * 1024

BF16 = jnp.bfloat16
F32 = jnp.float32


def _dot(a, b):
    return jnp.dot(a, b, preferred_element_type=F32)


def _rms_rows(x, g):
    ms = jnp.mean(x * x, axis=-1, keepdims=True)
    return x * lax.rsqrt(ms + RMS_EPS) * g


def _gelu(x):
    return 0.5 * x * (1.0 + lax.erf(x * np.float32(math.sqrt(0.5))))


def _params(*sem):
    return pltpu.CompilerParams(dimension_semantics=sem, vmem_limit_bytes=VMEM_LIMIT)


def _head_norm_rope(z, gain, cos, sin, gmat, first_half):
    sq = z * z
    hi = sq.astype(BF16)
    lo = (sq - hi.astype(F32)).astype(BF16)
    ssum = _dot(hi, gmat) + _dot(lo, gmat)
    zn = z * lax.rsqrt(ssum * (1.0 / HEAD_DIM) + RMS_EPS) * gain
    rot = jnp.where(first_half, pltpu.roll(zn, LANES - HEAD_DIM // 2, 1), pltpu.roll(zn, HEAD_DIM // 2, 1))
    return zn * cos + rot * sin


def _qkv_kernel(x_ref, g1_ref, w_ref, cos_ref, sin_ref, qg_ref, kg_ref,
                xn_ref, q_ref, k_ref, v_ref):
    tm = x_ref.shape[0]
    xn = _rms_rows(x_ref[...], g1_ref[...]).astype(BF16)
    xn_ref[...] = xn
    lane = lax.broadcasted_iota(jnp.int32, (tm, LANES), 1)
    first_half = (lane % HEAD_DIM) < (HEAD_DIM // 2)
    r = lax.broadcasted_iota(jnp.int32, (LANES, LANES), 0) // HEAD_DIM
    c = lax.broadcasted_iota(jnp.int32, (LANES, LANES), 1) // HEAD_DIM
    gmat = jnp.where(r == c, 1.0, 0.0).astype(BF16)
    cos = cos_ref[...]
    sin = sin_ref[...]
    qg = qg_ref[...]
    kg = kg_ref[...]
    scale = HEAD_DIM ** -0.5
    for s in range(ATTN_W // LANES):
        z = _dot(xn, w_ref[:, s * LANES:(s + 1) * LANES])
        q = _head_norm_rope(z, qg, cos, sin, gmat, first_half)
        q_ref[:, s * LANES:(s + 1) * LANES] = (q * scale).astype(BF16)
    for s in range(KV_W // LANES):
        z = _dot(xn, w_ref[:, ATTN_W + s * LANES:ATTN_W + (s + 1) * LANES])
        k_ref[:, s * LANES:(s + 1) * LANES] = _head_norm_rope(z, kg, cos, sin, gmat, first_half)
    for s in range(KV_W // LANES):
        v_ref[:, s * LANES:(s + 1) * LANES] = _dot(
            xn, w_ref[:, ATTN_W + KV_W + s * LANES:ATTN_W + KV_W + (s + 1) * LANES])


def _qkv_call(l, x, g1, w_qkv, cos_t, sin_t, qg, kg, tm, n_pos_blocks):
    T = x.shape[0]
    row = lambda i: (i, 0)
    layer = lambda i: (l, 0, 0)
    pos = lambda i: (i % n_pos_blocks, 0)
    return pl.pallas_call(
        _qkv_kernel,
        grid=(T // tm,),
        in_specs=[
            pl.BlockSpec((tm, D_MODEL), row),
            pl.BlockSpec((None, 1, D_MODEL), layer),
            pl.BlockSpec((None, D_MODEL, QKV_W), layer),
            pl.BlockSpec((tm, LANES), pos),
            pl.BlockSpec((tm, LANES), pos),
            pl.BlockSpec((None, 1, LANES), layer),
            pl.BlockSpec((None, 1, LANES), layer),
        ],
        out_specs=[
            pl.BlockSpec((tm, D_MODEL), row),
            pl.BlockSpec((tm, ATTN_W), row),
            pl.BlockSpec((tm, KV_W), row),
            pl.BlockSpec((tm, KV_W), row),
        ],
        out_shape=[
            jax.ShapeDtypeStruct((T, D_MODEL), BF16),
            jax.ShapeDtypeStruct((T, ATTN_W), BF16),
            jax.ShapeDtypeStruct((T, KV_W), F32),
            jax.ShapeDtypeStruct((T, KV_W), F32),
        ],
        compiler_params=_params("parallel"),
        name="qkv",
    )(x, g1, w_qkv, cos_t, sin_t, qg, kg)


def _uv_project(xn_ref, w_ref, lng_ref, lnb_ref, u_scr, vs_scr):
    xn = xn_ref[...]
    cw = 2 * LANES
    for c in range(SGU_W // cw):
        u_scr[:, c * cw:(c + 1) * cw] = _gelu(_dot(xn, w_ref[:, c * cw:(c + 1) * cw]))
    for c in range(SGU_W // cw):
        vs_scr[:, c * cw:(c + 1) * cw] = _gelu(_dot(xn, w_ref[:, SGU_W + c * cw:SGU_W + (c + 1) * cw]))
    vs = vs_scr[...]
    mu = jnp.mean(vs, axis=-1, keepdims=True)
    d = vs - mu
    var = jnp.mean(d * d, axis=-1, keepdims=True)
    vs_scr[...] = d * lax.rsqrt(var + LN_EPS) * lng_ref[...] + lnb_ref[...]


def _sgu_prompt_kernel(xn_ref, w_ref, lng_ref, lnb_ref, ws_ref, bs_ref, m_ref, u_scr, vs_scr):
    tm = xn_ref.shape[0]
    _uv_project(xn_ref, w_ref, lng_ref, lnb_ref, u_scr, vs_scr)
    t = lax.broadcasted_iota(jnp.int32, (SGU_CHUNK, SGU_CHUNK), 0)
    s = lax.broadcasted_iota(jnp.int32, (SGU_CHUNK, SGU_CHUNK), 1)
    causal = t >= s
    for g in range(SGU_GROUPS):
        wg = jnp.where(causal, ws_ref[g], 0.0).astype(BF16)
        bias = bs_ref[:, g:g + 1]
        cols = slice(g * SGU_CH, (g + 1) * SGU_CH)
        for n in range(tm // SGU_CHUNK):
            rows = slice(n * SGU_CHUNK, (n + 1) * SGU_CHUNK)
            mixed = _dot(wg, vs_scr[rows, cols].astype(BF16)) + bias
            m_ref[rows, cols] = (u_scr[rows, cols] * mixed).astype(BF16)


def _sgu_prompt_call(l, xn, w_uv, lng, lnb, ws, bs_t, tm):
    T = xn.shape[0]
    row = lambda i: (i, 0)
    layer = lambda i: (l, 0, 0)
    return pl.pallas_call(
        _sgu_prompt_kernel,
        grid=(T // tm,),
        in_specs=[
            pl.BlockSpec((tm, D_MODEL), row),
            pl.BlockSpec((None, D_MODEL, 2 * SGU_W), layer),
            pl.BlockSpec((None, 1, SGU_W), layer),
            pl.BlockSpec((None, 1, SGU_W), layer),
            pl.BlockSpec((None, SGU_GROUPS, SGU_CHUNK, SGU_CHUNK), lambda i: (l, 0, 0, 0)),
            pl.BlockSpec((None, SGU_CHUNK, SGU_GROUPS), layer),
        ],
        out_specs=pl.BlockSpec((tm, SGU_W), row),
        out_shape=jax.ShapeDtypeStruct((T, SGU_W), BF16),
        scratch_shapes=[pltpu.VMEM((tm, SGU_W), F32), pltpu.VMEM((tm, SGU_W), F32)],
        compiler_params=_params("parallel"),
        name="sgu_prompt",
    )(xn, w_uv, lng, lnb, ws, bs_t)


def _sgu_sample_kernel(ws_ref, bs_ref, xn_ref, w_ref, lng_ref, lnb_ref, m_ref, vsn_ref, u_scr, g_scr,
                       *, l, seq):
    nseq = xn_ref.shape[0] // seq
    _uv_project(xn_ref, w_ref, lng_ref, lnb_ref, u_scr, vsn_ref)
    for g in range(SGU_GROUPS):
        cols = slice(g * SGU_CH, (g + 1) * SGU_CH)
        g_scr[0] = vsn_ref[:, cols]
        g_scr[1] = u_scr[:, cols]
        for t in range(seq):
            step_t = pl.ds(t, nseq, stride=seq)
            acc = jnp.full((nseq, SGU_CH), bs_ref[l, g * seq + t], F32)
            for s in range(t + 1):
                acc = acc + ws_ref[l, (g * seq + t) * seq + s] * g_scr[0, pl.ds(s, nseq, stride=seq), :]
            g_scr[2, step_t, :] = g_scr[1, step_t, :] * acc
        m_ref[:, cols] = g_scr[2].astype(BF16)


def _sgu_sample_call(l, xn, w_uv, lng, lnb, ws_flat, bs_flat, seq):
    T = xn.shape[0]
    const = lambda i: (0, 0)
    layer = lambda i: (l, 0, 0)
    smem = pl.BlockSpec(memory_space=pltpu.SMEM)
    return pl.pallas_call(
        functools.partial(_sgu_sample_kernel, l=l, seq=seq),
        grid=(1,),
        in_specs=[
            smem,
            smem,
            pl.BlockSpec((T, D_MODEL), const),
            pl.BlockSpec((None, D_MODEL, 2 * SGU_W), layer),
            pl.BlockSpec((None, 1, SGU_W), layer),
            pl.BlockSpec((None, 1, SGU_W), layer),
        ],
        out_specs=[pl.BlockSpec((T, SGU_W), const), pl.BlockSpec((T, SGU_W), const)],
        out_shape=[jax.ShapeDtypeStruct((T, SGU_W), BF16), jax.ShapeDtypeStruct((T, SGU_W), F32)],
        scratch_shapes=[pltpu.VMEM((T, SGU_W), F32), pltpu.VMEM((3, T, SGU_CH), F32)],
        compiler_params=_params("arbitrary"),
        name="sgu_sample",
    )(ws_flat, bs_flat, xn, w_uv, lng, lnb)


def _softmax_sink(s, sink):
    m = jnp.maximum(jnp.max(s, axis=-1, keepdims=True), sink)
    p = jnp.exp(s - m)
    return p / (jnp.sum(p, axis=-1, keepdims=True) + jnp.exp(sink - m))


def _attn_prompt_kernel(sink_ref, q_ref, kp_ref, kc_ref, vp_ref, vc_ref, o_ref, *, l):
    blk = q_ref.shape[0]
    has_prev = pl.program_id(1) > 0
    qi = lax.broadcasted_iota(jnp.int32, (blk, 2 * blk), 0)
    kj = lax.broadcasted_iota(jnp.int32, (blk, 2 * blk), 1)
    diff = qi + blk - kj
    mask = (diff >= 0) & (diff < WINDOW) & ((kj >= blk) | has_prev)
    for g in range(N_KV_HEADS):
        cols = slice(g * HEAD_DIM, (g + 1) * HEAD_DIM)
        kg = jnp.concatenate([kp_ref[:, cols], kc_ref[:, cols]], axis=0).astype(BF16)
        vg = jnp.concatenate([vp_ref[:, cols], vc_ref[:, cols]], axis=0).astype(BF16)
        for j in range(Q_GROUP):
            h = g * Q_GROUP + j
            qh = q_ref[:, h * HEAD_DIM:(h + 1) * HEAD_DIM]
            s = lax.dot_general(qh, kg, (((1,), (1,)), ((), ())), preferred_element_type=F32)
            p = _softmax_sink(jnp.where(mask, s, NEG_INF), sink_ref[l, h])
            o_ref[:, h * HEAD_DIM:(h + 1) * HEAD_DIM] = _dot(p.astype(BF16), vg).astype(BF16)


def _attn_prompt_call(l, sinks, q, k, v, batch, seq):
    nb = seq // WINDOW
    cur = lambda b, i: (b * nb + i, 0)
    prev = lambda b, i: (b * nb + jnp.maximum(i - 1, 0), 0)
    return pl.pallas_call(
        functools.partial(_attn_prompt_kernel, l=l),
        grid=(batch, nb),
        in_specs=[
            pl.BlockSpec(memory_space=pltpu.SMEM),
            pl.BlockSpec((WINDOW, ATTN_W), cur),
            pl.BlockSpec((WINDOW, KV_W), prev),
            pl.BlockSpec((WINDOW, KV_W), cur),
            pl.BlockSpec((WINDOW, KV_W), prev),
            pl.BlockSpec((WINDOW, KV_W), cur),
        ],
        out_specs=pl.BlockSpec((WINDOW, ATTN_W), cur),
        out_shape=jax.ShapeDtypeStruct((batch * seq, ATTN_W), BF16),
        compiler_params=_params("parallel", "parallel"),
        name="attn_prompt",
    )(sinks, q, k, k, v, v)


def _attn_sample_kernel(sink_ref, q_ref, kn_ref, vn_ref, ck_ref, cv_ref, o_ref, nk_ref, nv_ref, *, l, seq):
    nseq = q_ref.shape[0] // seq
    nq = nseq * seq
    rows = Q_GROUP * nq
    ri = lax.broadcasted_iota(jnp.int32, (rows, nseq * WINDOW), 0) % nq
    ci = lax.broadcasted_iota(jnp.int32, (rows, nseq * WINDOW), 1)
    mask_c = (ri // seq == ci // WINDOW) & (ci % WINDOW > ri % seq)
    ri = lax.broadcasted_iota(jnp.int32, (rows, nq), 0) % nq
    ci = lax.broadcasted_iota(jnp.int32, (rows, nq), 1)
    mask_n = (ri // seq == ci // seq) & (ci % seq <= ri % seq)
    dn = (((1,), (1,)), ((), ()))
    for g in range(N_KV_HEADS):
        cols = slice(g * HEAD_DIM, (g + 1) * HEAD_DIM)
        heads = range(g * Q_GROUP, (g + 1) * Q_GROUP)
        qg = jnp.concatenate([q_ref[:, h * HEAD_DIM:(h + 1) * HEAD_DIM] for h in heads], axis=0)
        sink = jnp.concatenate([jnp.full((nq, 1), sink_ref[l, h], F32) for h in heads], axis=0)
        s_c = lax.dot_general(qg, ck_ref[:, cols].astype(BF16), dn, preferred_element_type=F32)
        s_n = lax.dot_general(qg, kn_ref[:, cols].astype(BF16), dn, preferred_element_type=F32)
        s_c = jnp.where(mask_c, s_c, NEG_INF)
        s_n = jnp.where(mask_n, s_n, NEG_INF)
        m = jnp.maximum(jnp.maximum(jnp.max(s_c, axis=-1, keepdims=True),
                                    jnp.max(s_n, axis=-1, keepdims=True)), sink)
        p_c = jnp.exp(s_c - m)
        p_n = jnp.exp(s_n - m)
        denom = (jnp.sum(p_c, axis=-1, keepdims=True) + jnp.sum(p_n, axis=-1, keepdims=True)
                 + jnp.exp(sink - m))
        o = (_dot((p_c / denom).astype(BF16), cv_ref[:, cols].astype(BF16))
             + _dot((p_n / denom).astype(BF16), vn_ref[:, cols].astype(BF16)))
        for j, h in enumerate(heads):
            o_ref[:, h * HEAD_DIM:(h + 1) * HEAD_DIM] = o[j * nq:(j + 1) * nq].astype(BF16)
    for b in range(nseq):
        nk_ref[b * WINDOW:(b + 1) * WINDOW - seq, :] = ck_ref[b * WINDOW + seq:(b + 1) * WINDOW, :]
        nv_ref[b * WINDOW:(b + 1) * WINDOW - seq, :] = cv_ref[b * WINDOW + seq:(b + 1) * WINDOW, :]
        nk_ref[(b + 1) * WINDOW - seq:(b + 1) * WINDOW, :] = kn_ref[b * seq:(b + 1) * seq, :]
        nv_ref[(b + 1) * WINDOW - seq:(b + 1) * WINDOW, :] = vn_ref[b * seq:(b + 1) * seq, :]


def _attn_sample_call(l, sinks, q, k, v, ck, cv, seq, seq_block):
    T = q.shape[0]
    nseq = T // seq
    row = lambda i: (i, 0)
    cache = lambda i: (l, i, 0)
    return pl.pallas_call(
        functools.partial(_attn_sample_kernel, l=l, seq=seq),
        grid=(nseq // seq_block,),
        in_specs=[
            pl.BlockSpec(memory_space=pltpu.SMEM),
            pl.BlockSpec((seq_block * seq, ATTN_W), row),
            pl.BlockSpec((seq_block * seq, KV_W), row),
            pl.BlockSpec((seq_block * seq, KV_W), row),
            pl.BlockSpec((None, seq_block * WINDOW, KV_W), cache),
            pl.BlockSpec((None, seq_block * WINDOW, KV_W), cache),
        ],
        out_specs=[
            pl.BlockSpec((seq_block * seq, ATTN_W), row),
            pl.BlockSpec((seq_block * WINDOW, KV_W), row),
            pl.BlockSpec((seq_block * WINDOW, KV_W), row),
        ],
        out_shape=[
            jax.ShapeDtypeStruct((T, ATTN_W), BF16),
            jax.ShapeDtypeStruct((nseq * WINDOW, KV_W), F32),
            jax.ShapeDtypeStruct((nseq * WINDOW, KV_W), F32),
        ],
        compiler_params=_params("parallel"),
        name="attn_sample",
    )(sinks, q, k, v, ck, cv)


def _merge_kernel(x_ref, xn_ref, a_ref, m_ref, wga_ref, wgm_ref, wau_ref, wsu_ref, wout_ref,
                  h_ref, merged_scr):
    j = pl.program_id(1)
    xn = xn_ref[...]
    ga = jax.nn.sigmoid(_dot(xn, wga_ref[...]))
    gm = jax.nn.sigmoid(_dot(xn, wgm_ref[...]))
    merged = ga * _dot(a_ref[...], wau_ref[...]) + gm * _dot(m_ref[...], wsu_ref[...])
    merged_scr[j] = merged.astype(BF16)

    @pl.when(j == pl.num_programs(1) - 1)
    def _():
        tn = merged_scr.shape[2]
        acc = x_ref[...]
        for c in range(merged_scr.shape[0]):
            acc = acc + _dot(merged_scr[c], wout_ref[c * tn:(c + 1) * tn, :])
        h_ref[...] = acc


def _merge_call(l, x, xn, a, m, w_gate, w_au, w_su, w_out, tm, tn):
    T = x.shape[0]
    nj = D_MODEL // tn
    row = lambda i, j: (i, 0)
    return pl.pallas_call(
        _merge_kernel,
        grid=(T // tm, nj),
        in_specs=[
            pl.BlockSpec((tm, D_MODEL), row),
            pl.BlockSpec((tm, D_MODEL), row),
            pl.BlockSpec((tm, ATTN_W), row),
            pl.BlockSpec((tm, SGU_W), row),
            pl.BlockSpec((None, D_MODEL, tn), lambda i, j: (l, 0, j)),
            pl.BlockSpec((None, D_MODEL, tn), lambda i, j: (l, 0, nj + j)),
            pl.BlockSpec((None, ATTN_W, tn), lambda i, j: (l, 0, j)),
            pl.BlockSpec((None, SGU_W, tn), lambda i, j: (l, 0, j)),
            pl.BlockSpec((None, D_MODEL, D_MODEL), lambda i, j: (l, 0, 0)),
        ],
        out_specs=pl.BlockSpec((tm, D_MODEL), row),
        out_shape=jax.ShapeDtypeStruct((T, D_MODEL), F32),
        scratch_shapes=[pltpu.VMEM((nj, tm, tn), BF16)],
        compiler_params=_params("parallel", "arbitrary"),
        name="merge",
    )(x, xn, a, m, w_gate, w_gate, w_au, w_su, w_out)


def _ffn_kernel(h_ref, g2_ref, w1_ref, w2_ref, y_ref, hn_scr):
    @pl.when(pl.program_id(1) == 0)
    def _():
        h = h_ref[...]
        hn_scr[...] = _rms_rows(h, g2_ref[...]).astype(BF16)
        y_ref[...] = h

    a = jnp.maximum(_dot(hn_scr[...], w1_ref[...]), 0.0)
    y_ref[...] += _dot((a * a).astype(BF16), w2_ref[...])


def _ffn_call(l, h, g2, w1, w2, tm, tf):
    T = h.shape[0]
    row = lambda i, f: (i, 0)
    return pl.pallas_call(
        _ffn_kernel,
        grid=(T // tm, D_FF // tf),
        in_specs=[
            pl.BlockSpec((tm, D_MODEL), row),
            pl.BlockSpec((None, 1, D_MODEL), lambda i, f: (l, 0, 0)),
            pl.BlockSpec((None, D_MODEL, tf), lambda i, f: (l, 0, f)),
            pl.BlockSpec((None, tf, D_MODEL), lambda i, f: (l, f, 0)),
        ],
        out_specs=pl.BlockSpec((tm, D_MODEL), row),
        out_shape=jax.ShapeDtypeStruct((T, D_MODEL), F32),
        scratch_shapes=[pltpu.VMEM((tm, D_MODEL), BF16)],
        compiler_params=_params("parallel", "arbitrary"),
        name="ffn",
    )(h, g2, w1, w2)


def _rope_tables(pos):
    half = HEAD_DIM // 2
    inv = jnp.power(jnp.float32(ROPE_THETA), -jnp.arange(half, dtype=jnp.float32) / half)
    ang = pos.astype(jnp.float32)[:, None] * inv[None, :]
    cos = jnp.cos(ang)
    sin = jnp.sin(ang)
    reps = LANES // HEAD_DIM
    return jnp.tile(cos, (1, 2 * reps)), jnp.tile(jnp.concatenate([-sin, sin], axis=-1), (1, reps))


def kernel(x_prompt, x_sample, cache_k, cache_v, norm1_g, w_in, q_norm_g, k_norm_g, attn_sinks,
           sgu_ln_g, sgu_ln_b, sgu_w, sgu_b, w_attn_up, w_sgu_up, w_out, norm2_g, w_ff1, w_ff2):
    batch, seq, _ = x_prompt.shape
    dec_batch, dec_seq, _ = x_sample.shape
    tp = batch * seq
    ts = dec_batch * dec_seq
    tm = 512
    tn = 512
    assert seq % tm == 0 and tm % SGU_CHUNK == 0 and seq % WINDOW == 0

    cos_p, sin_p = _rope_tables(jnp.arange(seq))
    cos_s, sin_s = _rope_tables(jnp.tile(PAST_LEN + jnp.arange(dec_seq), dec_batch))

    w_qkv = w_in[:, :, :QKV_W].astype(BF16)
    w_uv = w_in[:, :, UV_OFF:GATE_OFF].astype(BF16)
    w_gate = w_in[:, :, GATE_OFF:].astype(BF16)
    w_au = w_attn_up.astype(BF16)
    w_su = w_sgu_up.astype(BF16)
    w_o = w_out.astype(BF16)
    w1 = w_ff1.astype(BF16)
    w2 = w_ff2.astype(BF16)

    reps = LANES // HEAD_DIM
    g1 = norm1_g.reshape(DEPTH, 1, D_MODEL)
    g2 = norm2_g.reshape(DEPTH, 1, D_MODEL)
    qg = jnp.tile(q_norm_g, (1, reps)).reshape(DEPTH, 1, LANES)
    kg = jnp.tile(k_norm_g, (1, reps)).reshape(DEPTH, 1, LANES)
    lng = sgu_ln_g.reshape(DEPTH, 1, SGU_W)
    lnb = sgu_ln_b.reshape(DEPTH, 1, SGU_W)
    bs_t = jnp.swapaxes(sgu_b, 1, 2)
    ws_s = sgu_w[:, :, :dec_seq, :dec_seq].reshape(DEPTH, -1)
    bs_s = sgu_b[:, :, :dec_seq].reshape(DEPTH, -1)
    ck = cache_k.reshape(DEPTH, dec_batch * WINDOW, KV_W)
    cv = cache_v.reshape(DEPTH, dec_batch * WINDOW, KV_W)

    xp = x_prompt.reshape(tp, D_MODEL)
    xs = x_sample.reshape(ts, D_MODEL)
    kp_l, vp_l, ks_l, vs_l, sv_l = [], [], [], [], []
    for l in range(DEPTH):
        xn, q, k, v = _qkv_call(l, xp, g1, w_qkv, cos_p, sin_p, qg, kg, tm, seq // tm)
        m = _sgu_prompt_call(l, xn, w_uv, lng, lnb, sgu_w, bs_t, tm)
        a = _attn_prompt_call(l, attn_sinks, q, k, v, batch, seq)
        h = _merge_call(l, xp, xn, a, m, w_gate, w_au, w_su, w_o, tm, tn)
        xp = _ffn_call(l, h, g2, w1, w2, tm, tn)
        kp_l.append(k.reshape(batch, seq, N_KV_HEADS, HEAD_DIM)[:, -WINDOW:])
        vp_l.append(v.reshape(batch, seq, N_KV_HEADS, HEAD_DIM)[:, -WINDOW:])

        xn, q, k, v = _qkv_call(l, xs, g1, w_qkv, cos_s, sin_s, qg, kg, ts, 1)
        m, vsn = _sgu_sample_call(l, xn, w_uv, lng, lnb, ws_s, bs_s, dec_seq)
        a, nk, nv = _attn_sample_call(l, attn_sinks, q, k, v, ck, cv, dec_seq, 8)
        h = _merge_call(l, xs, xn, a, m, w_gate, w_au, w_su, w_o, ts, tn)
        xs = _ffn_call(l, h, g2, w1, w2, ts, tn)
        ks_l.append(nk.reshape(dec_batch, WINDOW, N_KV_HEADS, HEAD_DIM))
        vs_l.append(nv.reshape(dec_batch, WINDOW, N_KV_HEADS, HEAD_DIM))
        sv_l.append(vsn.reshape(dec_batch, dec_seq, SGU_GROUPS, SGU_CH))

    return (xp.reshape(batch, seq, D_MODEL), xs.reshape(dec_batch, dec_seq, D_MODEL),
            jnp.stack(kp_l), jnp.stack(vp_l), jnp.stack(ks_l), jnp.stack(vs_l), jnp.stack(sv_l))
```

```python
import functools
import math

import jax
import jax.numpy as jnp
import numpy as np
from jax import lax
from jax.experimental import pallas as pl
from jax.experimental.pallas import tpu as pltpu

D_MODEL = 2048
DEPTH = 4
PAST_LEN = 16384
HEAD_DIM = 64
N_HEADS = 16
N_KV_HEADS = 4
Q_GROUP = 4
ATTN_W = N_HEADS * HEAD_DIM
KV_W = N_KV_HEADS * HEAD_DIM
WINDOW = 128
ROPE_THETA = 10000.0
SGU_CHUNK = 128
SGU_W = D_MODEL // 2
SGU_GROUPS = 8
SGU_CH = SGU_W // SGU_GROUPS
D_FF = 4 * D_MODEL
RMS_EPS = 1e-6
LN_EPS = 1e-5
NEG_INF = -1e30

QKV_W = ATTN_W + 2 * KV_W
UV_OFF = QKV_W
GATE_OFF = QKV_W + 2 * SGU_W

LANES = 128
VMEM_LIMIT = 56 * 1024 * 1024

BF16 = jnp.bfloat16
F32 = jnp.float32


def _dot(a, b):
    return jnp.dot(a, b, preferred_element_type=F32)


def _rms_rows(x, g):
    ms = jnp.mean(x * x, axis=-1, keepdims=True)
    return x * lax.rsqrt(ms + RMS_EPS) * g


def _gelu(x):
    return 0.5 * x * (1.0 + lax.erf(x * np.float32(math.sqrt(0.5))))


def _params(*sem):
    return pltpu.CompilerParams(dimension_semantics=sem, vmem_limit_bytes=VMEM_LIMIT)


def _head_norm_rope(z, gain, cos, sin, gmat, first_half):
    width = z.shape[1]
    sq = z * z
    hi = sq.astype(BF16)
    lo = (sq - hi.astype(F32)).astype(BF16)
    ssum = _dot(jnp.concatenate([hi, lo], axis=1), gmat)
    zn = z * lax.rsqrt(ssum * (1.0 / HEAD_DIM) + RMS_EPS) * gain
    rot = jnp.where(first_half, pltpu.roll(zn, width - HEAD_DIM // 2, 1), pltpu.roll(zn, HEAD_DIM // 2, 1))
    return zn * cos + rot * sin


def _qkv_kernel(x_ref, g1_ref, w_ref, cos_ref, sin_ref, qg_ref, kg_ref,
                xn_ref, q_ref, k_ref, v_ref):
    tm = x_ref.shape[0]
    cw = KV_W
    xn = _rms_rows(x_ref[...], g1_ref[...]).astype(BF16)
    xn_ref[...] = xn
    lane = lax.broadcasted_iota(jnp.int32, (tm, cw), 1)
    first_half = (lane % HEAD_DIM) < (HEAD_DIM // 2)
    r = (lax.broadcasted_iota(jnp.int32, (2 * cw, cw), 0) % cw) // HEAD_DIM
    c = lax.broadcasted_iota(jnp.int32, (2 * cw, cw), 1) // HEAD_DIM
    gmat = jnp.where(r == c, 1.0, 0.0).astype(BF16)
    cos = cos_ref[...]
    sin = sin_ref[...]
    scale = HEAD_DIM ** -0.5
    n_q = ATTN_W // cw

    def project(c):
        return _dot(xn, w_ref[:, c * cw:(c + 1) * cw])

    z = project(0)
    for c in range(n_q + 1):
        z_next = project(c + 1)
        if c < n_q:
            q = _head_norm_rope(z, qg_ref[...], cos, sin, gmat, first_half)
            q_ref[:, c * cw:(c + 1) * cw] = (q * scale).astype(BF16)
        else:
            k_ref[...] = _head_norm_rope(z, kg_ref[...], cos, sin, gmat, first_half)
        z = z_next
    v_ref[...] = z


def _qkv_call(l, x, g1, w_qkv, cos_t, sin_t, qg, kg, tm, n_pos_blocks):
    T = x.shape[0]
    row = lambda i: (i, 0)
    layer = lambda i: (l, 0, 0)
    pos = lambda i: (i % n_pos_blocks, 0)
    return pl.pallas_call(
        _qkv_kernel,
        grid=(T // tm,),
        in_specs=[
            pl.BlockSpec((tm, D_MODEL), row),
            pl.BlockSpec((None, 1, D_MODEL), layer),
            pl.BlockSpec((None, D_MODEL, QKV_W), layer),
            pl.BlockSpec((tm, KV_W), pos),
            pl.BlockSpec((tm, KV_W), pos),
            pl.BlockSpec((None, 1, KV_W), layer),
            pl.BlockSpec((None, 1, KV_W), layer),
        ],
        out_specs=[
            pl.BlockSpec((tm, D_MODEL), row),
            pl.BlockSpec((tm, ATTN_W), row),
            pl.BlockSpec((tm, KV_W), row),
            pl.BlockSpec((tm, KV_W), row),
        ],
        out_shape=[
            jax.ShapeDtypeStruct((T, D_MODEL), BF16),
            jax.ShapeDtypeStruct((T, ATTN_W), BF16),
            jax.ShapeDtypeStruct((T, KV_W), F32),
            jax.ShapeDtypeStruct((T, KV_W), F32),
        ],
        compiler_params=_params("parallel"),
        name="qkv",
    )(x, g1, w_qkv, cos_t, sin_t, qg, kg)


def _uv_project(xn_ref, w_ref, lng_ref, lnb_ref, u_scr, vs_scr):
    xn = xn_ref[...]
    cw = 2 * LANES
    for c in range(SGU_W // cw):
        u_scr[:, c * cw:(c + 1) * cw] = _gelu(_dot(xn, w_ref[:, c * cw:(c + 1) * cw]))
    for c in range(SGU_W // cw):
        vs_scr[:, c * cw:(c + 1) * cw] = _gelu(_dot(xn, w_ref[:, SGU_W + c * cw:SGU_W + (c + 1) * cw]))
    vs = vs_scr[...]
    mu = jnp.mean(vs, axis=-1, keepdims=True)
    d = vs - mu
    var = jnp.mean(d * d, axis=-1, keepdims=True)
    vs_scr[...] = d * lax.rsqrt(var + LN_EPS) * lng_ref[...] + lnb_ref[...]


def _sgu_prompt_kernel(xn_ref, w_ref, lng_ref, lnb_ref, ws_ref, bs_ref, m_ref, u_scr, vs_scr):
    tm = xn_ref.shape[0]
    _uv_project(xn_ref, w_ref, lng_ref, lnb_ref, u_scr, vs_scr)
    t = lax.broadcasted_iota(jnp.int32, (SGU_CHUNK, SGU_CHUNK), 0)
    s = lax.broadcasted_iota(jnp.int32, (SGU_CHUNK, SGU_CHUNK), 1)
    causal = t >= s
    for g in range(SGU_GROUPS):
        wg = jnp.where(causal, ws_ref[g], 0.0).astype(BF16)
        bias = bs_ref[:, g:g + 1]
        cols = slice(g * SGU_CH, (g + 1) * SGU_CH)
        for n in range(tm // SGU_CHUNK):
            rows = slice(n * SGU_CHUNK, (n + 1) * SGU_CHUNK)
            mixed = _dot(wg, vs_scr[rows, cols].astype(BF16)) + bias
            m_ref[rows, cols] = (u_scr[rows, cols] * mixed).astype(BF16)


def _sgu_prompt_call(l, xn, w_uv, lng, lnb, ws, bs_t, tm):
    T = xn.shape[0]
    row = lambda i: (i, 0)
    layer = lambda i: (l, 0, 0)
    return pl.pallas_call(
        _sgu_prompt_kernel,
        grid=(T // tm,),
        in_specs=[
            pl.BlockSpec((tm, D_MODEL), row),
            pl.BlockSpec((None, D_MODEL, 2 * SGU_W), layer),
            pl.BlockSpec((None, 1, SGU_W), layer),
            pl.BlockSpec((None, 1, SGU_W), layer),
            pl.BlockSpec((None, SGU_GROUPS, SGU_CHUNK, SGU_CHUNK), lambda i: (l, 0, 0, 0)),
            pl.BlockSpec((None, SGU_CHUNK, SGU_GROUPS), layer),
        ],
        out_specs=pl.BlockSpec((tm, SGU_W), row),
        out_shape=jax.ShapeDtypeStruct((T, SGU_W), BF16),
        scratch_shapes=[pltpu.VMEM((tm, SGU_W), F32), pltpu.VMEM((tm, SGU_W), F32)],
        compiler_params=_params("parallel"),
        name="sgu_prompt",
    )(xn, w_uv, lng, lnb, ws, bs_t)


def _sgu_sample_kernel(ws_ref, bs_ref, xn_ref, w_ref, lng_ref, lnb_ref, m_ref, vsn_ref, u_scr, g_scr,
                       *, l, seq):
    nseq = xn_ref.shape[0] // seq
    _uv_project(xn_ref, w_ref, lng_ref, lnb_ref, u_scr, vsn_ref)
    for g in range(SGU_GROUPS):
        cols = slice(g * SGU_CH, (g + 1) * SGU_CH)
        g_scr[0] = vsn_ref[:, cols]
        g_scr[1] = u_scr[:, cols]
        for t in range(seq):
            step_t = pl.ds(t, nseq, stride=seq)
            acc = jnp.full((nseq, SGU_CH), bs_ref[l, g * seq + t], F32)
            for s in range(t + 1):
                acc = acc + ws_ref[l, (g * seq + t) * seq + s] * g_scr[0, pl.ds(s, nseq, stride=seq), :]
            g_scr[2, step_t, :] = g_scr[1, step_t, :] * acc
        m_ref[:, cols] = g_scr[2].astype(BF16)


def _sgu_sample_call(l, xn, w_uv, lng, lnb, ws_flat, bs_flat, seq):
    T = xn.shape[0]
    const = lambda i: (0, 0)
    layer = lambda i: (l, 0, 0)
    smem = pl.BlockSpec(memory_space=pltpu.SMEM)
    return pl.pallas_call(
        functools.partial(_sgu_sample_kernel, l=l, seq=seq),
        grid=(1,),
        in_specs=[
            smem,
            smem,
            pl.BlockSpec((T, D_MODEL), const),
            pl.BlockSpec((None, D_MODEL, 2 * SGU_W), layer),
            pl.BlockSpec((None, 1, SGU_W), layer),
            pl.BlockSpec((None, 1, SGU_W), layer),
        ],
        out_specs=[pl.BlockSpec((T, SGU_W), const), pl.BlockSpec((T, SGU_W), const)],
        out_shape=[jax.ShapeDtypeStruct((T, SGU_W), BF16), jax.ShapeDtypeStruct((T, SGU_W), F32)],
        scratch_shapes=[pltpu.VMEM((T, SGU_W), F32), pltpu.VMEM((3, T, SGU_CH), F32)],
        compiler_params=_params("arbitrary"),
        name="sgu_sample",
    )(ws_flat, bs_flat, xn, w_uv, lng, lnb)


def _attn_prompt_kernel(sink_ref, q_ref, kp_ref, kc_ref, vp_ref, vc_ref, o_ref, *, l):
    blk = WINDOW
    nq = Q_GROUP * blk
    first_tile = pl.program_id(1) == 0
    c = lax.broadcasted_iota(jnp.int32, (blk, nq), 0)
    r = lax.broadcasted_iota(jnp.int32, (blk, nq), 1) % blk
    from_prev = c > r
    k_all = jnp.concatenate([kp_ref[...], kc_ref[...]], axis=0).astype(BF16)
    vt_all = jnp.concatenate([vp_ref[...], vc_ref[...]], axis=0).T.astype(BF16)
    sinks = [jnp.concatenate([jnp.full((1, blk), sink_ref[l, g * Q_GROUP + j], F32) for j in range(Q_GROUP)],
                             axis=1) for g in range(N_KV_HEADS)]

    def scores(jb):
        out = []
        for g in range(N_KV_HEADS):
            qg = jnp.concatenate([q_ref[jb * blk:(jb + 1) * blk, h * HEAD_DIM:(h + 1) * HEAD_DIM]
                                  for h in range(g * Q_GROUP, (g + 1) * Q_GROUP)], axis=0)
            kg = k_all[jb * blk:(jb + 2) * blk, g * HEAD_DIM:(g + 1) * HEAD_DIM]
            out.append(lax.dot_general(kg, qg, (((1,), (1,)), ((), ())), preferred_element_type=F32))
        return out

    nblk = q_ref.shape[0] // blk
    s2 = scores(0)
    for jb in range(nblk):
        s2_next = scores(jb + 1) if jb + 1 < nblk else None
        for g in range(N_KV_HEADS):
            s_prev = s2[g][:blk]
            if jb == 0:
                s_prev = jnp.where(first_tile, NEG_INF, s_prev)
            s = jnp.where(from_prev, s_prev, s2[g][blk:])
            m = jnp.maximum(jnp.max(s, axis=0, keepdims=True), sinks[g])
            p = jnp.exp(s - m)
            p = p / (jnp.sum(p, axis=0, keepdims=True) + jnp.exp(sinks[g] - m))
            p2 = jnp.concatenate([jnp.where(from_prev, p, 0.0), jnp.where(from_prev, 0.0, p)], axis=0)
            vt = vt_all[g * HEAD_DIM:(g + 1) * HEAD_DIM, jb * blk:(jb + 2) * blk]
            o = _dot(vt, p2.astype(BF16)).T
            for j in range(Q_GROUP):
                h = g * Q_GROUP + j
                o_ref[jb * blk:(jb + 1) * blk, h * HEAD_DIM:(h + 1) * HEAD_DIM] = (
                    o[j * blk:(j + 1) * blk].astype(BF16))
        s2 = s2_next


def _attn_prompt_call(l, sinks, q, k, v, batch, seq, tq):
    nt = seq // tq
    blocks_per_tile = tq // WINDOW
    cur = lambda b, i: (b * nt + i, 0)
    prev = lambda b, i: (b * (seq // WINDOW) + jnp.maximum(i * blocks_per_tile - 1, 0), 0)
    return pl.pallas_call(
        functools.partial(_attn_prompt_kernel, l=l),
        grid=(batch, nt),
        in_specs=[
            pl.BlockSpec(memory_space=pltpu.SMEM),
            pl.BlockSpec((tq, ATTN_W), cur),
            pl.BlockSpec((WINDOW, KV_W), prev),
            pl.BlockSpec((tq, KV_W), cur),
            pl.BlockSpec((WINDOW, KV_W), prev),
            pl.BlockSpec((tq, KV_W), cur),
        ],
        out_specs=pl.BlockSpec((tq, ATTN_W), cur),
        out_shape=jax.ShapeDtypeStruct((batch * seq, ATTN_W), BF16),
        compiler_params=_params("parallel", "parallel"),
        name="attn_prompt",
    )(sinks, q, k, k, v, v)


def _attn_sample_kernel(sink_ref, q_ref, kn_ref, vn_ref, ck_ref, cv_ref, o_ref, nk_ref, nv_ref, *, l, seq):
    nseq = q_ref.shape[0] // seq
    nq = nseq * seq
    rows = Q_GROUP * nq
    ri = lax.broadcasted_iota(jnp.int32, (rows, nseq * WINDOW), 0) % nq
    ci = lax.broadcasted_iota(jnp.int32, (rows, nseq * WINDOW), 1)
    mask_c = (ri // seq == ci // WINDOW) & (ci % WINDOW > ri % seq)
    ri = lax.broadcasted_iota(jnp.int32, (rows, nq), 0) % nq
    ci = lax.broadcasted_iota(jnp.int32, (rows, nq), 1)
    mask_n = (ri // seq == ci // seq) & (ci % seq <= ri % seq)
    dn = (((1,), (1,)), ((), ()))
    for g in range(N_KV_HEADS):
        cols = slice(g * HEAD_DIM, (g + 1) * HEAD_DIM)
        heads = range(g * Q_GROUP, (g + 1) * Q_GROUP)
        qg = jnp.concatenate([q_ref[:, h * HEAD_DIM:(h + 1) * HEAD_DIM] for h in heads], axis=0)
        sink = jnp.concatenate([jnp.full((nq, 1), sink_ref[l, h], F32) for h in heads], axis=0)
        s_c = lax.dot_general(qg, ck_ref[:, cols].astype(BF16), dn, preferred_element_type=F32)
        s_n = lax.dot_general(qg, kn_ref[:, cols].astype(BF16), dn, preferred_element_type=F32)
        s_c = jnp.where(mask_c, s_c, NEG_INF)
        s_n = jnp.where(mask_n, s_n, NEG_INF)
        m = jnp.maximum(jnp.maximum(jnp.max(s_c, axis=-1, keepdims=True),
                                    jnp.max(s_n, axis=-1, keepdims=True)), sink)
        p_c = jnp.exp(s_c - m)
        p_n = jnp.exp(s_n - m)
        denom = (jnp.sum(p_c, axis=-1, keepdims=True) + jnp.sum(p_n, axis=-1, keepdims=True)
                 + jnp.exp(sink - m))
        o = (_dot((p_c / denom).astype(BF16), cv_ref[:, cols].astype(BF16))
             + _dot((p_n / denom).astype(BF16), vn_ref[:, cols].astype(BF16)))
        for j, h in enumerate(heads):
            o_ref[:, h * HEAD_DIM:(h + 1) * HEAD_DIM] = o[j * nq:(j + 1) * nq].astype(BF16)
    for b in range(nseq):
        nk_ref[b * WINDOW:(b + 1) * WINDOW - seq, :] = ck_ref[b * WINDOW + seq:(b + 1) * WINDOW, :]
        nv_ref[b * WINDOW:(b + 1) * WINDOW - seq, :] = cv_ref[b * WINDOW + seq:(b + 1) * WINDOW, :]
        nk_ref[(b + 1) * WINDOW - seq:(b + 1) * WINDOW, :] = kn_ref[b * seq:(b + 1) * seq, :]
        nv_ref[(b + 1) * WINDOW - seq:(b + 1) * WINDOW, :] = vn_ref[b * seq:(b + 1) * seq, :]


def _attn_sample_call(l, sinks, q, k, v, ck, cv, seq, seq_block):
    T = q.shape[0]
    nseq = T // seq
    row = lambda i: (i, 0)
    cache = lambda i: (l, i, 0)
    return pl.pallas_call(
        functools.partial(_attn_sample_kernel, l=l, seq=seq),
        grid=(nseq // seq_block,),
        in_specs=[
            pl.BlockSpec(memory_space=pltpu.SMEM),
            pl.BlockSpec((seq_block * seq, ATTN_W), row),
            pl.BlockSpec((seq_block * seq, KV_W), row),
            pl.BlockSpec((seq_block * seq, KV_W), row),
            pl.BlockSpec((None, seq_block * WINDOW, KV_W), cache),
            pl.BlockSpec((None, seq_block * WINDOW, KV_W), cache),
        ],
        out_specs=[
            pl.BlockSpec((seq_block * seq, ATTN_W), row),
            pl.BlockSpec((seq_block * WINDOW, KV_W), row),
            pl.BlockSpec((seq_block * WINDOW, KV_W), row),
        ],
        out_shape=[
            jax.ShapeDtypeStruct((T, ATTN_W), BF16),
            jax.ShapeDtypeStruct((nseq * WINDOW, KV_W), F32),
            jax.ShapeDtypeStruct((nseq * WINDOW, KV_W), F32),
        ],
        compiler_params=_params("parallel"),
        name="attn_sample",
    )(sinks, q, k, v, ck, cv)


def _merge_kernel(x_ref, xn_ref, a_ref, m_ref, wga_ref, wgm_ref, wau_ref, wsu_ref, wout_ref,
                  h_ref, merged_scr):
    j = pl.program_id(1)
    xn = xn_ref[...]
    ga = jax.nn.sigmoid(_dot(xn, wga_ref[...]))
    gm = jax.nn.sigmoid(_dot(xn, wgm_ref[...]))
    merged = ga * _dot(a_ref[...], wau_ref[...]) + gm * _dot(m_ref[...], wsu_ref[...])
    merged_scr[j] = merged.astype(BF16)

    @pl.when(j == pl.num_programs(1) - 1)
    def _():
        tn = merged_scr.shape[2]
        acc = x_ref[...]
        for c in range(merged_scr.shape[0]):
            acc = acc + _dot(merged_scr[c], wout_ref[c * tn:(c + 1) * tn, :])
        h_ref[...] = acc


def _merge_call(l, x, xn, a, m, w_gate, w_au, w_su, w_out, tm, tn):
    T = x.shape[0]
    nj = D_MODEL // tn
    row = lambda i, j: (i, 0)
    return pl.pallas_call(
        _merge_kernel,
        grid=(T // tm, nj),
        in_specs=[
            pl.BlockSpec((tm, D_MODEL), row),
            pl.BlockSpec((tm, D_MODEL), row),
            pl.BlockSpec((tm, ATTN_W), row),
            pl.BlockSpec((tm, SGU_W), row),
            pl.BlockSpec((None, D_MODEL, tn), lambda i, j: (l, 0, j)),
            pl.BlockSpec((None, D_MODEL, tn), lambda i, j: (l, 0, nj + j)),
            pl.BlockSpec((None, ATTN_W, tn), lambda i, j: (l, 0, j)),
            pl.BlockSpec((None, SGU_W, tn), lambda i, j: (l, 0, j)),
            pl.BlockSpec((None, D_MODEL, D_MODEL), lambda i, j: (l, 0, 0)),
        ],
        out_specs=pl.BlockSpec((tm, D_MODEL), row),
        out_shape=jax.ShapeDtypeStruct((T, D_MODEL), F32),
        scratch_shapes=[pltpu.VMEM((nj, tm, tn), BF16)],
        compiler_params=_params("parallel", "arbitrary"),
        name="merge",
    )(x, xn, a, m, w_gate, w_gate, w_au, w_su, w_out)


def _ffn_kernel(h_ref, g2_ref, w1_ref, w2_ref, y_ref, hn_scr):
    @pl.when(pl.program_id(1) == 0)
    def _():
        h = h_ref[...]
        hn_scr[...] = _rms_rows(h, g2_ref[...]).astype(BF16)
        y_ref[...] = h

    half = h_ref.shape[0] // 2
    halves = (slice(0, half), slice(half, 2 * half))
    acts = []
    for rows in halves:
        a = jnp.maximum(_dot(hn_scr[rows, :], w1_ref[...]), 0.0)
        acts.append((a * a).astype(BF16))
    for rows, a in zip(halves, acts):
        y_ref[rows, :] += _dot(a, w2_ref[...])


def _ffn_call(l, h, g2, w1, w2, tm, tf):
    T = h.shape[0]
    row = lambda i, f: (i, 0)
    return pl.pallas_call(
        _ffn_kernel,
        grid=(T // tm, D_FF // tf),
        in_specs=[
            pl.BlockSpec((tm, D_MODEL), row),
            pl.BlockSpec((None, 1, D_MODEL), lambda i, f: (l, 0, 0)),
            pl.BlockSpec((None, D_MODEL, tf), lambda i, f: (l, 0, f)),
            pl.BlockSpec((None, tf, D_MODEL), lambda i, f: (l, f, 0)),
        ],
        out_specs=pl.BlockSpec((tm, D_MODEL), row),
        out_shape=jax.ShapeDtypeStruct((T, D_MODEL), F32),
        scratch_shapes=[pltpu.VMEM((tm, D_MODEL), BF16)],
        compiler_params=_params("parallel", "arbitrary"),
        name="ffn",
    )(h, g2, w1, w2)


def _rope_tables(pos):
    half = HEAD_DIM // 2
    inv = jnp.power(jnp.float32(ROPE_THETA), -jnp.arange(half, dtype=jnp.float32) / half)
    ang = pos.astype(jnp.float32)[:, None] * inv[None, :]
    cos = jnp.cos(ang)
    sin = jnp.sin(ang)
    return jnp.tile(cos, (1, 2 * N_KV_HEADS)), jnp.tile(jnp.concatenate([-sin, sin], axis=-1), (1, N_KV_HEADS))


def kernel(x_prompt, x_sample, cache_k, cache_v, norm1_g, w_in, q_norm_g, k_norm_g, attn_sinks,
           sgu_ln_g, sgu_ln_b, sgu_w, sgu_b, w_attn_up, w_sgu_up, w_out, norm2_g, w_ff1, w_ff2):
    batch, seq, _ = x_prompt.shape
    dec_batch, dec_seq, _ = x_sample.shape
    tp = batch * seq
    ts = dec_batch * dec_seq
    tm = 512
    tn = 512
    assert seq % tm == 0 and tm % SGU_CHUNK == 0 and seq % WINDOW == 0

    cos_p, sin_p = _rope_tables(jnp.arange(seq))
    cos_s, sin_s = _rope_tables(jnp.tile(PAST_LEN + jnp.arange(dec_seq), dec_batch))

    w_qkv = w_in[:, :, :QKV_W].astype(BF16)
    w_uv = w_in[:, :, UV_OFF:GATE_OFF].astype(BF16)
    w_gate = w_in[:, :, GATE_OFF:].astype(BF16)
    w_au = w_attn_up.astype(BF16)
    w_su = w_sgu_up.astype(BF16)
    w_o = w_out.astype(BF16)
    w1 = w_ff1.astype(BF16)
    w2 = w_ff2.astype(BF16)

    g1 = norm1_g.reshape(DEPTH, 1, D_MODEL)
    g2 = norm2_g.reshape(DEPTH, 1, D_MODEL)
    qg = jnp.tile(q_norm_g, (1, N_KV_HEADS)).reshape(DEPTH, 1, KV_W)
    kg = jnp.tile(k_norm_g, (1, N_KV_HEADS)).reshape(DEPTH, 1, KV_W)
    lng = sgu_ln_g.reshape(DEPTH, 1, SGU_W)
    lnb = sgu_ln_b.reshape(DEPTH, 1, SGU_W)
    bs_t = jnp.swapaxes(sgu_b, 1, 2)
    ws_s = sgu_w[:, :, :dec_seq, :dec_seq].reshape(DEPTH, -1)
    bs_s = sgu_b[:, :, :dec_seq].reshape(DEPTH, -1)
    ck = cache_k.reshape(DEPTH, dec_batch * WINDOW, KV_W)
    cv = cache_v.reshape(DEPTH, dec_batch * WINDOW, KV_W)

    xp = x_prompt.reshape(tp, D_MODEL)
    xs = x_sample.reshape(ts, D_MODEL)
    kp_l, vp_l, ks_l, vs_l, sv_l = [], [], [], [], []
    for l in range(DEPTH):
        xn, q, k, v = _qkv_call(l, xp, g1, w_qkv, cos_p, sin_p, qg, kg, tm, seq // tm)
        m = _sgu_prompt_call(l, xn, w_uv, lng, lnb, sgu_w, bs_t, tm)
        a = _attn_prompt_call(l, attn_sinks, q, k, v, batch, seq, tm)
        h = _merge_call(l, xp, xn, a, m, w_gate, w_au, w_su, w_o, tm, tn)
        xp = _ffn_call(l, h, g2, w1, w2, 2 * tm, tn)
        kp_l.append(k.reshape(batch, seq, N_KV_HEADS, HEAD_DIM)[:, -WINDOW:])
        vp_l.append(v.reshape(batch, seq, N_KV_HEADS, HEAD_DIM)[:, -WINDOW:])

        xn, q, k, v = _qkv_call(l, xs, g1, w_qkv, cos_s, sin_s, qg, kg, ts, 1)
        m, vsn = _sgu_sample_call(l, xn, w_uv, lng, lnb, ws_s, bs_s, dec_seq)
        a, nk, nv = _attn_sample_call(l, attn_sinks, q, k, v, ck, cv, dec_seq, 8)
        h = _merge_call(l, xs, xn, a, m, w_gate, w_au, w_su, w_o, ts, tn)
        xs = _ffn_call(l, h, g2, w1, w2, ts, tn)
        ks_l.append(nk.reshape(dec_batch, WINDOW, N_KV_HEADS, HEAD_DIM))
        vs_l.append(nv.reshape(dec_batch, WINDOW, N_KV_HEADS, HEAD_DIM))
        sv_l.append(vsn.reshape(dec_batch, dec_seq, SGU_GROUPS, SGU_CH))

    return (xp.reshape(batch, seq, D_MODEL), xs.reshape(dec_batch, dec_seq, D_MODEL),
            jnp.stack(kp_l), jnp.stack(vp_l), jnp.stack(ks_l), jnp.stack(vs_l), jnp.stack(sv_l))
```

```python
import functools
import math

import jax
import jax.numpy as jnp
import numpy as np
from jax import lax
from jax.experimental import pallas as pl
from jax.experimental.pallas import tpu as pltpu

D_MODEL = 2048
DEPTH = 4
PAST_LEN = 16384
HEAD_DIM = 64
N_HEADS = 16
N_KV_HEADS = 4
Q_GROUP = 4
ATTN_W = N_HEADS * HEAD_DIM
KV_W = N_KV_HEADS * HEAD_DIM
WINDOW = 128
ROPE_THETA = 10000.0
SGU_CHUNK = 128
SGU_W = D_MODEL // 2
SGU_GROUPS = 8
SGU_CH = SGU_W // SGU_GROUPS
D_FF = 4 * D_MODEL
RMS_EPS = 1e-6
LN_EPS = 1e-5
NEG_INF = -1e30

QKV_W = ATTN_W + 2 * KV_W
UV_OFF = QKV_W
GATE_OFF = QKV_W + 2 * SGU_W

LANES = 128
VMEM_LIMIT = 56 * 1024 * 1024

BF16 = jnp.bfloat16
F32 = jnp.float32


def _dot(a, b):
    return jnp.dot(a, b, preferred_element_type=F32)


def _rms_rows(x, g):
    ms = jnp.mean(x * x, axis=-1, keepdims=True)
    return x * lax.rsqrt(ms + RMS_EPS) * g


def _gelu(x):
    return 0.5 * x * (1.0 + lax.erf(x * np.float32(math.sqrt(0.5))))


def _params(*sem):
    return pltpu.CompilerParams(dimension_semantics=sem, vmem_limit_bytes=VMEM_LIMIT)


def _head_norm_rope(z, gain, cos, sin, gmat, first_half):
    width = z.shape[1]
    sq = z * z
    hi = sq.astype(BF16)
    lo = (sq - hi.astype(F32)).astype(BF16)
    ssum = _dot(jnp.concatenate([hi, lo], axis=1), gmat)
    zn = z * lax.rsqrt(ssum * (1.0 / HEAD_DIM) + RMS_EPS) * gain
    rot = jnp.where(first_half, pltpu.roll(zn, width - HEAD_DIM // 2, 1), pltpu.roll(zn, HEAD_DIM // 2, 1))
    return zn * cos + rot * sin


def _qkv_kernel(x_ref, g1_ref, w_ref, cos_ref, sin_ref, qg_ref, kg_ref,
                xn_ref, q_ref, k_ref, v_ref):
    tm = x_ref.shape[0]
    cw = KV_W
    xn = _rms_rows(x_ref[...], g1_ref[...]).astype(BF16)
    xn_ref[...] = xn
    lane = lax.broadcasted_iota(jnp.int32, (tm, cw), 1)
    first_half = (lane % HEAD_DIM) < (HEAD_DIM // 2)
    r = (lax.broadcasted_iota(jnp.int32, (2 * cw, cw), 0) % cw) // HEAD_DIM
    c = lax.broadcasted_iota(jnp.int32, (2 * cw, cw), 1) // HEAD_DIM
    gmat = jnp.where(r == c, 1.0, 0.0).astype(BF16)
    cos = cos_ref[...]
    sin = sin_ref[...]
    scale = HEAD_DIM ** -0.5
    n_q = ATTN_W // cw

    def project(c):
        return _dot(xn, w_ref[:, c * cw:(c + 1) * cw].astype(BF16))

    z = project(0)
    for c in range(n_q + 1):
        z_next = project(c + 1)
        if c < n_q:
            q = _head_norm_rope(z, qg_ref[...], cos, sin, gmat, first_half)
            q_ref[:, c * cw:(c + 1) * cw] = (q * scale).astype(BF16)
        else:
            k_ref[...] = _head_norm_rope(z, kg_ref[...], cos, sin, gmat, first_half)
        z = z_next
    v_ref[...] = z


def _qkv_call(l, x, g1, w_in, cos_t, sin_t, qg, kg, tm, n_pos_blocks):
    T = x.shape[0]
    row = lambda i: (i, 0)
    layer = lambda i: (l, 0, 0)
    pos = lambda i: (i % n_pos_blocks, 0)
    return pl.pallas_call(
        _qkv_kernel,
        grid=(T // tm,),
        in_specs=[
            pl.BlockSpec((tm, D_MODEL), row),
            pl.BlockSpec((None, 1, D_MODEL), layer),
            pl.BlockSpec((None, D_MODEL, QKV_W), layer),
            pl.BlockSpec((tm, KV_W), pos),
            pl.BlockSpec((tm, KV_W), pos),
            pl.BlockSpec((None, 1, KV_W), layer),
            pl.BlockSpec((None, 1, KV_W), layer),
        ],
        out_specs=[
            pl.BlockSpec((tm, D_MODEL), row),
            pl.BlockSpec((tm, ATTN_W), row),
            pl.BlockSpec((tm, KV_W), row),
            pl.BlockSpec((tm, KV_W), row),
        ],
        out_shape=[
            jax.ShapeDtypeStruct((T, D_MODEL), BF16),
            jax.ShapeDtypeStruct((T, ATTN_W), BF16),
            jax.ShapeDtypeStruct((T, KV_W), F32),
            jax.ShapeDtypeStruct((T, KV_W), F32),
        ],
        compiler_params=_params("parallel"),
        name="qkv",
    )(x, g1, w_in, cos_t, sin_t, qg, kg)


UV_BLOCK = 512
UV_BLOCKS = 2 * SGU_W // UV_BLOCK


def _uv_project(xn_ref, w_refs, lng_ref, lnb_ref, u_scr, vs_scr):
    xn = xn_ref[...]
    cw = 2 * LANES
    per_block = UV_BLOCK // cw
    for c in range(2 * SGU_W // cw):
        w = w_refs[c // per_block][:, (c % per_block) * cw:(c % per_block + 1) * cw].astype(BF16)
        dst = u_scr if c < SGU_W // cw else vs_scr
        c_dst = c % (SGU_W // cw)
        dst[:, c_dst * cw:(c_dst + 1) * cw] = _gelu(_dot(xn, w))
    vs = vs_scr[...]
    mu = jnp.mean(vs, axis=-1, keepdims=True)
    d = vs - mu
    var = jnp.mean(d * d, axis=-1, keepdims=True)
    vs_scr[...] = d * lax.rsqrt(var + LN_EPS) * lng_ref[...] + lnb_ref[...]


def _sgu_prompt_kernel(xn_ref, *refs):
    w_refs = refs[:UV_BLOCKS]
    lng_ref, lnb_ref, ws_ref, bs_ref, m_ref, u_scr, vs_scr = refs[UV_BLOCKS:]
    tm = xn_ref.shape[0]
    _uv_project(xn_ref, w_refs, lng_ref, lnb_ref, u_scr, vs_scr)
    t = lax.broadcasted_iota(jnp.int32, (SGU_CHUNK, SGU_CHUNK), 0)
    s = lax.broadcasted_iota(jnp.int32, (SGU_CHUNK, SGU_CHUNK), 1)
    causal = t >= s
    for g in range(SGU_GROUPS):
        wg = jnp.where(causal, ws_ref[g], 0.0).astype(BF16)
        bias = bs_ref[:, g:g + 1]
        cols = slice(g * SGU_CH, (g + 1) * SGU_CH)
        for n in range(tm // SGU_CHUNK):
            rows = slice(n * SGU_CHUNK, (n + 1) * SGU_CHUNK)
            mixed = _dot(wg, vs_scr[rows, cols].astype(BF16)) + bias
            m_ref[rows, cols] = (u_scr[rows, cols] * mixed).astype(BF16)


def _uv_weight_specs(l):
    return [pl.BlockSpec((None, D_MODEL, UV_BLOCK), functools.partial(lambda b, i: (l, 0, b), UV_OFF // UV_BLOCK + k))
            for k in range(UV_BLOCKS)]


def _sgu_prompt_call(l, xn, w_in, lng, lnb, ws, bs_t, tm):
    T = xn.shape[0]
    row = lambda i: (i, 0)
    layer = lambda i: (l, 0, 0)
    return pl.pallas_call(
        _sgu_prompt_kernel,
        grid=(T // tm,),
        in_specs=[
            pl.BlockSpec((tm, D_MODEL), row),
            *_uv_weight_specs(l),
            pl.BlockSpec((None, 1, SGU_W), layer),
            pl.BlockSpec((None, 1, SGU_W), layer),
            pl.BlockSpec((None, SGU_GROUPS, SGU_CHUNK, SGU_CHUNK), lambda i: (l, 0, 0, 0)),
            pl.BlockSpec((None, SGU_CHUNK, SGU_GROUPS), layer),
        ],
        out_specs=pl.BlockSpec((tm, SGU_W), row),
        out_shape=jax.ShapeDtypeStruct((T, SGU_W), BF16),
        scratch_shapes=[pltpu.VMEM((tm, SGU_W), F32), pltpu.VMEM((tm, SGU_W), F32)],
        compiler_params=_params("parallel"),
        name="sgu_prompt",
    )(xn, *([w_in] * UV_BLOCKS), lng, lnb, ws, bs_t)


def _sgu_sample_kernel(ws_ref, bs_ref, xn_ref, *refs, l, seq):
    w_refs = refs[:UV_BLOCKS]
    lng_ref, lnb_ref, m_ref, vsn_ref, u_scr, g_scr = refs[UV_BLOCKS:]
    nseq = xn_ref.shape[0] // seq
    _uv_project(xn_ref, w_refs, lng_ref, lnb_ref, u_scr, vsn_ref)
    for g in range(SGU_GROUPS):
        cols = slice(g * SGU_CH, (g + 1) * SGU_CH)
        g_scr[0] = vsn_ref[:, cols]
        g_scr[1] = u_scr[:, cols]
        for t in range(seq):
            step_t = pl.ds(t, nseq, stride=seq)
            acc = jnp.full((nseq, SGU_CH), bs_ref[l, g * seq + t], F32)
            for s in range(t + 1):
                acc = acc + ws_ref[l, (g * seq + t) * seq + s] * g_scr[0, pl.ds(s, nseq, stride=seq), :]
            g_scr[2, step_t, :] = g_scr[1, step_t, :] * acc
        m_ref[:, cols] = g_scr[2].astype(BF16)


def _sgu_sample_call(l, xn, w_in, lng, lnb, ws_flat, bs_flat, seq):
    T = xn.shape[0]
    const = lambda i: (0, 0)
    layer = lambda i: (l, 0, 0)
    smem = pl.BlockSpec(memory_space=pltpu.SMEM)
    return pl.pallas_call(
        functools.partial(_sgu_sample_kernel, l=l, seq=seq),
        grid=(1,),
        in_specs=[
            smem,
            smem,
            pl.BlockSpec((T, D_MODEL), const),
            *_uv_weight_specs(l),
            pl.BlockSpec((None, 1, SGU_W), layer),
            pl.BlockSpec((None, 1, SGU_W), layer),
        ],
        out_specs=[pl.BlockSpec((T, SGU_W), const), pl.BlockSpec((T, SGU_W), const)],
        out_shape=[jax.ShapeDtypeStruct((T, SGU_W), BF16), jax.ShapeDtypeStruct((T, SGU_W), F32)],
        scratch_shapes=[pltpu.VMEM((T, SGU_W), F32), pltpu.VMEM((3, T, SGU_CH), F32)],
        compiler_params=_params("arbitrary"),
        name="sgu_sample",
    )(ws_flat, bs_flat, xn, *([w_in] * UV_BLOCKS), lng, lnb)


def _attn_prompt_kernel(sink_ref, q_ref, kp_ref, kc_ref, vp_ref, vc_ref, o_ref, *, l):
    blk = WINDOW
    nq = Q_GROUP * blk
    first_tile = pl.program_id(1) == 0
    c = lax.broadcasted_iota(jnp.int32, (blk, nq), 0)
    r = lax.broadcasted_iota(jnp.int32, (blk, nq), 1) % blk
    from_prev = c > r
    k_all = jnp.concatenate([kp_ref[...], kc_ref[...]], axis=0).astype(BF16)
    vt_all = jnp.concatenate([vp_ref[...], vc_ref[...]], axis=0).T.astype(BF16)
    sinks = [jnp.concatenate([jnp.full((1, blk), sink_ref[l, g * Q_GROUP + j], F32) for j in range(Q_GROUP)],
                             axis=1) for g in range(N_KV_HEADS)]

    def scores(jb):
        out = []
        for g in range(N_KV_HEADS):
            qg = jnp.concatenate([q_ref[jb * blk:(jb + 1) * blk, h * HEAD_DIM:(h + 1) * HEAD_DIM]
                                  for h in range(g * Q_GROUP, (g + 1) * Q_GROUP)], axis=0)
            kg = k_all[jb * blk:(jb + 2) * blk, g * HEAD_DIM:(g + 1) * HEAD_DIM]
            out.append(lax.dot_general(kg, qg, (((1,), (1,)), ((), ())), preferred_element_type=F32))
        return out

    nblk = q_ref.shape[0] // blk
    s2 = scores(0)
    for jb in range(nblk):
        s2_next = scores(jb + 1) if jb + 1 < nblk else None
        for g in range(N_KV_HEADS):
            s_prev = s2[g][:blk]
            if jb == 0:
                s_prev = jnp.where(first_tile, NEG_INF, s_prev)
            s = jnp.where(from_prev, s_prev, s2[g][blk:])
            m = jnp.maximum(jnp.max(s, axis=0, keepdims=True), sinks[g])
            p = jnp.exp(s - m)
            p = p / (jnp.sum(p, axis=0, keepdims=True) + jnp.exp(sinks[g] - m))
            p2 = jnp.concatenate([jnp.where(from_prev, p, 0.0), jnp.where(from_prev, 0.0, p)], axis=0)
            vt = vt_all[g * HEAD_DIM:(g + 1) * HEAD_DIM, jb * blk:(jb + 2) * blk]
            o = _dot(vt, p2.astype(BF16)).T
            for j in range(Q_GROUP):
                h = g * Q_GROUP + j
                o_ref[jb * blk:(jb + 1) * blk, h * HEAD_DIM:(h + 1) * HEAD_DIM] = (
                    o[j * blk:(j + 1) * blk].astype(BF16))
        s2 = s2_next


def _attn_prompt_call(l, sinks, q, k, v, batch, seq, tq):
    nt = seq // tq
    blocks_per_tile = tq // WINDOW
    cur = lambda b, i: (b * nt + i, 0)
    prev = lambda b, i: (b * (seq // WINDOW) + jnp.maximum(i * blocks_per_tile - 1, 0), 0)
    return pl.pallas_call(
        functools.partial(_attn_prompt_kernel, l=l),
        grid=(batch, nt),
        in_specs=[
            pl.BlockSpec(memory_space=pltpu.SMEM),
            pl.BlockSpec((tq, ATTN_W), cur),
            pl.BlockSpec((WINDOW, KV_W), prev),
            pl.BlockSpec((tq, KV_W), cur),
            pl.BlockSpec((WINDOW, KV_W), prev),
            pl.BlockSpec((tq, KV_W), cur),
        ],
        out_specs=pl.BlockSpec((tq, ATTN_W), cur),
        out_shape=jax.ShapeDtypeStruct((batch * seq, ATTN_W), BF16),
        compiler_params=_params("parallel", "parallel"),
        name="attn_prompt",
    )(sinks, q, k, k, v, v)


def _attn_sample_kernel(sink_ref, q_ref, kn_ref, vn_ref, ck_ref, cv_ref, o_ref, nk_ref, nv_ref, *, l, seq):
    nseq = q_ref.shape[0] // seq
    nq = nseq * seq
    rows = Q_GROUP * nq
    ri = lax.broadcasted_iota(jnp.int32, (rows, nseq * WINDOW), 0) % nq
    ci = lax.broadcasted_iota(jnp.int32, (rows, nseq * WINDOW), 1)
    mask_c = (ri // seq == ci // WINDOW) & (ci % WINDOW > ri % seq)
    ri = lax.broadcasted_iota(jnp.int32, (rows, nq), 0) % nq
    ci = lax.broadcasted_iota(jnp.int32, (rows, nq), 1)
    mask_n = (ri // seq == ci // seq) & (ci % seq <= ri % seq)
    dn = (((1,), (1,)), ((), ()))
    for g in range(N_KV_HEADS):
        cols = slice(g * HEAD_DIM, (g + 1) * HEAD_DIM)
        heads = range(g * Q_GROUP, (g + 1) * Q_GROUP)
        qg = jnp.concatenate([q_ref[:, h * HEAD_DIM:(h + 1) * HEAD_DIM] for h in heads], axis=0)
        sink = jnp.concatenate([jnp.full((nq, 1), sink_ref[l, h], F32) for h in heads], axis=0)
        s_c = lax.dot_general(qg, ck_ref[:, cols].astype(BF16), dn, preferred_element_type=F32)
        s_n = lax.dot_general(qg, kn_ref[:, cols].astype(BF16), dn, preferred_element_type=F32)
        s_c = jnp.where(mask_c, s_c, NEG_INF)
        s_n = jnp.where(mask_n, s_n, NEG_INF)
        m = jnp.maximum(jnp.maximum(jnp.max(s_c, axis=-1, keepdims=True),
                                    jnp.max(s_n, axis=-1, keepdims=True)), sink)
        p_c = jnp.exp(s_c - m)
        p_n = jnp.exp(s_n - m)
        denom = (jnp.sum(p_c, axis=-1, keepdims=True) + jnp.sum(p_n, axis=-1, keepdims=True)
                 + jnp.exp(sink - m))
        o = (_dot((p_c / denom).astype(BF16), cv_ref[:, cols].astype(BF16))
             + _dot((p_n / denom).astype(BF16), vn_ref[:, cols].astype(BF16)))
        for j, h in enumerate(heads):
            o_ref[:, h * HEAD_DIM:(h + 1) * HEAD_DIM] = o[j * nq:(j + 1) * nq].astype(BF16)
    for b in range(nseq):
        nk_ref[b * WINDOW:(b + 1) * WINDOW - seq, :] = ck_ref[b * WINDOW + seq:(b + 1) * WINDOW, :]
        nv_ref[b * WINDOW:(b + 1) * WINDOW - seq, :] = cv_ref[b * WINDOW + seq:(b + 1) * WINDOW, :]
        nk_ref[(b + 1) * WINDOW - seq:(b + 1) * WINDOW, :] = kn_ref[b * seq:(b + 1) * seq, :]
        nv_ref[(b + 1) * WINDOW - seq:(b + 1) * WINDOW, :] = vn_ref[b * seq:(b + 1) * seq, :]


def _attn_sample_call(l, sinks, q, k, v, ck, cv, seq, seq_block):
    T = q.shape[0]
    nseq = T // seq
    row = lambda i: (i, 0)
    cache = lambda i: (l, i, 0)
    return pl.pallas_call(
        functools.partial(_attn_sample_kernel, l=l, seq=seq),
        grid=(nseq // seq_block,),
        in_specs=[
            pl.BlockSpec(memory_space=pltpu.SMEM),
            pl.BlockSpec((seq_block * seq, ATTN_W), row),
            pl.BlockSpec((seq_block * seq, KV_W), row),
            pl.BlockSpec((seq_block * seq, KV_W), row),
            pl.BlockSpec((None, seq_block * WINDOW, KV_W), cache),
            pl.BlockSpec((None, seq_block * WINDOW, KV_W), cache),
        ],
        out_specs=[
            pl.BlockSpec((seq_block * seq, ATTN_W), row),
            pl.BlockSpec((seq_block * WINDOW, KV_W), row),
            pl.BlockSpec((seq_block * WINDOW, KV_W), row),
        ],
        out_shape=[
            jax.ShapeDtypeStruct((T, ATTN_W), BF16),
            jax.ShapeDtypeStruct((nseq * WINDOW, KV_W), F32),
            jax.ShapeDtypeStruct((nseq * WINDOW, KV_W), F32),
        ],
        compiler_params=_params("parallel"),
        name="attn_sample",
    )(sinks, q, k, v, ck, cv)


def _merge_kernel(x_ref, xn_ref, a_ref, m_ref, wga_ref, wgm_ref, wau_ref, wsu_ref, wout_ref,
                  h_ref, merged_scr):
    j = pl.program_id(1)
    xn = xn_ref[...]
    ga = jax.nn.sigmoid(_dot(xn, wga_ref[...]))
    gm = jax.nn.sigmoid(_dot(xn, wgm_ref[...]))
    merged = ga * _dot(a_ref[...], wau_ref[...]) + gm * _dot(m_ref[...], wsu_ref[...])
    merged_scr[j] = merged.astype(BF16)

    @pl.when(j == pl.num_programs(1) - 1)
    def _():
        tn = merged_scr.shape[2]
        acc = x_ref[...]
        for c in range(merged_scr.shape[0]):
            acc = acc + _dot(merged_scr[c], wout_ref[c * tn:(c + 1) * tn, :])
        h_ref[...] = acc


def _merge_call(l, x, xn, a, m, w_gate, w_au, w_su, w_out, tm, tn):
    T = x.shape[0]
    nj = D_MODEL // tn
    row = lambda i, j: (i, 0)
    return pl.pallas_call(
        _merge_kernel,
        grid=(T // tm, nj),
        in_specs=[
            pl.BlockSpec((tm, D_MODEL), row),
            pl.BlockSpec((tm, D_MODEL), row),
            pl.BlockSpec((tm, ATTN_W), row),
            pl.BlockSpec((tm, SGU_W), row),
            pl.BlockSpec((None, D_MODEL, tn), lambda i, j: (l, 0, j)),
            pl.BlockSpec((None, D_MODEL, tn), lambda i, j: (l, 0, nj + j)),
            pl.BlockSpec((None, ATTN_W, tn), lambda i, j: (l, 0, j)),
            pl.BlockSpec((None, SGU_W, tn), lambda i, j: (l, 0, j)),
            pl.BlockSpec((None, D_MODEL, D_MODEL), lambda i, j: (l, 0, 0)),
        ],
        out_specs=pl.BlockSpec((tm, D_MODEL), row),
        out_shape=jax.ShapeDtypeStruct((T, D_MODEL), F32),
        scratch_shapes=[pltpu.VMEM((nj, tm, tn), BF16)],
        compiler_params=_params("parallel", "arbitrary"),
        name="merge",
    )(x, xn, a, m, w_gate, w_gate, w_au, w_su, w_out)


def _ffn_kernel(h_ref, g2_ref, w1_ref, w2_ref, y_ref, hn_scr):
    @pl.when(pl.program_id(1) == 0)
    def _():
        h = h_ref[...]
        hn_scr[...] = _rms_rows(h, g2_ref[...]).astype(BF16)
        y_ref[...] = h

    half = h_ref.shape[0] // 2
    halves = (slice(0, half), slice(half, 2 * half))
    w1 = w1_ref[...].astype(BF16)
    w2 = w2_ref[...].astype(BF16)
    acts = []
    for rows in halves:
        a = jnp.maximum(_dot(hn_scr[rows, :], w1), 0.0)
        acts.append((a * a).astype(BF16))
    for rows, a in zip(halves, acts):
        y_ref[rows, :] += _dot(a, w2)


def _ffn_call(l, h, g2, w1, w2, tm, tf):
    T = h.shape[0]
    row = lambda i, f: (i, 0)
    return pl.pallas_call(
        _ffn_kernel,
        grid=(T // tm, D_FF // tf),
        in_specs=[
            pl.BlockSpec((tm, D_MODEL), row),
            pl.BlockSpec((None, 1, D_MODEL), lambda i, f: (l, 0, 0)),
            pl.BlockSpec((None, D_MODEL, tf), lambda i, f: (l, 0, f)),
            pl.BlockSpec((None, tf, D_MODEL), lambda i, f: (l, f, 0)),
        ],
        out_specs=pl.BlockSpec((tm, D_MODEL), row),
        out_shape=jax.ShapeDtypeStruct((T, D_MODEL), F32),
        scratch_shapes=[pltpu.VMEM((tm, D_MODEL), BF16)],
        compiler_params=_params("parallel", "arbitrary"),
        name="ffn",
    )(h, g2, w1, w2)


def _rope_tables(pos):
    half = HEAD_DIM // 2
    inv = jnp.power(jnp.float32(ROPE_THETA), -jnp.arange(half, dtype=jnp.float32) / half)
    ang = pos.astype(jnp.float32)[:, None] * inv[None, :]
    cos = jnp.cos(ang)
    sin = jnp.sin(ang)
    return jnp.tile(cos, (1, 2 * N_KV_HEADS)), jnp.tile(jnp.concatenate([-sin, sin], axis=-1), (1, N_KV_HEADS))


def kernel(x_prompt, x_sample, cache_k, cache_v, norm1_g, w_in, q_norm_g, k_norm_g, attn_sinks,
           sgu_ln_g, sgu_ln_b, sgu_w, sgu_b, w_attn_up, w_sgu_up, w_out, norm2_g, w_ff1, w_ff2):
    batch, seq, _ = x_prompt.shape
    dec_batch, dec_seq, _ = x_sample.shape
    tp = batch * seq
    ts = dec_batch * dec_seq
    tm = 512
    tn = 512
    assert seq % tm == 0 and tm % SGU_CHUNK == 0 and seq % WINDOW == 0

    cos_p, sin_p = _rope_tables(jnp.arange(seq))
    cos_s, sin_s = _rope_tables(jnp.tile(PAST_LEN + jnp.arange(dec_seq), dec_batch))

    w_gate = w_in[:, :, GATE_OFF:].astype(BF16)
    w_au = w_attn_up.astype(BF16)
    w_su = w_sgu_up.astype(BF16)
    w_o = w_out.astype(BF16)

    g1 = norm1_g.reshape(DEPTH, 1, D_MODEL)
    g2 = norm2_g.reshape(DEPTH, 1, D_MODEL)
    qg = jnp.tile(q_norm_g, (1, N_KV_HEADS)).reshape(DEPTH, 1, KV_W)
    kg = jnp.tile(k_norm_g, (1, N_KV_HEADS)).reshape(DEPTH, 1, KV_W)
    lng = sgu_ln_g.reshape(DEPTH, 1, SGU_W)
    lnb = sgu_ln_b.reshape(DEPTH, 1, SGU_W)
    bs_t = jnp.swapaxes(sgu_b, 1, 2)
    ws_s = sgu_w[:, :, :dec_seq, :dec_seq].reshape(DEPTH, -1)
    bs_s = sgu_b[:, :, :dec_seq].reshape(DEPTH, -1)
    ck = cache_k.reshape(DEPTH, dec_batch * WINDOW, KV_W)
    cv = cache_v.reshape(DEPTH, dec_batch * WINDOW, KV_W)

    xp = x_prompt.reshape(tp, D_MODEL)
    xs = x_sample.reshape(ts, D_MODEL)
    kp_l, vp_l, ks_l, vs_l, sv_l = [], [], [], [], []
    for l in range(DEPTH):
        xn, q, k, v = _qkv_call(l, xp, g1, w_in, cos_p, sin_p, qg, kg, tm, seq // tm)
        m = _sgu_prompt_call(l, xn, w_in, lng, lnb, sgu_w, bs_t, tm)
        a = _attn_prompt_call(l, attn_sinks, q, k, v, batch, seq, tm)
        h = _merge_call(l, xp, xn, a, m, w_gate, w_au, w_su, w_o, tm, tn)
        xp = _ffn_call(l, h, g2, w_ff1, w_ff2, 2 * tm, tn)
        kp_l.append(k.reshape(batch, seq, N_KV_HEADS, HEAD_DIM)[:, -WINDOW:])
        vp_l.append(v.reshape(batch, seq, N_KV_HEADS, HEAD_DIM)[:, -WINDOW:])

        xn, q, k, v = _qkv_call(l, xs, g1, w_in, cos_s, sin_s, qg, kg, ts, 1)
        m, vsn = _sgu_sample_call(l, xn, w_in, lng, lnb, ws_s, bs_s, dec_seq)
        a, nk, nv = _attn_sample_call(l, attn_sinks, q, k, v, ck, cv, dec_seq, 8)
        h = _merge_call(l, xs, xn, a, m, w_gate, w_au, w_su, w_o, ts, 2 * tn)
        xs = _ffn_call(l, h, g2, w_ff1, w_ff2, ts, 2 * tn)
        ks_l.append(nk.reshape(dec_batch, WINDOW, N_KV_HEADS, HEAD_DIM))
        vs_l.append(nv.reshape(dec_batch, WINDOW, N_KV_HEADS, HEAD_DIM))
        sv_l.append(vsn.reshape(dec_batch, dec_seq, SGU_GROUPS, SGU_CH))

    return (xp.reshape(batch, seq, D_MODEL), xs.reshape(dec_batch, dec_seq, D_MODEL),
            jnp.stack(kp_l), jnp.stack(vp_l), jnp.stack(ks_l), jnp.stack(vs_l), jnp.stack(sv_l))
```

```python
import functools
import math

import jax
import jax.numpy as jnp
import numpy as np
from jax import lax
from jax.experimental import pallas as pl
from jax.experimental.pallas import tpu as pltpu

D_MODEL = 2048
DEPTH = 4
PAST_LEN = 16384
HEAD_DIM = 64
N_HEADS = 16
N_KV_HEADS = 4
Q_GROUP = 4
ATTN_W = N_HEADS * HEAD_DIM
KV_W = N_KV_HEADS * HEAD_DIM
WINDOW = 128
ROPE_THETA = 10000.0
SGU_CHUNK = 128
SGU_W = D_MODEL // 2
SGU_GROUPS = 8
SGU_CH = SGU_W // SGU_GROUPS
D_FF = 4 * D_MODEL
RMS_EPS = 1e-6
LN_EPS = 1e-5
NEG_INF = -1e30

QKV_W = ATTN_W + 2 * KV_W
UV_OFF = QKV_W
GATE_OFF = QKV_W + 2 * SGU_W

LANES = 128
VMEM_LIMIT = 56 * 1024 * 1024

BF16 = jnp.bfloat16
F32 = jnp.float32


def _dot(a, b):
    return jnp.dot(a, b, preferred_element_type=F32)


def _rms_rows(x, g):
    ms = jnp.mean(x * x, axis=-1, keepdims=True)
    return x * lax.rsqrt(ms + RMS_EPS) * g


def _gelu(x):
    return 0.5 * x * (1.0 + lax.erf(x * np.float32(math.sqrt(0.5))))


def _params(*sem):
    return pltpu.CompilerParams(dimension_semantics=sem, vmem_limit_bytes=VMEM_LIMIT)


def _head_norm_rope(z, gain, cos, sin, gmat, first_half):
    width = z.shape[1]
    sq = z * z
    hi = sq.astype(BF16)
    lo = (sq - hi.astype(F32)).astype(BF16)
    ssum = _dot(jnp.concatenate([hi, lo], axis=1), gmat)
    zn = z * lax.rsqrt(ssum * (1.0 / HEAD_DIM) + RMS_EPS) * gain
    rot = jnp.where(first_half, pltpu.roll(zn, width - HEAD_DIM // 2, 1), pltpu.roll(zn, HEAD_DIM // 2, 1))
    return zn * cos + rot * sin


def _qkv_kernel(x_ref, g1_ref, w_ref, cos_ref, sin_ref, qg_ref, kg_ref,
                xn_ref, q_ref, k_ref, v_ref):
    tm = x_ref.shape[0]
    cw = KV_W
    xn = _rms_rows(x_ref[...], g1_ref[...]).astype(BF16)
    xn_ref[...] = xn
    lane = lax.broadcasted_iota(jnp.int32, (tm, cw), 1)
    first_half = (lane % HEAD_DIM) < (HEAD_DIM // 2)
    r = (lax.broadcasted_iota(jnp.int32, (2 * cw, cw), 0) % cw) // HEAD_DIM
    c = lax.broadcasted_iota(jnp.int32, (2 * cw, cw), 1) // HEAD_DIM
    gmat = jnp.where(r == c, 1.0, 0.0).astype(BF16)
    cos = cos_ref[...]
    sin = sin_ref[...]
    scale = HEAD_DIM ** -0.5
    n_q = ATTN_W // cw

    def project(c):
        return _dot(xn, w_ref[:, c * cw:(c + 1) * cw].astype(BF16))

    z = project(0)
    for c in range(n_q + 1):
        z_next = project(c + 1)
        if c < n_q:
            q = _head_norm_rope(z, qg_ref[...], cos, sin, gmat, first_half)
            q_ref[:, c * cw:(c + 1) * cw] = (q * scale).astype(BF16)
        else:
            k_ref[...] = _head_norm_rope(z, kg_ref[...], cos, sin, gmat, first_half)
        z = z_next
    v_ref[...] = z


def _qkv_call(l, x, g1, w_in, cos_t, sin_t, qg, kg, tm, n_pos_blocks):
    T = x.shape[0]
    row = lambda i: (i, 0)
    layer = lambda i: (l, 0, 0)
    pos = lambda i: (i % n_pos_blocks, 0)
    return pl.pallas_call(
        _qkv_kernel,
        grid=(T // tm,),
        in_specs=[
            pl.BlockSpec((tm, D_MODEL), row),
            pl.BlockSpec((None, 1, D_MODEL), layer),
            pl.BlockSpec((None, D_MODEL, QKV_W), layer),
            pl.BlockSpec((tm, KV_W), pos),
            pl.BlockSpec((tm, KV_W), pos),
            pl.BlockSpec((None, 1, KV_W), layer),
            pl.BlockSpec((None, 1, KV_W), layer),
        ],
        out_specs=[
            pl.BlockSpec((tm, D_MODEL), row),
            pl.BlockSpec((tm, ATTN_W), row),
            pl.BlockSpec((tm, KV_W), row),
            pl.BlockSpec((tm, KV_W), row),
        ],
        out_shape=[
            jax.ShapeDtypeStruct((T, D_MODEL), BF16),
            jax.ShapeDtypeStruct((T, ATTN_W), BF16),
            jax.ShapeDtypeStruct((T, KV_W), F32),
            jax.ShapeDtypeStruct((T, KV_W), F32),
        ],
        compiler_params=_params("parallel"),
        name="qkv",
    )(x, g1, w_in, cos_t, sin_t, qg, kg)


UV_BLOCK = 512
UV_BLOCKS = 2 * SGU_W // UV_BLOCK


def _uv_project(xn_ref, w_refs, lng_ref, lnb_ref, u_scr, vs_scr):
    xn = xn_ref[...]
    cw = 2 * LANES
    per_block = UV_BLOCK // cw
    for c in range(2 * SGU_W // cw):
        w = w_refs[c // per_block][:, (c % per_block) * cw:(c % per_block + 1) * cw].astype(BF16)
        dst = u_scr if c < SGU_W // cw else vs_scr
        c_dst = c % (SGU_W // cw)
        dst[:, c_dst * cw:(c_dst + 1) * cw] = _gelu(_dot(xn, w))
    vs = vs_scr[...]
    mu = jnp.mean(vs, axis=-1, keepdims=True)
    d = vs - mu
    var = jnp.mean(d * d, axis=-1, keepdims=True)
    vs_scr[...] = d * lax.rsqrt(var + LN_EPS) * lng_ref[...] + lnb_ref[...]


def _sgu_kernel(wss_ref, bss_ref, xn_ref, *refs, l, seq, n_tail):
    w_refs = refs[:UV_BLOCKS]
    lng_ref, lnb_ref, ws_ref, bs_ref, m_ref, vsn_ref, u_scr, vs_scr, g_scr = refs[UV_BLOCKS:]
    tm = xn_ref.shape[0]
    n_chunks = tm // SGU_CHUNK
    n_shared = (tm - n_tail) // SGU_CHUNK
    last_tile = pl.program_id(0) == pl.num_programs(0) - 1
    _uv_project(xn_ref, w_refs, lng_ref, lnb_ref, u_scr, vs_scr)

    def mix_prompt_chunks(chunks):
        t = lax.broadcasted_iota(jnp.int32, (SGU_CHUNK, SGU_CHUNK), 0)
        s = lax.broadcasted_iota(jnp.int32, (SGU_CHUNK, SGU_CHUNK), 1)
        for g in range(SGU_GROUPS):
            wg = jnp.where(t >= s, ws_ref[g], 0.0).astype(BF16)
            bias = bs_ref[:, g:g + 1]
            cols = slice(g * SGU_CH, (g + 1) * SGU_CH)
            for n in chunks:
                rows = slice(n * SGU_CHUNK, (n + 1) * SGU_CHUNK)
                mixed = _dot(wg, vs_scr[rows, cols].astype(BF16)) + bias
                m_ref[rows, cols] = (u_scr[rows, cols] * mixed).astype(BF16)

    mix_prompt_chunks(range(n_shared))

    @pl.when(jnp.logical_not(last_tile))
    def _():
        mix_prompt_chunks(range(n_shared, n_chunks))

    @pl.when(last_tile)
    def _():
        tail = slice(tm - n_tail, tm)
        nseq = n_tail // seq
        vsn_ref[...] = vs_scr[tail, :]
        for g in range(SGU_GROUPS):
            cols = slice(g * SGU_CH, (g + 1) * SGU_CH)
            g_scr[0] = vs_scr[tail, cols]
            g_scr[1] = u_scr[tail, cols]
            for t in range(seq):
                step_t = pl.ds(t, nseq, stride=seq)
                acc = jnp.full((nseq, SGU_CH), bss_ref[l, g * seq + t], F32)
                for s in range(t + 1):
                    acc = acc + wss_ref[l, (g * seq + t) * seq + s] * g_scr[0, pl.ds(s, nseq, stride=seq), :]
                g_scr[2, step_t, :] = g_scr[1, step_t, :] * acc
            m_ref[tail, cols] = g_scr[2].astype(BF16)


def _uv_weight_specs(l):
    return [pl.BlockSpec((None, D_MODEL, UV_BLOCK), functools.partial(lambda b, i: (l, 0, b), UV_OFF // UV_BLOCK + k))
            for k in range(UV_BLOCKS)]


def _sgu_call(l, xn, w_in, lng, lnb, ws, bs_t, ws_sample, bs_sample, tm, seq, n_tail):
    T = xn.shape[0]
    row = lambda i: (i, 0)
    layer = lambda i: (l, 0, 0)
    smem = pl.BlockSpec(memory_space=pltpu.SMEM)
    return pl.pallas_call(
        functools.partial(_sgu_kernel, l=l, seq=seq, n_tail=n_tail),
        grid=(T // tm,),
        in_specs=[
            smem,
            smem,
            pl.BlockSpec((tm, D_MODEL), row),
            *_uv_weight_specs(l),
            pl.BlockSpec((None, 1, SGU_W), layer),
            pl.BlockSpec((None, 1, SGU_W), layer),
            pl.BlockSpec((None, SGU_GROUPS, SGU_CHUNK, SGU_CHUNK), lambda i: (l, 0, 0, 0)),
            pl.BlockSpec((None, SGU_CHUNK, SGU_GROUPS), layer),
        ],
        out_specs=[pl.BlockSpec((tm, SGU_W), row), pl.BlockSpec((n_tail, SGU_W), lambda i: (0, 0))],
        out_shape=[jax.ShapeDtypeStruct((T, SGU_W), BF16), jax.ShapeDtypeStruct((n_tail, SGU_W), F32)],
        scratch_shapes=[pltpu.VMEM((tm, SGU_W), F32), pltpu.VMEM((tm, SGU_W), F32),
                        pltpu.VMEM((3, n_tail, SGU_CH), F32)],
        compiler_params=_params("arbitrary"),
        name="sgu",
    )(ws_sample, bs_sample, xn, *([w_in] * UV_BLOCKS), lng, lnb, ws, bs_t)


def _attn_prompt_kernel(sink_ref, q_ref, kp_ref, kc_ref, vp_ref, vc_ref, o_ref, *, l):
    blk = WINDOW
    nq = Q_GROUP * blk
    first_tile = pl.program_id(1) == 0
    c = lax.broadcasted_iota(jnp.int32, (blk, nq), 0)
    r = lax.broadcasted_iota(jnp.int32, (blk, nq), 1) % blk
    from_prev = c > r
    k_all = jnp.concatenate([kp_ref[...], kc_ref[...]], axis=0).astype(BF16)
    vt_all = jnp.concatenate([vp_ref[...], vc_ref[...]], axis=0).T.astype(BF16)
    sinks = [jnp.concatenate([jnp.full((1, blk), sink_ref[l, g * Q_GROUP + j], F32) for j in range(Q_GROUP)],
                             axis=1) for g in range(N_KV_HEADS)]

    def scores(jb):
        out = []
        for g in range(N_KV_HEADS):
            qg = jnp.concatenate([q_ref[jb * blk:(jb + 1) * blk, h * HEAD_DIM:(h + 1) * HEAD_DIM]
                                  for h in range(g * Q_GROUP, (g + 1) * Q_GROUP)], axis=0)
            kg = k_all[jb * blk:(jb + 2) * blk, g * HEAD_DIM:(g + 1) * HEAD_DIM]
            out.append(lax.dot_general(kg, qg, (((1,), (1,)), ((), ())), preferred_element_type=F32))
        return out

    nblk = q_ref.shape[0] // blk
    s2 = scores(0)
    for jb in range(nblk):
        s2_next = scores(jb + 1) if jb + 1 < nblk else None
        for g in range(N_KV_HEADS):
            s_prev = s2[g][:blk]
            if jb == 0:
                s_prev = jnp.where(first_tile, NEG_INF, s_prev)
            s = jnp.where(from_prev, s_prev, s2[g][blk:])
            m = jnp.maximum(jnp.max(s, axis=0, keepdims=True), sinks[g])
            p = jnp.exp(s - m)
            p = p / (jnp.sum(p, axis=0, keepdims=True) + jnp.exp(sinks[g] - m))
            p2 = jnp.concatenate([jnp.where(from_prev, p, 0.0), jnp.where(from_prev, 0.0, p)], axis=0)
            vt = vt_all[g * HEAD_DIM:(g + 1) * HEAD_DIM, jb * blk:(jb + 2) * blk]
            o = _dot(vt, p2.astype(BF16)).T
            for j in range(Q_GROUP):
                h = g * Q_GROUP + j
                o_ref[jb * blk:(jb + 1) * blk, h * HEAD_DIM:(h + 1) * HEAD_DIM] = (
                    o[j * blk:(j + 1) * blk].astype(BF16))
        s2 = s2_next


def _attn_prompt_call(l, sinks, q, k, v, batch, seq, tq):
    nt = seq // tq
    blocks_per_tile = tq // WINDOW
    cur = lambda b, i: (b * nt + i, 0)
    prev = lambda b, i: (b * (seq // WINDOW) + jnp.maximum(i * blocks_per_tile - 1, 0), 0)
    return pl.pallas_call(
        functools.partial(_attn_prompt_kernel, l=l),
        grid=(batch, nt),
        in_specs=[
            pl.BlockSpec(memory_space=pltpu.SMEM),
            pl.BlockSpec((tq, ATTN_W), cur),
            pl.BlockSpec((WINDOW, KV_W), prev),
            pl.BlockSpec((tq, KV_W), cur),
            pl.BlockSpec((WINDOW, KV_W), prev),
            pl.BlockSpec((tq, KV_W), cur),
        ],
        out_specs=pl.BlockSpec((tq, ATTN_W), cur),
        out_shape=jax.ShapeDtypeStruct((q.shape[0], ATTN_W), BF16),
        compiler_params=_params("parallel", "parallel"),
        name="attn_prompt",
    )(sinks, q, k, k, v, v)


def _attn_sample_kernel(sink_ref, q_ref, kn_ref, vn_ref, ck_ref, cv_ref, a_prompt_ref, o_ref, nk_ref, nv_ref,
                        *, l, seq):
    del a_prompt_ref
    nseq = q_ref.shape[0] // seq
    nq = nseq * seq
    rows = Q_GROUP * nq
    ri = lax.broadcasted_iota(jnp.int32, (rows, nseq * WINDOW), 0) % nq
    ci = lax.broadcasted_iota(jnp.int32, (rows, nseq * WINDOW), 1)
    mask_c = (ri // seq == ci // WINDOW) & (ci % WINDOW > ri % seq)
    ri = lax.broadcasted_iota(jnp.int32, (rows, nq), 0) % nq
    ci = lax.broadcasted_iota(jnp.int32, (rows, nq), 1)
    mask_n = (ri // seq == ci // seq) & (ci % seq <= ri % seq)
    dn = (((1,), (1,)), ((), ()))
    for g in range(N_KV_HEADS):
        cols = slice(g * HEAD_DIM, (g + 1) * HEAD_DIM)
        heads = range(g * Q_GROUP, (g + 1) * Q_GROUP)
        qg = jnp.concatenate([q_ref[:, h * HEAD_DIM:(h + 1) * HEAD_DIM] for h in heads], axis=0)
        sink = jnp.concatenate([jnp.full((nq, 1), sink_ref[l, h], F32) for h in heads], axis=0)
        s_c = lax.dot_general(qg, ck_ref[:, cols].astype(BF16), dn, preferred_element_type=F32)
        s_n = lax.dot_general(qg, kn_ref[:, cols].astype(BF16), dn, preferred_element_type=F32)
        s_c = jnp.where(mask_c, s_c, NEG_INF)
        s_n = jnp.where(mask_n, s_n, NEG_INF)
        m = jnp.maximum(jnp.maximum(jnp.max(s_c, axis=-1, keepdims=True),
                                    jnp.max(s_n, axis=-1, keepdims=True)), sink)
        p_c = jnp.exp(s_c - m)
        p_n = jnp.exp(s_n - m)
        denom = (jnp.sum(p_c, axis=-1, keepdims=True) + jnp.sum(p_n, axis=-1, keepdims=True)
                 + jnp.exp(sink - m))
        o = (_dot((p_c / denom).astype(BF16), cv_ref[:, cols].astype(BF16))
             + _dot((p_n / denom).astype(BF16), vn_ref[:, cols].astype(BF16)))
        for j, h in enumerate(heads):
            o_ref[:, h * HEAD_DIM:(h + 1) * HEAD_DIM] = o[j * nq:(j + 1) * nq].astype(BF16)
    for b in range(nseq):
        nk_ref[b * WINDOW:(b + 1) * WINDOW - seq, :] = ck_ref[b * WINDOW + seq:(b + 1) * WINDOW, :]
        nv_ref[b * WINDOW:(b + 1) * WINDOW - seq, :] = cv_ref[b * WINDOW + seq:(b + 1) * WINDOW, :]
        nk_ref[(b + 1) * WINDOW - seq:(b + 1) * WINDOW, :] = kn_ref[b * seq:(b + 1) * seq, :]
        nv_ref[(b + 1) * WINDOW - seq:(b + 1) * WINDOW, :] = vn_ref[b * seq:(b + 1) * seq, :]


def _attn_sample_call(l, sinks, q, k, v, ck, cv, a_prompt, first_row, nseq, seq, seq_block):
    rows = seq_block * seq
    assert first_row % rows == 0
    sample_row = lambda i: (first_row // rows + i, 0)
    row = lambda i: (i, 0)
    cache = lambda i: (l, i, 0)
    a_prompt_arg = 6
    return pl.pallas_call(
        functools.partial(_attn_sample_kernel, l=l, seq=seq),
        grid=(nseq // seq_block,),
        in_specs=[
            pl.BlockSpec(memory_space=pltpu.SMEM),
            pl.BlockSpec((rows, ATTN_W), sample_row),
            pl.BlockSpec((rows, KV_W), sample_row),
            pl.BlockSpec((rows, KV_W), sample_row),
            pl.BlockSpec((None, seq_block * WINDOW, KV_W), cache),
            pl.BlockSpec((None, seq_block * WINDOW, KV_W), cache),
            pl.BlockSpec(memory_space=pl.ANY),
        ],
        out_specs=[
            pl.BlockSpec((rows, ATTN_W), sample_row),
            pl.BlockSpec((seq_block * WINDOW, KV_W), row),
            pl.BlockSpec((seq_block * WINDOW, KV_W), row),
        ],
        out_shape=[
            jax.ShapeDtypeStruct(a_prompt.shape, BF16),
            jax.ShapeDtypeStruct((nseq * WINDOW, KV_W), F32),
            jax.ShapeDtypeStruct((nseq * WINDOW, KV_W), F32),
        ],
        input_output_aliases={a_prompt_arg: 0},
        compiler_params=_params("parallel"),
        name="attn_sample",
    )(sinks, q, k, v, ck, cv, a_prompt)


def _merge_kernel(x_ref, xn_ref, a_ref, m_ref, wga_ref, wgm_ref, wau_ref, wsu_ref, wout_ref, h_ref):
    @pl.when(pl.program_id(1) == 0)
    def _():
        h_ref[...] = x_ref[...]

    xn = xn_ref[...]
    a = a_ref[...]
    m = m_ref[...]
    half = wga_ref.shape[1] // 2
    merged = []
    for cols in (slice(0, half), slice(half, 2 * half)):
        ga = jax.nn.sigmoid(_dot(xn, wga_ref[:, cols]))
        gm = jax.nn.sigmoid(_dot(xn, wgm_ref[:, cols]))
        merged.append((ga * _dot(a, wau_ref[:, cols]) + gm * _dot(m, wsu_ref[:, cols])).astype(BF16))
    h_ref[...] += _dot(merged[0], wout_ref[:half, :]) + _dot(merged[1], wout_ref[half:, :])


def _merge_call(l, x, xn, a, m, w_gate, w_au, w_su, w_out, tm, tn):
    T = x.shape[0]
    nj = D_MODEL // tn
    row = lambda i, j: (i, 0)
    return pl.pallas_call(
        _merge_kernel,
        grid=(T // tm, nj),
        in_specs=[
            pl.BlockSpec((tm, D_MODEL), row, pipeline_mode=pl.Buffered(1)),
            pl.BlockSpec((tm, D_MODEL), row),
            pl.BlockSpec((tm, ATTN_W), row),
            pl.BlockSpec((tm, SGU_W), row),
            pl.BlockSpec((None, D_MODEL, tn), lambda i, j: (l, 0, j)),
            pl.BlockSpec((None, D_MODEL, tn), lambda i, j: (l, 0, nj + j)),
            pl.BlockSpec((None, ATTN_W, tn), lambda i, j: (l, 0, j)),
            pl.BlockSpec((None, SGU_W, tn), lambda i, j: (l, 0, j)),
            pl.BlockSpec((None, tn, D_MODEL), lambda i, j: (l, j, 0)),
        ],
        out_specs=pl.BlockSpec((tm, D_MODEL), row),
        out_shape=jax.ShapeDtypeStruct((T, D_MODEL), F32),
        compiler_params=_params("parallel", "arbitrary"),
        name="merge",
    )(x, xn, a, m, w_gate, w_gate, w_au, w_su, w_out)


def _ffn_kernel(h_ref, g2_ref, w1_ref, w2_ref, y_ref, hn_scr, *, n_sub):
    @pl.when(pl.program_id(1) == 0)
    def _():
        h = h_ref[...]
        hn_scr[...] = _rms_rows(h, g2_ref[...]).astype(BF16)
        y_ref[...] = h

    sub = h_ref.shape[0] // n_sub
    subtiles = [slice(r * sub, (r + 1) * sub) for r in range(n_sub)]
    w1 = w1_ref[...].astype(BF16)
    w2 = w2_ref[...].astype(BF16)
    acts = []
    for rows in subtiles:
        a = jnp.maximum(_dot(hn_scr[rows, :], w1), 0.0)
        acts.append((a * a).astype(BF16))
    for rows, a in zip(subtiles, acts):
        y_ref[rows, :] += _dot(a, w2)


def _ffn_call(l, h, g2, w1, w2, tm, tf, n_sub):
    T = h.shape[0]
    row = lambda i, f: (i, 0)
    return pl.pallas_call(
        functools.partial(_ffn_kernel, n_sub=n_sub),
        grid=(T // tm, D_FF // tf),
        in_specs=[
            pl.BlockSpec((tm, D_MODEL), row),
            pl.BlockSpec((None, 1, D_MODEL), lambda i, f: (l, 0, 0)),
            pl.BlockSpec((None, D_MODEL, tf), lambda i, f: (l, 0, f)),
            pl.BlockSpec((None, tf, D_MODEL), lambda i, f: (l, f, 0)),
        ],
        out_specs=pl.BlockSpec((tm, D_MODEL), row),
        out_shape=jax.ShapeDtypeStruct((T, D_MODEL), F32),
        scratch_shapes=[pltpu.VMEM((tm, D_MODEL), BF16)],
        compiler_params=_params("parallel", "arbitrary"),
        name="ffn",
    )(h, g2, w1, w2)


def _rope_tables(pos):
    half = HEAD_DIM // 2
    inv = jnp.power(jnp.float32(ROPE_THETA), -jnp.arange(half, dtype=jnp.float32) / half)
    ang = pos.astype(jnp.float32)[:, None] * inv[None, :]
    cos = jnp.cos(ang)
    sin = jnp.sin(ang)
    return jnp.tile(cos, (1, 2 * N_KV_HEADS)), jnp.tile(jnp.concatenate([-sin, sin], axis=-1), (1, N_KV_HEADS))


def kernel(x_prompt, x_sample, cache_k, cache_v, norm1_g, w_in, q_norm_g, k_norm_g, attn_sinks,
           sgu_ln_g, sgu_ln_b, sgu_w, sgu_b, w_attn_up, w_sgu_up, w_out, norm2_g, w_ff1, w_ff2):
    batch, seq, _ = x_prompt.shape
    dec_batch, dec_seq, _ = x_sample.shape
    tp = batch * seq
    ts = dec_batch * dec_seq
    tm = 768
    tn = 512
    tq = 512
    assert (tp + ts) % tm == 0 and ts < tm and ts % SGU_CHUNK == 0 and (tm - ts) % SGU_CHUNK == 0
    assert seq % tq == 0 and seq % WINDOW == 0

    cos_p, sin_p = _rope_tables(jnp.arange(seq))
    cos_s, sin_s = _rope_tables(jnp.tile(PAST_LEN + jnp.arange(dec_seq), dec_batch))
    cos_t = jnp.concatenate([cos_p] * batch + [cos_s], axis=0)
    sin_t = jnp.concatenate([sin_p] * batch + [sin_s], axis=0)

    w_gate = w_in[:, :, GATE_OFF:].astype(BF16)
    w_au = w_attn_up.astype(BF16)
    w_su = w_sgu_up.astype(BF16)
    w_o = w_out.astype(BF16)

    g1 = norm1_g.reshape(DEPTH, 1, D_MODEL)
    g2 = norm2_g.reshape(DEPTH, 1, D_MODEL)
    qg = jnp.tile(q_norm_g, (1, N_KV_HEADS)).reshape(DEPTH, 1, KV_W)
    kg = jnp.tile(k_norm_g, (1, N_KV_HEADS)).reshape(DEPTH, 1, KV_W)
    lng = sgu_ln_g.reshape(DEPTH, 1, SGU_W)
    lnb = sgu_ln_b.reshape(DEPTH, 1, SGU_W)
    bs_t = jnp.swapaxes(sgu_b, 1, 2)
    ws_s = sgu_w[:, :, :dec_seq, :dec_seq].reshape(DEPTH, -1)
    bs_s = sgu_b[:, :, :dec_seq].reshape(DEPTH, -1)
    ck = cache_k.reshape(DEPTH, dec_batch * WINDOW, KV_W)
    cv = cache_v.reshape(DEPTH, dec_batch * WINDOW, KV_W)

    x = jnp.concatenate([x_prompt.reshape(tp, D_MODEL), x_sample.reshape(ts, D_MODEL)], axis=0)
    kp_l, vp_l, ks_l, vs_l, sv_l = [], [], [], [], []
    for l in range(DEPTH):
        xn, q, k, v = _qkv_call(l, x, g1, w_in, cos_t, sin_t, qg, kg, tm, (tp + ts) // tm)
        m, vsn = _sgu_call(l, xn, w_in, lng, lnb, sgu_w, bs_t, ws_s, bs_s, tm, dec_seq, ts)
        a = _attn_prompt_call(l, attn_sinks, q, k, v, batch, seq, tq)
        a, nk, nv = _attn_sample_call(l, attn_sinks, q, k, v, ck, cv, a, tp, dec_batch, dec_seq, 8)
        h = _merge_call(l, x, xn, a, m, w_gate, w_au, w_su, w_o, tm, tn)
        x = _ffn_call(l, h, g2, w_ff1, w_ff2, tm, tn, 3)
        kp_l.append(k[:tp].reshape(batch, seq, N_KV_HEADS, HEAD_DIM)[:, -WINDOW:])
        vp_l.append(v[:tp].reshape(batch, seq, N_KV_HEADS, HEAD_DIM)[:, -WINDOW:])
        ks_l.append(nk.reshape(dec_batch, WINDOW, N_KV_HEADS, HEAD_DIM))
        vs_l.append(nv.reshape(dec_batch, WINDOW, N_KV_HEADS, HEAD_DIM))
        sv_l.append(vsn.reshape(dec_batch, dec_seq, SGU_GROUPS, SGU_CH))

    return (x[:tp].reshape(batch, seq, D_MODEL), x[tp:].reshape(dec_batch, dec_seq, D_MODEL),
            jnp.stack(kp_l), jnp.stack(vp_l), jnp.stack(ks_l), jnp.stack(vs_l), jnp.stack(sv_l))
```

```python
import functools
import math

import jax
import jax.numpy as jnp
import numpy as np
from jax import lax
from jax.experimental import pallas as pl
from jax.experimental.pallas import tpu as pltpu

D_MODEL = 2048
DEPTH = 4
PAST_LEN = 16384
HEAD_DIM = 64
N_HEADS = 16
N_KV_HEADS = 4
Q_GROUP = 4
ATTN_W = N_HEADS * HEAD_DIM
KV_W = N_KV_HEADS * HEAD_DIM
WINDOW = 128
ROPE_THETA = 10000.0
SGU_CHUNK = 128
SGU_W = D_MODEL // 2
SGU_GROUPS = 8
SGU_CH = SGU_W // SGU_GROUPS
D_FF = 4 * D_MODEL
RMS_EPS = 1e-6
LN_EPS = 1e-5
NEG_INF = -1e30

QKV_W = ATTN_W + 2 * KV_W
UV_OFF = QKV_W
GATE_OFF = QKV_W + 2 * SGU_W

LANES = 128
VMEM_LIMIT = 58 * 1024 * 1024

BF16 = jnp.bfloat16
F32 = jnp.float32


def _dot(a, b):
    return jnp.dot(a, b, preferred_element_type=F32)


def _rms_rows(x, g):
    ms = jnp.mean(x * x, axis=-1, keepdims=True)
    return x * lax.rsqrt(ms + RMS_EPS) * g


def _gelu(x):
    return 0.5 * x * (1.0 + lax.erf(x * np.float32(math.sqrt(0.5))))


def _params(*sem):
    return pltpu.CompilerParams(dimension_semantics=sem, vmem_limit_bytes=VMEM_LIMIT)


def _head_norm_rope(z, gain, cos, sin, gmat, first_half):
    width = z.shape[1]
    sq = z * z
    hi = sq.astype(BF16)
    lo = (sq - hi.astype(F32)).astype(BF16)
    ssum = _dot(jnp.concatenate([hi, lo], axis=1), gmat)
    zn = z * lax.rsqrt(ssum * (1.0 / HEAD_DIM) + RMS_EPS) * gain
    rot = jnp.where(first_half, pltpu.roll(zn, width - HEAD_DIM // 2, 1), pltpu.roll(zn, HEAD_DIM // 2, 1))
    return zn * cos + rot * sin


def _qkv_kernel(x_ref, xs_ref, g1_ref, w_ref, cos_ref, sin_ref, coss_ref, sins_ref, qg_ref, kg_ref,
                xn_ref, q_ref, k_ref, v_ref, xns_ref, qs_ref, ks_ref, vs_ref):
    tm = x_ref.shape[0]
    rows = tm + xs_ref.shape[0]
    cw = KV_W
    g1 = g1_ref[...]
    xn = jnp.concatenate([_rms_rows(x_ref[...], g1).astype(BF16), _rms_rows(xs_ref[...], g1).astype(BF16)],
                         axis=0)
    xn_ref[...] = xn[:tm]
    xns_ref[...] = xn[tm:]
    lane = lax.broadcasted_iota(jnp.int32, (rows, cw), 1)
    first_half = (lane % HEAD_DIM) < (HEAD_DIM // 2)
    r = (lax.broadcasted_iota(jnp.int32, (2 * cw, cw), 0) % cw) // HEAD_DIM
    c = lax.broadcasted_iota(jnp.int32, (2 * cw, cw), 1) // HEAD_DIM
    gmat = jnp.where(r == c, 1.0, 0.0).astype(BF16)
    cos = jnp.concatenate([cos_ref[...], coss_ref[...]], axis=0)
    sin = jnp.concatenate([sin_ref[...], sins_ref[...]], axis=0)
    scale = HEAD_DIM ** -0.5
    n_q = ATTN_W // cw

    def project(c):
        return _dot(xn, w_ref[:, c * cw:(c + 1) * cw].astype(BF16))

    z = project(0)
    for c in range(n_q + 1):
        z_next = project(c + 1)
        if c < n_q:
            q = (_head_norm_rope(z, qg_ref[...], cos, sin, gmat, first_half) * scale).astype(BF16)
            q_ref[:, c * cw:(c + 1) * cw] = q[:tm]
            qs_ref[:, c * cw:(c + 1) * cw] = q[tm:]
        else:
            k = _head_norm_rope(z, kg_ref[...], cos, sin, gmat, first_half)
            k_ref[...] = k[:tm]
            ks_ref[...] = k[tm:]
        z = z_next
    v_ref[...] = z[:tm]
    vs_ref[...] = z[tm:]


def _qkv_call(l, x, xs, g1, w_in, cos_t, sin_t, cos_s, sin_s, qg, kg, tm, n_pos_blocks):
    T = x.shape[0]
    n_tiles = T // tm
    Ts = xs.shape[0]
    ts = Ts // n_tiles
    assert Ts == ts * n_tiles and ts % 16 == 0
    row = lambda i: (i, 0)
    layer = lambda i: (l, 0, 0)
    pos = lambda i: (i % n_pos_blocks, 0)
    return pl.pallas_call(
        _qkv_kernel,
        grid=(n_tiles,),
        in_specs=[
            pl.BlockSpec((tm, D_MODEL), row),
            pl.BlockSpec((ts, D_MODEL), row),
            pl.BlockSpec((None, 1, D_MODEL), layer),
            pl.BlockSpec((None, D_MODEL, QKV_W), layer),
            pl.BlockSpec((tm, KV_W), pos),
            pl.BlockSpec((tm, KV_W), pos),
            pl.BlockSpec((ts, KV_W), row),
            pl.BlockSpec((ts, KV_W), row),
            pl.BlockSpec((None, 1, KV_W), layer),
            pl.BlockSpec((None, 1, KV_W), layer),
        ],
        out_specs=[
            pl.BlockSpec((tm, D_MODEL), row),
            pl.BlockSpec((tm, ATTN_W), row),
            pl.BlockSpec((tm, KV_W), row),
            pl.BlockSpec((tm, KV_W), row),
            pl.BlockSpec((ts, D_MODEL), row),
            pl.BlockSpec((ts, ATTN_W), row),
            pl.BlockSpec((ts, KV_W), row),
            pl.BlockSpec((ts, KV_W), row),
        ],
        out_shape=[
            jax.ShapeDtypeStruct((T, D_MODEL), BF16),
            jax.ShapeDtypeStruct((T, ATTN_W), BF16),
            jax.ShapeDtypeStruct((T, KV_W), F32),
            jax.ShapeDtypeStruct((T, KV_W), F32),
            jax.ShapeDtypeStruct((Ts, D_MODEL), BF16),
            jax.ShapeDtypeStruct((Ts, ATTN_W), BF16),
            jax.ShapeDtypeStruct((Ts, KV_W), F32),
            jax.ShapeDtypeStruct((Ts, KV_W), F32),
        ],
        compiler_params=_params("parallel"),
        name="qkv",
    )(x, xs, g1, w_in, cos_t, sin_t, cos_s, sin_s, qg, kg)


UV_BLOCK = 512
UV_BLOCKS = 2 * SGU_W // UV_BLOCK


def _uv_project(xn, w_refs, lng_ref, lnb_ref, u_scr, vs_scr):
    cw = 2 * LANES
    per_block = UV_BLOCK // cw
    for c in range(2 * SGU_W // cw):
        w = w_refs[c // per_block][:, (c % per_block) * cw:(c % per_block + 1) * cw].astype(BF16)
        dst = u_scr if c < SGU_W // cw else vs_scr
        c_dst = c % (SGU_W // cw)
        dst[:, c_dst * cw:(c_dst + 1) * cw] = _gelu(_dot(xn, w))
    vs = vs_scr[...]
    mu = jnp.mean(vs, axis=-1, keepdims=True)
    d = vs - mu
    var = jnp.mean(d * d, axis=-1, keepdims=True)
    vs_scr[...] = d * lax.rsqrt(var + LN_EPS) * lng_ref[...] + lnb_ref[...]


def _sgu_kernel(wss_ref, bss_ref, xn_ref, xns_ref, *refs, l, seq):
    w_refs = refs[:UV_BLOCKS]
    lng_ref, lnb_ref, ws_ref, bs_ref, m_ref, ms_ref, vsn_ref, u_scr, vs_scr, g_scr = refs[UV_BLOCKS:]
    tm = xn_ref.shape[0]
    nseq = xns_ref.shape[0] // seq
    xn = jnp.concatenate([xn_ref[...], xns_ref[...]], axis=0)
    _uv_project(xn, w_refs, lng_ref, lnb_ref, u_scr, vs_scr)

    t = lax.broadcasted_iota(jnp.int32, (SGU_CHUNK, SGU_CHUNK), 0)
    s = lax.broadcasted_iota(jnp.int32, (SGU_CHUNK, SGU_CHUNK), 1)
    causal = t >= s
    for g in range(SGU_GROUPS):
        wg = jnp.where(causal, ws_ref[g], 0.0).astype(BF16)
        bias = bs_ref[:, g:g + 1]
        cols = slice(g * SGU_CH, (g + 1) * SGU_CH)
        for n in range(tm // SGU_CHUNK):
            rows = slice(n * SGU_CHUNK, (n + 1) * SGU_CHUNK)
            mixed = _dot(wg, vs_scr[rows, cols].astype(BF16)) + bias
            m_ref[rows, cols] = (u_scr[rows, cols] * mixed).astype(BF16)

    vsn_ref[...] = vs_scr[tm:, :]
    for g in range(SGU_GROUPS):
        cols = slice(g * SGU_CH, (g + 1) * SGU_CH)
        g_scr[0] = vs_scr[tm:, cols]
        g_scr[1] = u_scr[tm:, cols]
        for t in range(seq):
            step_t = pl.ds(t, nseq, stride=seq)
            acc = jnp.full((nseq, SGU_CH), bss_ref[l, g * seq + t], F32)
            for s in range(t + 1):
                acc = acc + wss_ref[l, (g * seq + t) * seq + s] * g_scr[0, pl.ds(s, nseq, stride=seq), :]
            g_scr[2, step_t, :] = g_scr[1, step_t, :] * acc
        ms_ref[:, cols] = g_scr[2].astype(BF16)


def _uv_weight_specs(l):
    return [pl.BlockSpec((None, D_MODEL, UV_BLOCK), functools.partial(lambda b, i: (l, 0, b), UV_OFF // UV_BLOCK + k))
            for k in range(UV_BLOCKS)]


def _sgu_call(l, xn, xns, w_in, lng, lnb, ws, bs_t, ws_sample, bs_sample, tm, seq):
    T = xn.shape[0]
    n_tiles = T // tm
    Ts = xns.shape[0]
    ts = Ts // n_tiles
    assert Ts == ts * n_tiles and ts % 16 == 0 and ts % seq == 0
    row = lambda i: (i, 0)
    layer = lambda i: (l, 0, 0)
    smem = pl.BlockSpec(memory_space=pltpu.SMEM)
    return pl.pallas_call(
        functools.partial(_sgu_kernel, l=l, seq=seq),
        grid=(n_tiles,),
        in_specs=[
            smem,
            smem,
            pl.BlockSpec((tm, D_MODEL), row),
            pl.BlockSpec((ts, D_MODEL), row),
            *_uv_weight_specs(l),
            pl.BlockSpec((None, 1, SGU_W), layer),
            pl.BlockSpec((None, 1, SGU_W), layer),
            pl.BlockSpec((None, SGU_GROUPS, SGU_CHUNK, SGU_CHUNK), lambda i: (l, 0, 0, 0)),
            pl.BlockSpec((None, SGU_CHUNK, SGU_GROUPS), layer),
        ],
        out_specs=[pl.BlockSpec((tm, SGU_W), row), pl.BlockSpec((ts, SGU_W), row),
                   pl.BlockSpec((ts, SGU_W), row)],
        out_shape=[jax.ShapeDtypeStruct((T, SGU_W), BF16), jax.ShapeDtypeStruct((Ts, SGU_W), BF16),
                   jax.ShapeDtypeStruct((Ts, SGU_W), F32)],
        scratch_shapes=[pltpu.VMEM((tm + ts, SGU_W), F32), pltpu.VMEM((tm + ts, SGU_W), F32),
                        pltpu.VMEM((3, ts, SGU_CH), F32)],
        compiler_params=_params("parallel"),
        name="sgu",
    )(ws_sample, bs_sample, xn, xns, *([w_in] * UV_BLOCKS), lng, lnb, ws, bs_t)


def _attn_prompt_kernel(sink_ref, q_ref, kp_ref, kc_ref, vp_ref, vc_ref, o_ref, *, l):
    blk = WINDOW
    nq = Q_GROUP * blk
    first_tile = pl.program_id(1) == 0
    c = lax.broadcasted_iota(jnp.int32, (blk, nq), 0)
    r = lax.broadcasted_iota(jnp.int32, (blk, nq), 1) % blk
    from_prev = c > r
    k_all = jnp.concatenate([kp_ref[...], kc_ref[...]], axis=0).astype(BF16)
    vt_all = jnp.concatenate([vp_ref[...], vc_ref[...]], axis=0).T.astype(BF16)
    sinks = [jnp.concatenate([jnp.full((1, blk), sink_ref[l, g * Q_GROUP + j], F32) for j in range(Q_GROUP)],
                             axis=1) for g in range(N_KV_HEADS)]

    def scores(jb):
        out = []
        for g in range(N_KV_HEADS):
            qg = jnp.concatenate([q_ref[jb * blk:(jb + 1) * blk, h * HEAD_DIM:(h + 1) * HEAD_DIM]
                                  for h in range(g * Q_GROUP, (g + 1) * Q_GROUP)], axis=0)
            kg = k_all[jb * blk:(jb + 2) * blk, g * HEAD_DIM:(g + 1) * HEAD_DIM]
            out.append(lax.dot_general(kg, qg, (((1,), (1,)), ((), ())), preferred_element_type=F32))
        return out

    nblk = q_ref.shape[0] // blk
    s2 = scores(0)
    for jb in range(nblk):
        s2_next = scores(jb + 1) if jb + 1 < nblk else None
        for g in range(N_KV_HEADS):
            s_prev = s2[g][:blk]
            if jb == 0:
                s_prev = jnp.where(first_tile, NEG_INF, s_prev)
            s = jnp.where(from_prev, s_prev, s2[g][blk:])
            m = jnp.maximum(jnp.max(s, axis=0, keepdims=True), sinks[g])
            p = jnp.exp(s - m)
            p = p / (jnp.sum(p, axis=0, keepdims=True) + jnp.exp(sinks[g] - m))
            p2 = jnp.concatenate([jnp.where(from_prev, p, 0.0), jnp.where(from_prev, 0.0, p)], axis=0)
            vt = vt_all[g * HEAD_DIM:(g + 1) * HEAD_DIM, jb * blk:(jb + 2) * blk]
            o = _dot(vt, p2.astype(BF16)).T
            for j in range(Q_GROUP):
                h = g * Q_GROUP + j
                o_ref[jb * blk:(jb + 1) * blk, h * HEAD_DIM:(h + 1) * HEAD_DIM] = (
                    o[j * blk:(j + 1) * blk].astype(BF16))
        s2 = s2_next


def _attn_prompt_call(l, sinks, q, k, v, batch, seq, tq):
    nt = seq // tq
    blocks_per_tile = tq // WINDOW
    cur = lambda b, i: (b * nt + i, 0)
    prev = lambda b, i: (b * (seq // WINDOW) + jnp.maximum(i * blocks_per_tile - 1, 0), 0)
    return pl.pallas_call(
        functools.partial(_attn_prompt_kernel, l=l),
        grid=(batch, nt),
        in_specs=[
            pl.BlockSpec(memory_space=pltpu.SMEM),
            pl.BlockSpec((tq, ATTN_W), cur),
            pl.BlockSpec((WINDOW, KV_W), prev),
            pl.BlockSpec((tq, KV_W), cur),
            pl.BlockSpec((WINDOW, KV_W), prev),
            pl.BlockSpec((tq, KV_W), cur),
        ],
        out_specs=pl.BlockSpec((tq, ATTN_W), cur),
        out_shape=jax.ShapeDtypeStruct((batch * seq, ATTN_W), BF16),
        compiler_params=_params("parallel", "parallel"),
        name="attn_prompt",
    )(sinks, q, k, k, v, v)


def _attn_sample_kernel(sink_ref, q_ref, kn_ref, vn_ref, ck_ref, cv_ref, o_ref, nk_ref, nv_ref, *, l, seq):
    nseq = q_ref.shape[0] // seq
    nq = nseq * seq
    rows = Q_GROUP * nq
    ri = lax.broadcasted_iota(jnp.int32, (rows, nseq * WINDOW), 0) % nq
    ci = lax.broadcasted_iota(jnp.int32, (rows, nseq * WINDOW), 1)
    mask_c = (ri // seq == ci // WINDOW) & (ci % WINDOW > ri % seq)
    ri = lax.broadcasted_iota(jnp.int32, (rows, nq), 0) % nq
    ci = lax.broadcasted_iota(jnp.int32, (rows, nq), 1)
    mask_n = (ri // seq == ci // seq) & (ci % seq <= ri % seq)
    dn = (((1,), (1,)), ((), ()))
    for g in range(N_KV_HEADS):
        cols = slice(g * HEAD_DIM, (g + 1) * HEAD_DIM)
        heads = range(g * Q_GROUP, (g + 1) * Q_GROUP)
        qg = jnp.concatenate([q_ref[:, h * HEAD_DIM:(h + 1) * HEAD_DIM] for h in heads], axis=0)
        sink = jnp.concatenate([jnp.full((nq, 1), sink_ref[l, h], F32) for h in heads], axis=0)
        s_c = lax.dot_general(qg, ck_ref[:, cols].astype(BF16), dn, preferred_element_type=F32)
        s_n = lax.dot_general(qg, kn_ref[:, cols].astype(BF16), dn, preferred_element_type=F32)
        s_c = jnp.where(mask_c, s_c, NEG_INF)
        s_n = jnp.where(mask_n, s_n, NEG_INF)
        m = jnp.maximum(jnp.maximum(jnp.max(s_c, axis=-1, keepdims=True),
                                    jnp.max(s_n, axis=-1, keepdims=True)), sink)
        p_c = jnp.exp(s_c - m)
        p_n = jnp.exp(s_n - m)
        denom = (jnp.sum(p_c, axis=-1, keepdims=True) + jnp.sum(p_n, axis=-1, keepdims=True)
                 + jnp.exp(sink - m))
        o = (_dot((p_c / denom).astype(BF16), cv_ref[:, cols].astype(BF16))
             + _dot((p_n / denom).astype(BF16), vn_ref[:, cols].astype(BF16)))
        for j, h in enumerate(heads):
            o_ref[:, h * HEAD_DIM:(h + 1) * HEAD_DIM] = o[j * nq:(j + 1) * nq].astype(BF16)
    for b in range(nseq):
        nk_ref[b * WINDOW:(b + 1) * WINDOW - seq, :] = ck_ref[b * WINDOW + seq:(b + 1) * WINDOW, :]
        nv_ref[b * WINDOW:(b + 1) * WINDOW - seq, :] = cv_ref[b * WINDOW + seq:(b + 1) * WINDOW, :]
        nk_ref[(b + 1) * WINDOW - seq:(b + 1) * WINDOW, :] = kn_ref[b * seq:(b + 1) * seq, :]
        nv_ref[(b + 1) * WINDOW - seq:(b + 1) * WINDOW, :] = vn_ref[b * seq:(b + 1) * seq, :]


def _attn_sample_call(l, sinks, q, k, v, ck, cv, seq, seq_block):
    T = q.shape[0]
    nseq = T // seq
    row = lambda i: (i, 0)
    cache = lambda i: (l, i, 0)
    return pl.pallas_call(
        functools.partial(_attn_sample_kernel, l=l, seq=seq),
        grid=(nseq // seq_block,),
        in_specs=[
            pl.BlockSpec(memory_space=pltpu.SMEM),
            pl.BlockSpec((seq_block * seq, ATTN_W), row),
            pl.BlockSpec((seq_block * seq, KV_W), row),
            pl.BlockSpec((seq_block * seq, KV_W), row),
            pl.BlockSpec((None, seq_block * WINDOW, KV_W), cache),
            pl.BlockSpec((None, seq_block * WINDOW, KV_W), cache),
        ],
        out_specs=[
            pl.BlockSpec((seq_block * seq, ATTN_W), row),
            pl.BlockSpec((seq_block * WINDOW, KV_W), row),
            pl.BlockSpec((seq_block * WINDOW, KV_W), row),
        ],
        out_shape=[
            jax.ShapeDtypeStruct((T, ATTN_W), BF16),
            jax.ShapeDtypeStruct((nseq * WINDOW, KV_W), F32),
            jax.ShapeDtypeStruct((nseq * WINDOW, KV_W), F32),
        ],
        compiler_params=_params("parallel"),
        name="attn_sample",
    )(sinks, q, k, v, ck, cv)


def _merge_kernel(x_ref, xn_ref, a_ref, m_ref, wga_ref, wgm_ref, wau_ref, wsu_ref, wout_ref,
                  h_ref, merged_scr):
    j = pl.program_id(1)
    xn = xn_ref[...]
    ga = jax.nn.sigmoid(_dot(xn, wga_ref[...]))
    gm = jax.nn.sigmoid(_dot(xn, wgm_ref[...]))
    merged = ga * _dot(a_ref[...], wau_ref[...]) + gm * _dot(m_ref[...], wsu_ref[...])
    merged_scr[j] = merged.astype(BF16)

    @pl.when(j == pl.num_programs(1) - 1)
    def _():
        tn = merged_scr.shape[2]
        acc = x_ref[...]
        for c in range(merged_scr.shape[0]):
            acc = acc + _dot(merged_scr[c], wout_ref[c * tn:(c + 1) * tn, :])
        h_ref[...] = acc


def _merge_call(l, x, xn, a, m, w_gate, w_au, w_su, w_out, tm, tn):
    T = x.shape[0]
    nj = D_MODEL // tn
    row = lambda i, j: (i, 0)
    return pl.pallas_call(
        _merge_kernel,
        grid=(T // tm, nj),
        in_specs=[
            pl.BlockSpec((tm, D_MODEL), row),
            pl.BlockSpec((tm, D_MODEL), row),
            pl.BlockSpec((tm, ATTN_W), row),
            pl.BlockSpec((tm, SGU_W), row),
            pl.BlockSpec((None, D_MODEL, tn), lambda i, j: (l, 0, j)),
            pl.BlockSpec((None, D_MODEL, tn), lambda i, j: (l, 0, nj + j)),
            pl.BlockSpec((None, ATTN_W, tn), lambda i, j: (l, 0, j)),
            pl.BlockSpec((None, SGU_W, tn), lambda i, j: (l, 0, j)),
            pl.BlockSpec((None, D_MODEL, D_MODEL), lambda i, j: (l, 0, 0)),
        ],
        out_specs=pl.BlockSpec((tm, D_MODEL), row),
        out_shape=jax.ShapeDtypeStruct((T, D_MODEL), F32),
        scratch_shapes=[pltpu.VMEM((nj, tm, tn), BF16)],
        compiler_params=_params("parallel", "arbitrary"),
        name="merge",
    )(x, xn, a, m, w_gate, w_gate, w_au, w_su, w_out)


def _ffn_kernel(h_ref, hs_ref, g2_ref, w1_ref, w2_ref, y_ref, ys_ref, hn_scr):
    tm = h_ref.shape[0]

    @pl.when(pl.program_id(1) == 0)
    def _():
        h = h_ref[...]
        hn_scr[:tm, :] = _rms_rows(h, g2_ref[...]).astype(BF16)
        y_ref[...] = h
        hs = hs_ref[...]
        hn_scr[tm:, :] = _rms_rows(hs, g2_ref[...]).astype(BF16)
        ys_ref[...] = hs

    half = tm // 2
    w1 = w1_ref[...].astype(BF16)
    w2 = w2_ref[...].astype(BF16)
    acts = []
    for rows in (slice(0, half), slice(half, hn_scr.shape[0])):
        a = jnp.maximum(_dot(hn_scr[rows, :], w1), 0.0)
        acts.append((a * a).astype(BF16))
    y_ref[:half, :] += _dot(acts[0], w2)
    second = _dot(acts[1], w2)
    y_ref[half:, :] += second[:half]
    ys_ref[...] += second[half:]


def _ffn_call(l, h, hs, g2, w1, w2, tm, tf):
    T = h.shape[0]
    n_tiles = T // tm
    ts = hs.shape[0] // n_tiles
    assert hs.shape[0] == ts * n_tiles and ts % 16 == 0
    row = lambda i, f: (i, 0)
    return pl.pallas_call(
        _ffn_kernel,
        grid=(n_tiles, D_FF // tf),
        in_specs=[
            pl.BlockSpec((tm, D_MODEL), row),
            pl.BlockSpec((ts, D_MODEL), row),
            pl.BlockSpec((None, 1, D_MODEL), lambda i, f: (l, 0, 0)),
            pl.BlockSpec((None, D_MODEL, tf), lambda i, f: (l, 0, f)),
            pl.BlockSpec((None, tf, D_MODEL), lambda i, f: (l, f, 0)),
        ],
        out_specs=[pl.BlockSpec((tm, D_MODEL), row), pl.BlockSpec((ts, D_MODEL), row)],
        out_shape=[jax.ShapeDtypeStruct((T, D_MODEL), F32), jax.ShapeDtypeStruct(hs.shape, F32)],
        scratch_shapes=[pltpu.VMEM((tm + ts, D_MODEL), BF16)],
        compiler_params=_params("parallel", "arbitrary"),
        name="ffn",
    )(h, hs, g2, w1, w2)


def _rope_tables(pos):
    half = HEAD_DIM // 2
    inv = jnp.power(jnp.float32(ROPE_THETA), -jnp.arange(half, dtype=jnp.float32) / half)
    ang = pos.astype(jnp.float32)[:, None] * inv[None, :]
    cos = jnp.cos(ang)
    sin = jnp.sin(ang)
    return jnp.tile(cos, (1, 2 * N_KV_HEADS)), jnp.tile(jnp.concatenate([-sin, sin], axis=-1), (1, N_KV_HEADS))


def kernel(x_prompt, x_sample, cache_k, cache_v, norm1_g, w_in, q_norm_g, k_norm_g, attn_sinks,
           sgu_ln_g, sgu_ln_b, sgu_w, sgu_b, w_attn_up, w_sgu_up, w_out, norm2_g, w_ff1, w_ff2):
    batch, seq, _ = x_prompt.shape
    dec_batch, dec_seq, _ = x_sample.shape
    tp = batch * seq
    ts = dec_batch * dec_seq
    tm = 512
    tn = 512
    tq = 512
    assert seq % tm == 0 and tm % SGU_CHUNK == 0 and seq % tq == 0 and tq % WINDOW == 0

    cos_p, sin_p = _rope_tables(jnp.arange(seq))
    cos_s, sin_s = _rope_tables(jnp.tile(PAST_LEN + jnp.arange(dec_seq), dec_batch))

    w_gate = w_in[:, :, GATE_OFF:].astype(BF16)
    w_au = w_attn_up.astype(BF16)
    w_su = w_sgu_up.astype(BF16)
    w_o = w_out.astype(BF16)

    g1 = norm1_g.reshape(DEPTH, 1, D_MODEL)
    g2 = norm2_g.reshape(DEPTH, 1, D_MODEL)
    qg = jnp.tile(q_norm_g, (1, N_KV_HEADS)).reshape(DEPTH, 1, KV_W)
    kg = jnp.tile(k_norm_g, (1, N_KV_HEADS)).reshape(DEPTH, 1, KV_W)
    lng = sgu_ln_g.reshape(DEPTH, 1, SGU_W)
    lnb = sgu_ln_b.reshape(DEPTH, 1, SGU_W)
    bs_t = jnp.swapaxes(sgu_b, 1, 2)
    ws_s = sgu_w[:, :, :dec_seq, :dec_seq].reshape(DEPTH, -1)
    bs_s = sgu_b[:, :, :dec_seq].reshape(DEPTH, -1)
    ck = cache_k.reshape(DEPTH, dec_batch * WINDOW, KV_W)
    cv = cache_v.reshape(DEPTH, dec_batch * WINDOW, KV_W)

    xp = x_prompt.reshape(tp, D_MODEL)
    xs = x_sample.reshape(ts, D_MODEL)
    kp_l, vp_l, ks_l, vs_l, sv_l = [], [], [], [], []
    for l in range(DEPTH):
        xn, q, k, v, xns, qs, ks, vs = _qkv_call(l, xp, xs, g1, w_in, cos_p, sin_p, cos_s, sin_s, qg, kg,
                                                 tm, seq // tm)
        m, ms, vsn = _sgu_call(l, xn, xns, w_in, lng, lnb, sgu_w, bs_t, ws_s, bs_s, tm, dec_seq)
        a = _attn_prompt_call(l, attn_sinks, q, k, v, batch, seq, tq)
        hp = _merge_call(l, xp, xn, a, m, w_gate, w_au, w_su, w_o, tm, tn)
        a, nk, nv = _attn_sample_call(l, attn_sinks, qs, ks, vs, ck, cv, dec_seq, 8)
        hs = _merge_call(l, xs, xns, a, ms, w_gate, w_au, w_su, w_o, ts, tn)
        xp, xs = _ffn_call(l, hp, hs, g2, w_ff1, w_ff2, 2 * tm, tn)
        kp_l.append(k.reshape(batch, seq, N_KV_HEADS, HEAD_DIM)[:, -WINDOW:])
        vp_l.append(v.reshape(batch, seq, N_KV_HEADS, HEAD_DIM)[:, -WINDOW:])
        ks_l.append(nk.reshape(dec_batch, WINDOW, N_KV_HEADS, HEAD_DIM))
        vs_l.append(nv.reshape(dec_batch, WINDOW, N_KV_HEADS, HEAD_DIM))
        sv_l.append(vsn.reshape(dec_batch, dec_seq, SGU_GROUPS, SGU_CH))

    return (xp.reshape(batch, seq, D_MODEL), xs.reshape(dec_batch, dec_seq, D_MODEL),
            jnp.stack(kp_l), jnp.stack(vp_l), jnp.stack(ks_l), jnp.stack(vs_l), jnp.stack(sv_l))
```

```python
import functools
import math

import jax
import jax.numpy as jnp
import numpy as np
from jax import lax
from jax.experimental import pallas as pl
from jax.experimental.pallas import tpu as pltpu

D_MODEL = 2048
DEPTH = 4
PAST_LEN = 16384
HEAD_DIM = 64
N_HEADS = 16
N_KV_HEADS = 4
Q_GROUP = 4
ATTN_W = N_HEADS * HEAD_DIM
KV_W = N_KV_HEADS * HEAD_DIM
WINDOW = 128
ROPE_THETA = 10000.0
SGU_CHUNK = 128
SGU_W = D_MODEL // 2
SGU_GROUPS = 8
SGU_CH = SGU_W // SGU_GROUPS
D_FF = 4 * D_MODEL
RMS_EPS = 1e-6
LN_EPS = 1e-5
NEG_INF = -1e30

QKV_W = ATTN_W + 2 * KV_W
UV_OFF = QKV_W
GATE_OFF = QKV_W + 2 * SGU_W

LANES = 128
VMEM_LIMIT = 58 * 1024 * 1024

BF16 = jnp.bfloat16
F32 = jnp.float32


def _dot(a, b):
    return jnp.dot(a, b, preferred_element_type=F32)


def _rms_rows(x, g):
    ms = jnp.mean(x * x, axis=-1, keepdims=True)
    return x * lax.rsqrt(ms + RMS_EPS) * g


def _gelu(x):
    return 0.5 * x * (1.0 + lax.erf(x * np.float32(math.sqrt(0.5))))


def _params(*sem):
    return pltpu.CompilerParams(dimension_semantics=sem, vmem_limit_bytes=VMEM_LIMIT)


def _head_norm_rope(z, gain, cos, sin, gmat, first_half):
    width = z.shape[1]
    sq = z * z
    hi = sq.astype(BF16)
    lo = (sq - hi.astype(F32)).astype(BF16)
    ssum = _dot(jnp.concatenate([hi, lo], axis=1), gmat)
    zn = z * lax.rsqrt(ssum * (1.0 / HEAD_DIM) + RMS_EPS) * gain
    rot = jnp.where(first_half, pltpu.roll(zn, width - HEAD_DIM // 2, 1), pltpu.roll(zn, HEAD_DIM // 2, 1))
    return zn * cos + rot * sin


def _qkv_kernel(x_ref, xs_ref, g1_ref, w_ref, cos_ref, sin_ref, coss_ref, sins_ref, qg_ref, kg_ref,
                xn_ref, q_ref, k_ref, v_ref, xns_ref, qs_ref, ks_ref, vs_ref):
    tm = x_ref.shape[0]
    rows = tm + xs_ref.shape[0]
    cw = KV_W
    g1 = g1_ref[...]
    xn = jnp.concatenate([_rms_rows(x_ref[...], g1).astype(BF16), _rms_rows(xs_ref[...], g1).astype(BF16)],
                         axis=0)
    xn_ref[...] = xn[:tm]
    xns_ref[...] = xn[tm:]
    lane = lax.broadcasted_iota(jnp.int32, (rows, cw), 1)
    first_half = (lane % HEAD_DIM) < (HEAD_DIM // 2)
    r = (lax.broadcasted_iota(jnp.int32, (2 * cw, cw), 0) % cw) // HEAD_DIM
    c = lax.broadcasted_iota(jnp.int32, (2 * cw, cw), 1) // HEAD_DIM
    gmat = jnp.where(r == c, 1.0, 0.0).astype(BF16)
    cos = jnp.concatenate([cos_ref[...], coss_ref[...]], axis=0)
    sin = jnp.concatenate([sin_ref[...], sins_ref[...]], axis=0)
    scale = HEAD_DIM ** -0.5
    n_q = ATTN_W // cw

    def project(c):
        return _dot(xn, w_ref[:, c * cw:(c + 1) * cw].astype(BF16))

    z = project(0)
    for c in range(n_q + 1):
        z_next = project(c + 1)
        if c < n_q:
            q = (_head_norm_rope(z, qg_ref[...], cos, sin, gmat, first_half) * scale).astype(BF16)
            q_ref[:, c * cw:(c + 1) * cw] = q[:tm]
            qs_ref[:, c * cw:(c + 1) * cw] = q[tm:]
        else:
            k = _head_norm_rope(z, kg_ref[...], cos, sin, gmat, first_half)
            k_ref[...] = k[:tm]
            ks_ref[...] = k[tm:]
        z = z_next
    v_ref[...] = z[:tm]
    vs_ref[...] = z[tm:]


def _qkv_call(l, x, xs, g1, w_in, cos_t, sin_t, cos_s, sin_s, qg, kg, tm, n_pos_blocks):
    T = x.shape[0]
    n_tiles = T // tm
    Ts = xs.shape[0]
    ts = Ts // n_tiles
    assert Ts == ts * n_tiles and ts % 16 == 0
    row = lambda i: (i, 0)
    layer = lambda i: (l, 0, 0)
    pos = lambda i: (i % n_pos_blocks, 0)
    return pl.pallas_call(
        _qkv_kernel,
        grid=(n_tiles,),
        in_specs=[
            pl.BlockSpec((tm, D_MODEL), row),
            pl.BlockSpec((ts, D_MODEL), row),
            pl.BlockSpec((None, 1, D_MODEL), layer),
            pl.BlockSpec((None, D_MODEL, QKV_W), layer),
            pl.BlockSpec((tm, KV_W), pos),
            pl.BlockSpec((tm, KV_W), pos),
            pl.BlockSpec((ts, KV_W), row),
            pl.BlockSpec((ts, KV_W), row),
            pl.BlockSpec((None, 1, KV_W), layer),
            pl.BlockSpec((None, 1, KV_W), layer),
        ],
        out_specs=[
            pl.BlockSpec((tm, D_MODEL), row),
            pl.BlockSpec((tm, ATTN_W), row),
            pl.BlockSpec((tm, KV_W), row),
            pl.BlockSpec((tm, KV_W), row),
            pl.BlockSpec((ts, D_MODEL), row),
            pl.BlockSpec((ts, ATTN_W), row),
            pl.BlockSpec((ts, KV_W), row),
            pl.BlockSpec((ts, KV_W), row),
        ],
        out_shape=[
            jax.ShapeDtypeStruct((T, D_MODEL), BF16),
            jax.ShapeDtypeStruct((T, ATTN_W), BF16),
            jax.ShapeDtypeStruct((T, KV_W), F32),
            jax.ShapeDtypeStruct((T, KV_W), F32),
            jax.ShapeDtypeStruct((Ts, D_MODEL), BF16),
            jax.ShapeDtypeStruct((Ts, ATTN_W), BF16),
            jax.ShapeDtypeStruct((Ts, KV_W), F32),
            jax.ShapeDtypeStruct((Ts, KV_W), F32),
        ],
        compiler_params=_params("parallel"),
        name="qkv",
    )(x, xs, g1, w_in, cos_t, sin_t, cos_s, sin_s, qg, kg)


UV_BLOCK = 512
UV_BLOCKS = 2 * SGU_W // UV_BLOCK


def _uv_project(xn, w_refs, lng_ref, lnb_ref, u_scr, vs_scr):
    cw = 2 * LANES
    per_block = UV_BLOCK // cw
    n_u = SGU_W // cw
    for c in list(range(n_u, 2 * n_u)) + list(range(n_u)):
        w = w_refs[c // per_block][:, (c % per_block) * cw:(c % per_block + 1) * cw].astype(BF16)
        dst = u_scr if c < n_u else vs_scr
        c_dst = c % n_u
        dst[:, c_dst * cw:(c_dst + 1) * cw] = _gelu(_dot(xn, w))
    vs = vs_scr[...]
    mu = jnp.mean(vs, axis=-1, keepdims=True)
    d = vs - mu
    var = jnp.mean(d * d, axis=-1, keepdims=True)
    vs_scr[...] = d * lax.rsqrt(var + LN_EPS) * lng_ref[...] + lnb_ref[...]


def _sgu_kernel(wss_ref, bss_ref, xn_ref, xns_ref, *refs, l, seq):
    w_refs = refs[:UV_BLOCKS]
    lng_ref, lnb_ref, ws_ref, bs_ref, m_ref, ms_ref, vsn_ref, u_scr, vs_scr, g_scr = refs[UV_BLOCKS:]
    tm = xn_ref.shape[0]
    nseq = xns_ref.shape[0] // seq
    xn = jnp.concatenate([xn_ref[...], xns_ref[...]], axis=0)
    _uv_project(xn, w_refs, lng_ref, lnb_ref, u_scr, vs_scr)

    t = lax.broadcasted_iota(jnp.int32, (SGU_CHUNK, SGU_CHUNK), 0)
    s = lax.broadcasted_iota(jnp.int32, (SGU_CHUNK, SGU_CHUNK), 1)
    causal = t >= s
    for g in range(SGU_GROUPS):
        wg = jnp.where(causal, ws_ref[g], 0.0).astype(BF16)
        bias = bs_ref[:, g:g + 1]
        cols = slice(g * SGU_CH, (g + 1) * SGU_CH)
        for n in range(tm // SGU_CHUNK):
            rows = slice(n * SGU_CHUNK, (n + 1) * SGU_CHUNK)
            mixed = _dot(wg, vs_scr[rows, cols].astype(BF16)) + bias
            m_ref[rows, cols] = (u_scr[rows, cols] * mixed).astype(BF16)

    vsn_ref[...] = vs_scr[tm:, :]
    for g in range(SGU_GROUPS):
        cols = slice(g * SGU_CH, (g + 1) * SGU_CH)
        g_scr[0] = vs_scr[tm:, cols]
        g_scr[1] = u_scr[tm:, cols]
        for t in range(seq):
            step_t = pl.ds(t, nseq, stride=seq)
            acc = jnp.full((nseq, SGU_CH), bss_ref[l, g * seq + t], F32)
            for s in range(t + 1):
                acc = acc + wss_ref[l, (g * seq + t) * seq + s] * g_scr[0, pl.ds(s, nseq, stride=seq), :]
            g_scr[2, step_t, :] = g_scr[1, step_t, :] * acc
        ms_ref[:, cols] = g_scr[2].astype(BF16)


def _uv_weight_specs(l):
    return [pl.BlockSpec((None, D_MODEL, UV_BLOCK), functools.partial(lambda b, i: (l, 0, b), UV_OFF // UV_BLOCK + k))
            for k in range(UV_BLOCKS)]


def _sgu_call(l, xn, xns, w_in, lng, lnb, ws, bs_t, ws_sample, bs_sample, tm, seq):
    T = xn.shape[0]
    n_tiles = T // tm
    Ts = xns.shape[0]
    ts = Ts // n_tiles
    assert Ts == ts * n_tiles and ts % 16 == 0 and ts % seq == 0
    row = lambda i: (i, 0)
    layer = lambda i: (l, 0, 0)
    smem = pl.BlockSpec(memory_space=pltpu.SMEM)
    return pl.pallas_call(
        functools.partial(_sgu_kernel, l=l, seq=seq),
        grid=(n_tiles,),
        in_specs=[
            smem,
            smem,
            pl.BlockSpec((tm, D_MODEL), row),
            pl.BlockSpec((ts, D_MODEL), row),
            *_uv_weight_specs(l),
            pl.BlockSpec((None, 1, SGU_W), layer),
            pl.BlockSpec((None, 1, SGU_W), layer),
            pl.BlockSpec((None, SGU_GROUPS, SGU_CHUNK, SGU_CHUNK), lambda i: (l, 0, 0, 0)),
            pl.BlockSpec((None, SGU_CHUNK, SGU_GROUPS), layer),
        ],
        out_specs=[pl.BlockSpec((tm, SGU_W), row), pl.BlockSpec((ts, SGU_W), row),
                   pl.BlockSpec((ts, SGU_W), row)],
        out_shape=[jax.ShapeDtypeStruct((T, SGU_W), BF16), jax.ShapeDtypeStruct((Ts, SGU_W), BF16),
                   jax.ShapeDtypeStruct((Ts, SGU_W), F32)],
        scratch_shapes=[pltpu.VMEM((tm + ts, SGU_W), F32), pltpu.VMEM((tm + ts, SGU_W), F32),
                        pltpu.VMEM((3, ts, SGU_CH), F32)],
        compiler_params=_params("parallel"),
        name="sgu",
    )(ws_sample, bs_sample, xn, xns, *([w_in] * UV_BLOCKS), lng, lnb, ws, bs_t)


def _attn_prompt_kernel(sink_ref, q_ref, kp_ref, kc_ref, vp_ref, vc_ref, o_ref, *, l):
    blk = WINDOW
    nq = Q_GROUP * blk
    first_tile = pl.program_id(1) == 0
    c = lax.broadcasted_iota(jnp.int32, (blk, nq), 0)
    r = lax.broadcasted_iota(jnp.int32, (blk, nq), 1) % blk
    from_prev = c > r
    k_all = jnp.concatenate([kp_ref[...], kc_ref[...]], axis=0).astype(BF16)
    vt_all = jnp.concatenate([vp_ref[...], vc_ref[...]], axis=0).T.astype(BF16)
    sinks = [jnp.concatenate([jnp.full((1, blk), sink_ref[l, g * Q_GROUP + j], F32) for j in range(Q_GROUP)],
                             axis=1) for g in range(N_KV_HEADS)]

    def scores(jb):
        out = []
        for g in range(N_KV_HEADS):
            qg = jnp.concatenate([q_ref[jb * blk:(jb + 1) * blk, h * HEAD_DIM:(h + 1) * HEAD_DIM]
                                  for h in range(g * Q_GROUP, (g + 1) * Q_GROUP)], axis=0)
            kg = k_all[jb * blk:(jb + 2) * blk, g * HEAD_DIM:(g + 1) * HEAD_DIM]
            out.append(lax.dot_general(kg, qg, (((1,), (1,)), ((), ())), preferred_element_type=F32))
        return out

    nblk = q_ref.shape[0] // blk
    s2 = scores(0)
    for jb in range(nblk):
        s2_next = scores(jb + 1) if jb + 1 < nblk else None
        for g in range(N_KV_HEADS):
            s_prev = s2[g][:blk]
            if jb == 0:
                s_prev = jnp.where(first_tile, NEG_INF, s_prev)
            s = jnp.where(from_prev, s_prev, s2[g][blk:])
            m = jnp.maximum(jnp.max(s, axis=0, keepdims=True), sinks[g])
            p = jnp.exp(s - m)
            p = p / (jnp.sum(p, axis=0, keepdims=True) + jnp.exp(sinks[g] - m))
            p2 = jnp.concatenate([jnp.where(from_prev, p, 0.0), jnp.where(from_prev, 0.0, p)], axis=0)
            vt = vt_all[g * HEAD_DIM:(g + 1) * HEAD_DIM, jb * blk:(jb + 2) * blk]
            o = _dot(vt, p2.astype(BF16)).T
            for j in range(Q_GROUP):
                h = g * Q_GROUP + j
                o_ref[jb * blk:(jb + 1) * blk, h * HEAD_DIM:(h + 1) * HEAD_DIM] = (
                    o[j * blk:(j + 1) * blk].astype(BF16))
        s2 = s2_next


def _attn_prompt_call(l, sinks, q, k, v, batch, seq, tq):
    nt = seq // tq
    blocks_per_tile = tq // WINDOW
    cur = lambda b, i: (b * nt + i, 0)
    prev = lambda b, i: (b * (seq // WINDOW) + jnp.maximum(i * blocks_per_tile - 1, 0), 0)
    return pl.pallas_call(
        functools.partial(_attn_prompt_kernel, l=l),
        grid=(batch, nt),
        in_specs=[
            pl.BlockSpec(memory_space=pltpu.SMEM),
            pl.BlockSpec((tq, ATTN_W), cur),
            pl.BlockSpec((WINDOW, KV_W), prev),
            pl.BlockSpec((tq, KV_W), cur),
            pl.BlockSpec((WINDOW, KV_W), prev),
            pl.BlockSpec((tq, KV_W), cur),
        ],
        out_specs=pl.BlockSpec((tq, ATTN_W), cur),
        out_shape=jax.ShapeDtypeStruct((batch * seq, ATTN_W), BF16),
        compiler_params=_params("parallel", "parallel"),
        name="attn_prompt",
    )(sinks, q, k, k, v, v)


def _attn_sample_kernel(sink_ref, q_ref, kn_ref, vn_ref, ck_ref, cv_ref, o_ref, nk_ref, nv_ref, *, l, seq):
    nseq = q_ref.shape[0] // seq
    nq = nseq * seq
    rows = Q_GROUP * nq
    ri = lax.broadcasted_iota(jnp.int32, (rows, nseq * WINDOW), 0) % nq
    ci = lax.broadcasted_iota(jnp.int32, (rows, nseq * WINDOW), 1)
    mask_c = (ri // seq == ci // WINDOW) & (ci % WINDOW > ri % seq)
    ri = lax.broadcasted_iota(jnp.int32, (rows, nq), 0) % nq
    ci = lax.broadcasted_iota(jnp.int32, (rows, nq), 1)
    mask_n = (ri // seq == ci // seq) & (ci % seq <= ri % seq)
    dn = (((1,), (1,)), ((), ()))
    for g in range(N_KV_HEADS):
        cols = slice(g * HEAD_DIM, (g + 1) * HEAD_DIM)
        heads = range(g * Q_GROUP, (g + 1) * Q_GROUP)
        qg = jnp.concatenate([q_ref[:, h * HEAD_DIM:(h + 1) * HEAD_DIM] for h in heads], axis=0)
        sink = jnp.concatenate([jnp.full((nq, 1), sink_ref[l, h], F32) for h in heads], axis=0)
        s_c = lax.dot_general(qg, ck_ref[:, cols].astype(BF16), dn, preferred_element_type=F32)
        s_n = lax.dot_general(qg, kn_ref[:, cols].astype(BF16), dn, preferred_element_type=F32)
        s_c = jnp.where(mask_c, s_c, NEG_INF)
        s_n = jnp.where(mask_n, s_n, NEG_INF)
        m = jnp.maximum(jnp.maximum(jnp.max(s_c, axis=-1, keepdims=True),
                                    jnp.max(s_n, axis=-1, keepdims=True)), sink)
        p_c = jnp.exp(s_c - m)
        p_n = jnp.exp(s_n - m)
        denom = (jnp.sum(p_c, axis=-1, keepdims=True) + jnp.sum(p_n, axis=-1, keepdims=True)
                 + jnp.exp(sink - m))
        o = (_dot((p_c / denom).astype(BF16), cv_ref[:, cols].astype(BF16))
             + _dot((p_n / denom).astype(BF16), vn_ref[:, cols].astype(BF16)))
        for j, h in enumerate(heads):
            o_ref[:, h * HEAD_DIM:(h + 1) * HEAD_DIM] = o[j * nq:(j + 1) * nq].astype(BF16)
    for b in range(nseq):
        nk_ref[b * WINDOW:(b + 1) * WINDOW - seq, :] = ck_ref[b * WINDOW + seq:(b + 1) * WINDOW, :]
        nv_ref[b * WINDOW:(b + 1) * WINDOW - seq, :] = cv_ref[b * WINDOW + seq:(b + 1) * WINDOW, :]
        nk_ref[(b + 1) * WINDOW - seq:(b + 1) * WINDOW, :] = kn_ref[b * seq:(b + 1) * seq, :]
        nv_ref[(b + 1) * WINDOW - seq:(b + 1) * WINDOW, :] = vn_ref[b * seq:(b + 1) * seq, :]


def _attn_sample_call(l, sinks, q, k, v, ck, cv, seq, seq_block):
    T = q.shape[0]
    nseq = T // seq
    row = lambda i: (i, 0)
    cache = lambda i: (l, i, 0)
    return pl.pallas_call(
        functools.partial(_attn_sample_kernel, l=l, seq=seq),
        grid=(nseq // seq_block,),
        in_specs=[
            pl.BlockSpec(memory_space=pltpu.SMEM),
            pl.BlockSpec((seq_block * seq, ATTN_W), row),
            pl.BlockSpec((seq_block * seq, KV_W), row),
            pl.BlockSpec((seq_block * seq, KV_W), row),
            pl.BlockSpec((None, seq_block * WINDOW, KV_W), cache),
            pl.BlockSpec((None, seq_block * WINDOW, KV_W), cache),
        ],
        out_specs=[
            pl.BlockSpec((seq_block * seq, ATTN_W), row),
            pl.BlockSpec((seq_block * WINDOW, KV_W), row),
            pl.BlockSpec((seq_block * WINDOW, KV_W), row),
        ],
        out_shape=[
            jax.ShapeDtypeStruct((T, ATTN_W), BF16),
            jax.ShapeDtypeStruct((nseq * WINDOW, KV_W), F32),
            jax.ShapeDtypeStruct((nseq * WINDOW, KV_W), F32),
        ],
        compiler_params=_params("parallel"),
        name="attn_sample",
    )(sinks, q, k, v, ck, cv)


def _merge_kernel(xn_ref, a_ref, m_ref, wga_ref, wgm_ref, wau_ref, wsu_ref, wout_ref, x_ref,
                  h_ref, merged_scr):
    nj, _, tn = merged_scr.shape
    s = pl.program_id(1)

    @pl.when(s < nj)
    def _():
        xn = xn_ref[...]
        a = a_ref[...]
        m = m_ref[...]
        half = tn // 2
        for cols in (slice(0, half), slice(half, tn)):
            ga = jax.nn.sigmoid(_dot(xn, wga_ref[:, cols]))
            gm = jax.nn.sigmoid(_dot(xn, wgm_ref[:, cols]))
            merged = ga * _dot(a, wau_ref[:, cols]) + gm * _dot(m, wsu_ref[:, cols])
            merged_scr[s, :, cols] = merged.astype(BF16)

    @pl.when(s >= nj)
    def _():
        acc = x_ref[...]
        for c in range(nj):
            acc = acc + _dot(merged_scr[c], wout_ref[c * tn:(c + 1) * tn, :])
        h_ref[...] = acc


def _merge_call(l, x, xn, a, m, w_gate, w_au, w_su, w_out, tm, tn):
    T = x.shape[0]
    nj = D_MODEL // tn
    row = lambda i, s: (i, 0)
    gate_col = lambda s: jnp.minimum(s, nj - 1)
    out_col = lambda s: jnp.maximum(s - nj, 0)
    return pl.pallas_call(
        _merge_kernel,
        grid=(T // tm, 2 * nj),
        in_specs=[
            pl.BlockSpec((tm, D_MODEL), row),
            pl.BlockSpec((tm, ATTN_W), row),
            pl.BlockSpec((tm, SGU_W), row),
            pl.BlockSpec((None, D_MODEL, tn), lambda i, s: (l, 0, gate_col(s))),
            pl.BlockSpec((None, D_MODEL, tn), lambda i, s: (l, 0, nj + gate_col(s))),
            pl.BlockSpec((None, ATTN_W, tn), lambda i, s: (l, 0, gate_col(s))),
            pl.BlockSpec((None, SGU_W, tn), lambda i, s: (l, 0, gate_col(s))),
            pl.BlockSpec((None, D_MODEL, tn), lambda i, s: (l, 0, out_col(s))),
            pl.BlockSpec((tm, tn), lambda i, s: (i, out_col(s))),
        ],
        out_specs=pl.BlockSpec((tm, tn), lambda i, s: (i, out_col(s))),
        out_shape=jax.ShapeDtypeStruct((T, D_MODEL), F32),
        scratch_shapes=[pltpu.VMEM((nj, tm, tn), BF16)],
        compiler_params=_params("parallel", "arbitrary"),
        name="merge",
    )(xn, a, m, w_gate, w_gate, w_au, w_su, w_out, x)


def _ffn_kernel(h_ref, hs_ref, g2_ref, w1_ref, w2_ref, y_ref, ys_ref, hn_scr):
    tm = h_ref.shape[0]

    @pl.when(pl.program_id(1) == 0)
    def _():
        h = h_ref[...]
        hn_scr[:tm, :] = _rms_rows(h, g2_ref[...]).astype(BF16)
        y_ref[...] = h
        hs = hs_ref[...]
        hn_scr[tm:, :] = _rms_rows(hs, g2_ref[...]).astype(BF16)
        ys_ref[...] = hs

    half = tm // 2
    w1 = w1_ref[...].astype(BF16)
    w2 = w2_ref[...].astype(BF16)
    acts = []
    for rows in (slice(0, half), slice(half, hn_scr.shape[0])):
        a = jnp.maximum(_dot(hn_scr[rows, :], w1), 0.0)
        acts.append((a * a).astype(BF16))
    y_ref[:half, :] += _dot(acts[0], w2)
    second = _dot(acts[1], w2)
    y_ref[half:, :] += second[:half]
    ys_ref[...] += second[half:]


def _ffn_call(l, h, hs, g2, w1, w2, tm, tf):
    T = h.shape[0]
    n_tiles = T // tm
    ts = hs.shape[0] // n_tiles
    assert hs.shape[0] == ts * n_tiles and ts % 16 == 0
    row = lambda i, f: (i, 0)
    return pl.pallas_call(
        _ffn_kernel,
        grid=(n_tiles, D_FF // tf),
        in_specs=[
            pl.BlockSpec((tm, D_MODEL), row),
            pl.BlockSpec((ts, D_MODEL), row),
            pl.BlockSpec((None, 1, D_MODEL), lambda i, f: (l, 0, 0)),
            pl.BlockSpec((None, D_MODEL, tf), lambda i, f: (l, 0, f)),
            pl.BlockSpec((None, tf, D_MODEL), lambda i, f: (l, f, 0)),
        ],
        out_specs=[pl.BlockSpec((tm, D_MODEL), row), pl.BlockSpec((ts, D_MODEL), row)],
        out_shape=[jax.ShapeDtypeStruct((T, D_MODEL), F32), jax.ShapeDtypeStruct(hs.shape, F32)],
        scratch_shapes=[pltpu.VMEM((tm + ts, D_MODEL), BF16)],
        compiler_params=_params("parallel", "arbitrary"),
        name="ffn",
    )(h, hs, g2, w1, w2)


def _rope_tables(pos):
    half = HEAD_DIM // 2
    inv = jnp.power(jnp.float32(ROPE_THETA), -jnp.arange(half, dtype=jnp.float32) / half)
    ang = pos.astype(jnp.float32)[:, None] * inv[None, :]
    cos = jnp.cos(ang)
    sin = jnp.sin(ang)
    return jnp.tile(cos, (1, 2 * N_KV_HEADS)), jnp.tile(jnp.concatenate([-sin, sin], axis=-1), (1, N_KV_HEADS))


def kernel(x_prompt, x_sample, cache_k, cache_v, norm1_g, w_in, q_norm_g, k_norm_g, attn_sinks,
           sgu_ln_g, sgu_ln_b, sgu_w, sgu_b, w_attn_up, w_sgu_up, w_out, norm2_g, w_ff1, w_ff2):
    batch, seq, _ = x_prompt.shape
    dec_batch, dec_seq, _ = x_sample.shape
    tp = batch * seq
    ts = dec_batch * dec_seq
    tm = 512
    tn = 512
    tq = 512
    assert seq % tm == 0 and tm % SGU_CHUNK == 0 and seq % tq == 0 and tq % WINDOW == 0

    cos_p, sin_p = _rope_tables(jnp.arange(seq))
    cos_s, sin_s = _rope_tables(jnp.tile(PAST_LEN + jnp.arange(dec_seq), dec_batch))

    w_gate = w_in[:, :, GATE_OFF:].astype(BF16)
    w_au = w_attn_up.astype(BF16)
    w_su = w_sgu_up.astype(BF16)
    w_o = w_out.astype(BF16)

    g1 = norm1_g.reshape(DEPTH, 1, D_MODEL)
    g2 = norm2_g.reshape(DEPTH, 1, D_MODEL)
    qg = jnp.tile(q_norm_g, (1, N_KV_HEADS)).reshape(DEPTH, 1, KV_W)
    kg = jnp.tile(k_norm_g, (1, N_KV_HEADS)).reshape(DEPTH, 1, KV_W)
    lng = sgu_ln_g.reshape(DEPTH, 1, SGU_W)
    lnb = sgu_ln_b.reshape(DEPTH, 1, SGU_W)
    bs_t = jnp.swapaxes(sgu_b, 1, 2)
    ws_s = sgu_w[:, :, :dec_seq, :dec_seq].reshape(DEPTH, -1)
    bs_s = sgu_b[:, :, :dec_seq].reshape(DEPTH, -1)
    ck = cache_k.reshape(DEPTH, dec_batch * WINDOW, KV_W)
    cv = cache_v.reshape(DEPTH, dec_batch * WINDOW, KV_W)

    xp = x_prompt.reshape(tp, D_MODEL)
    xs = x_sample.reshape(ts, D_MODEL)
    kp_l, vp_l, ks_l, vs_l, sv_l = [], [], [], [], []
    for l in range(DEPTH):
        xn, q, k, v, xns, qs, ks, vs = _qkv_call(l, xp, xs, g1, w_in, cos_p, sin_p, cos_s, sin_s, qg, kg,
                                                 tm, seq // tm)
        m, ms, vsn = _sgu_call(l, xn, xns, w_in, lng, lnb, sgu_w, bs_t, ws_s, bs_s, tm, dec_seq)
        a = _attn_prompt_call(l, attn_sinks, q, k, v, batch, seq, tq)
        hp = _merge_call(l, xp, xn, a, m, w_gate, w_au, w_su, w_o, 2 * tm, tn)
        a, nk, nv = _attn_sample_call(l, attn_sinks, qs, ks, vs, ck, cv, dec_seq, 8)
        hs = _merge_call(l, xs, xns, a, ms, w_gate, w_au, w_su, w_o, ts, tn)
        xp, xs = _ffn_call(l, hp, hs, g2, w_ff1, w_ff2, 2 * tm, tn)
        kp_l.append(k.reshape(batch, seq, N_KV_HEADS, HEAD_DIM)[:, -WINDOW:])
        vp_l.append(v.reshape(batch, seq, N_KV_HEADS, HEAD_DIM)[:, -WINDOW:])
        ks_l.append(nk.reshape(dec_batch, WINDOW, N_KV_HEADS, HEAD_DIM))
        vs_l.append(nv.reshape(dec_batch, WINDOW, N_KV_HEADS, HEAD_DIM))
        sv_l.append(vsn.reshape(dec_batch, dec_seq, SGU_GROUPS, SGU_CH))

    return (xp.reshape(batch, seq, D_MODEL), xs.reshape(dec_batch, dec_seq, D_MODEL),
            jnp.stack(kp_l), jnp.stack(vp_l), jnp.stack(ks_l), jnp.stack(vs_l), jnp.stack(sv_l))
```

```python
import functools
import math

import jax
import jax.numpy as jnp
import numpy as np
from jax import lax
from jax.experimental import pallas as pl
from jax.experimental.pallas import tpu as pltpu

D_MODEL = 2048
DEPTH = 4
PAST_LEN = 16384
HEAD_DIM = 64
N_HEADS = 16
N_KV_HEADS = 4
Q_GROUP = 4
ATTN_W = N_HEADS * HEAD_DIM
KV_W = N_KV_HEADS * HEAD_DIM
WINDOW = 128
ROPE_THETA = 10000.0
SGU_CHUNK = 128
SGU_W = D_MODEL // 2
SGU_GROUPS = 8
SGU_CH = SGU_W // SGU_GROUPS
D_FF = 4 * D_MODEL
RMS_EPS = 1e-6
LN_EPS = 1e-5
NEG_INF = -1e30

QKV_W = ATTN_W + 2 * KV_W
UV_OFF = QKV_W
GATE_OFF = QKV_W + 2 * SGU_W

LANES = 128
VMEM_LIMIT = 58 * 1024 * 1024

BF16 = jnp.bfloat16
F32 = jnp.float32


def _dot(a, b):
    return jnp.dot(a, b, preferred_element_type=F32)


def _rms_rows(x, g):
    ms = jnp.mean(x * x, axis=-1, keepdims=True)
    return x * lax.rsqrt(ms + RMS_EPS) * g


def _gelu(x):
    return 0.5 * x * (1.0 + lax.erf(x * np.float32(math.sqrt(0.5))))


def _params(*sem):
    return pltpu.CompilerParams(dimension_semantics=sem, vmem_limit_bytes=VMEM_LIMIT)


def _head_norm_rope(z, gain, cos, sin, gmat, first_half):
    width = z.shape[1]
    sq = z * z
    hi = sq.astype(BF16)
    lo = (sq - hi.astype(F32)).astype(BF16)
    ssum = _dot(jnp.concatenate([hi, lo], axis=1), gmat)
    zn = z * lax.rsqrt(ssum * (1.0 / HEAD_DIM) + RMS_EPS) * gain
    rot = jnp.where(first_half, pltpu.roll(zn, width - HEAD_DIM // 2, 1), pltpu.roll(zn, HEAD_DIM // 2, 1))
    return zn * cos + rot * sin


def _qkv_kernel(x_ref, xs_ref, g1_ref, w_ref, cos_ref, sin_ref, coss_ref, sins_ref, qg_ref, kg_ref,
                wau_ref, wsu_ref,
                xn_ref, q_ref, k_ref, v_ref, xns_ref, qs_ref, ks_ref, vs_ref, wau_bf_ref, wsu_bf_ref):
    wau_bf_ref[...] = wau_ref[...].astype(BF16)
    wsu_bf_ref[...] = wsu_ref[...].astype(BF16)
    tm = x_ref.shape[0]
    rows = tm + xs_ref.shape[0]
    cw = KV_W
    g1 = g1_ref[...]
    xn = jnp.concatenate([_rms_rows(x_ref[...], g1).astype(BF16), _rms_rows(xs_ref[...], g1).astype(BF16)],
                         axis=0)
    xn_ref[...] = xn[:tm]
    xns_ref[...] = xn[tm:]
    lane = lax.broadcasted_iota(jnp.int32, (rows, cw), 1)
    first_half = (lane % HEAD_DIM) < (HEAD_DIM // 2)
    r = (lax.broadcasted_iota(jnp.int32, (2 * cw, cw), 0) % cw) // HEAD_DIM
    c = lax.broadcasted_iota(jnp.int32, (2 * cw, cw), 1) // HEAD_DIM
    gmat = jnp.where(r == c, 1.0, 0.0).astype(BF16)
    cos = jnp.concatenate([cos_ref[...], coss_ref[...]], axis=0)
    sin = jnp.concatenate([sin_ref[...], sins_ref[...]], axis=0)
    scale = HEAD_DIM ** -0.5
    n_q = ATTN_W // cw

    def project(c):
        return _dot(xn, w_ref[:, c * cw:(c + 1) * cw].astype(BF16))

    z = project(0)
    for c in range(n_q + 1):
        z_next = project(c + 1)
        if c < n_q:
            q = (_head_norm_rope(z, qg_ref[...], cos, sin, gmat, first_half) * scale).astype(BF16)
            q_ref[:, c * cw:(c + 1) * cw] = q[:tm]
            qs_ref[:, c * cw:(c + 1) * cw] = q[tm:]
        else:
            k = _head_norm_rope(z, kg_ref[...], cos, sin, gmat, first_half)
            k_ref[...] = k[:tm]
            ks_ref[...] = k[tm:]
        z = z_next
    v_ref[...] = z[:tm]
    vs_ref[...] = z[tm:]


def _qkv_call(l, x, xs, g1, w_in, cos_t, sin_t, cos_s, sin_s, qg, kg, w_attn_up, w_sgu_up, tm, n_pos_blocks):
    T = x.shape[0]
    n_tiles = T // tm
    Ts = xs.shape[0]
    ts = Ts // n_tiles
    assert Ts == ts * n_tiles and ts % 16 == 0
    assert ATTN_W % (16 * n_tiles) == 0 and SGU_W % (16 * n_tiles) == 0
    row = lambda i: (i, 0)
    layer = lambda i: (l, 0, 0)
    pos = lambda i: (i % n_pos_blocks, 0)
    return pl.pallas_call(
        _qkv_kernel,
        grid=(n_tiles,),
        in_specs=[
            pl.BlockSpec((tm, D_MODEL), row),
            pl.BlockSpec((ts, D_MODEL), row),
            pl.BlockSpec((None, 1, D_MODEL), layer),
            pl.BlockSpec((None, D_MODEL, QKV_W), layer),
            pl.BlockSpec((tm, KV_W), pos),
            pl.BlockSpec((tm, KV_W), pos),
            pl.BlockSpec((ts, KV_W), row),
            pl.BlockSpec((ts, KV_W), row),
            pl.BlockSpec((None, 1, KV_W), layer),
            pl.BlockSpec((None, 1, KV_W), layer),
            pl.BlockSpec((None, ATTN_W // n_tiles, D_MODEL), lambda i: (l, i, 0)),
            pl.BlockSpec((None, SGU_W // n_tiles, D_MODEL), lambda i: (l, i, 0)),
        ],
        out_specs=[
            pl.BlockSpec((tm, D_MODEL), row),
            pl.BlockSpec((tm, ATTN_W), row),
            pl.BlockSpec((tm, KV_W), row),
            pl.BlockSpec((tm, KV_W), row),
            pl.BlockSpec((ts, D_MODEL), row),
            pl.BlockSpec((ts, ATTN_W), row),
            pl.BlockSpec((ts, KV_W), row),
            pl.BlockSpec((ts, KV_W), row),
            pl.BlockSpec((ATTN_W // n_tiles, D_MODEL), row),
            pl.BlockSpec((SGU_W // n_tiles, D_MODEL), row),
        ],
        out_shape=[
            jax.ShapeDtypeStruct((T, D_MODEL), BF16),
            jax.ShapeDtypeStruct((T, ATTN_W), BF16),
            jax.ShapeDtypeStruct((T, KV_W), F32),
            jax.ShapeDtypeStruct((T, KV_W), F32),
            jax.ShapeDtypeStruct((Ts, D_MODEL), BF16),
            jax.ShapeDtypeStruct((Ts, ATTN_W), BF16),
            jax.ShapeDtypeStruct((Ts, KV_W), F32),
            jax.ShapeDtypeStruct((Ts, KV_W), F32),
            jax.ShapeDtypeStruct((ATTN_W, D_MODEL), BF16),
            jax.ShapeDtypeStruct((SGU_W, D_MODEL), BF16),
        ],
        compiler_params=_params("parallel"),
        name="qkv",
    )(x, xs, g1, w_in, cos_t, sin_t, cos_s, sin_s, qg, kg, w_attn_up, w_sgu_up)


UV_BLOCK = 512
UV_BLOCKS = 2 * SGU_W // UV_BLOCK


def _uv_project(xn, w_refs, lng_ref, lnb_ref, u_scr, vs_scr):
    cw = 2 * LANES
    per_block = UV_BLOCK // cw
    n_u = SGU_W // cw
    for c in list(range(n_u, 2 * n_u)) + list(range(n_u)):
        w = w_refs[c // per_block][:, (c % per_block) * cw:(c % per_block + 1) * cw].astype(BF16)
        dst = u_scr if c < n_u else vs_scr
        c_dst = c % n_u
        dst[:, c_dst * cw:(c_dst + 1) * cw] = _gelu(_dot(xn, w))
    vs = vs_scr[...]
    mu = jnp.mean(vs, axis=-1, keepdims=True)
    d = vs - mu
    var = jnp.mean(d * d, axis=-1, keepdims=True)
    vs_scr[...] = d * lax.rsqrt(var + LN_EPS) * lng_ref[...] + lnb_ref[...]


def _sgu_kernel(wss_ref, bss_ref, xn_ref, xns_ref, *refs, l, seq):
    w_refs = refs[:UV_BLOCKS]
    (lng_ref, lnb_ref, ws_ref, bs_ref, wout_ref,
     m_ref, ms_ref, vsn_ref, wout_bf_ref, u_scr, vs_scr, g_scr) = refs[UV_BLOCKS:]
    wout_bf_ref[...] = wout_ref[...].astype(BF16)
    tm = xn_ref.shape[0]
    nseq = xns_ref.shape[0] // seq
    xn = jnp.concatenate([xn_ref[...], xns_ref[...]], axis=0)
    _uv_project(xn, w_refs, lng_ref, lnb_ref, u_scr, vs_scr)

    t = lax.broadcasted_iota(jnp.int32, (SGU_CHUNK, SGU_CHUNK), 0)
    s = lax.broadcasted_iota(jnp.int32, (SGU_CHUNK, SGU_CHUNK), 1)
    causal = t >= s
    for g in range(SGU_GROUPS):
        wg = jnp.where(causal, ws_ref[g], 0.0).astype(BF16)
        bias = bs_ref[:, g:g + 1]
        cols = slice(g * SGU_CH, (g + 1) * SGU_CH)
        for n in range(tm // SGU_CHUNK):
            rows = slice(n * SGU_CHUNK, (n + 1) * SGU_CHUNK)
            mixed = _dot(wg, vs_scr[rows, cols].astype(BF16)) + bias
            m_ref[rows, cols] = (u_scr[rows, cols] * mixed).astype(BF16)

    vsn_ref[...] = vs_scr[tm:, :]
    for g in range(SGU_GROUPS):
        cols = slice(g * SGU_CH, (g + 1) * SGU_CH)
        g_scr[0] = vs_scr[tm:, cols]
        g_scr[1] = u_scr[tm:, cols]
        for t in range(seq):
            step_t = pl.ds(t, nseq, stride=seq)
            acc = jnp.full((nseq, SGU_CH), bss_ref[l, g * seq + t], F32)
            for s in range(t + 1):
                acc = acc + wss_ref[l, (g * seq + t) * seq + s] * g_scr[0, pl.ds(s, nseq, stride=seq), :]
            g_scr[2, step_t, :] = g_scr[1, step_t, :] * acc
        ms_ref[:, cols] = g_scr[2].astype(BF16)


def _uv_weight_specs(l):
    return [pl.BlockSpec((None, D_MODEL, UV_BLOCK), functools.partial(lambda b, i: (l, 0, b), UV_OFF // UV_BLOCK + k))
            for k in range(UV_BLOCKS)]


def _sgu_call(l, xn, xns, w_in, lng, lnb, ws, bs_t, ws_sample, bs_sample, w_out, tm, seq):
    T = xn.shape[0]
    n_tiles = T // tm
    Ts = xns.shape[0]
    ts = Ts // n_tiles
    assert Ts == ts * n_tiles and ts % 16 == 0 and ts % seq == 0 and D_MODEL % (16 * n_tiles) == 0
    slab = D_MODEL // n_tiles
    row = lambda i: (i, 0)
    layer = lambda i: (l, 0, 0)
    smem = pl.BlockSpec(memory_space=pltpu.SMEM)
    return pl.pallas_call(
        functools.partial(_sgu_kernel, l=l, seq=seq),
        grid=(n_tiles,),
        in_specs=[
            smem,
            smem,
            pl.BlockSpec((tm, D_MODEL), row),
            pl.BlockSpec((ts, D_MODEL), row),
            *_uv_weight_specs(l),
            pl.BlockSpec((None, 1, SGU_W), layer),
            pl.BlockSpec((None, 1, SGU_W), layer),
            pl.BlockSpec((None, SGU_GROUPS, SGU_CHUNK, SGU_CHUNK), lambda i: (l, 0, 0, 0)),
            pl.BlockSpec((None, SGU_CHUNK, SGU_GROUPS), layer),
            pl.BlockSpec((None, slab, D_MODEL), lambda i: (l, i, 0)),
        ],
        out_specs=[pl.BlockSpec((tm, SGU_W), row), pl.BlockSpec((ts, SGU_W), row),
                   pl.BlockSpec((ts, SGU_W), row), pl.BlockSpec((slab, D_MODEL), row)],
        out_shape=[jax.ShapeDtypeStruct((T, SGU_W), BF16), jax.ShapeDtypeStruct((Ts, SGU_W), BF16),
                   jax.ShapeDtypeStruct((Ts, SGU_W), F32), jax.ShapeDtypeStruct((D_MODEL, D_MODEL), BF16)],
        scratch_shapes=[pltpu.VMEM((tm + ts, SGU_W), F32), pltpu.VMEM((tm + ts, SGU_W), F32),
                        pltpu.VMEM((3, ts, SGU_CH), F32)],
        compiler_params=_params("parallel"),
        name="sgu",
    )(ws_sample, bs_sample, xn, xns, *([w_in] * UV_BLOCKS), lng, lnb, ws, bs_t, w_out)


def _attn_prompt_kernel(sink_ref, q_ref, kp_ref, kc_ref, vp_ref, vc_ref, wg_ref, o_ref, wg_bf_ref, *, l):
    wg_bf_ref[...] = wg_ref[...].astype(BF16)
    blk = WINDOW
    nq = Q_GROUP * blk
    first_tile = pl.program_id(1) == 0
    c = lax.broadcasted_iota(jnp.int32, (blk, nq), 0)
    r = lax.broadcasted_iota(jnp.int32, (blk, nq), 1) % blk
    from_prev = c > r
    k_all = jnp.concatenate([kp_ref[...], kc_ref[...]], axis=0).astype(BF16)
    vt_all = jnp.concatenate([vp_ref[...], vc_ref[...]], axis=0).T.astype(BF16)
    sinks = [jnp.concatenate([jnp.full((1, blk), sink_ref[l, g * Q_GROUP + j], F32) for j in range(Q_GROUP)],
                             axis=1) for g in range(N_KV_HEADS)]

    def scores(jb):
        out = []
        for g in range(N_KV_HEADS):
            qg = jnp.concatenate([q_ref[jb * blk:(jb + 1) * blk, h * HEAD_DIM:(h + 1) * HEAD_DIM]
                                  for h in range(g * Q_GROUP, (g + 1) * Q_GROUP)], axis=0)
            kg = k_all[jb * blk:(jb + 2) * blk, g * HEAD_DIM:(g + 1) * HEAD_DIM]
            out.append(lax.dot_general(kg, qg, (((1,), (1,)), ((), ())), preferred_element_type=F32))
        return out

    nblk = q_ref.shape[0] // blk
    s2 = scores(0)
    for jb in range(nblk):
        s2_next = scores(jb + 1) if jb + 1 < nblk else None
        for g in range(N_KV_HEADS):
            s_prev = s2[g][:blk]
            if jb == 0:
                s_prev = jnp.where(first_tile, NEG_INF, s_prev)
            s = jnp.where(from_prev, s_prev, s2[g][blk:])
            m = jnp.maximum(jnp.max(s, axis=0, keepdims=True), sinks[g])
            p = jnp.exp(s - m)
            p = p / (jnp.sum(p, axis=0, keepdims=True) + jnp.exp(sinks[g] - m))
            p2 = jnp.concatenate([jnp.where(from_prev, p, 0.0), jnp.where(from_prev, 0.0, p)], axis=0)
            vt = vt_all[g * HEAD_DIM:(g + 1) * HEAD_DIM, jb * blk:(jb + 2) * blk]
            o = _dot(vt, p2.astype(BF16)).T
            for j in range(Q_GROUP):
                h = g * Q_GROUP + j
                o_ref[jb * blk:(jb + 1) * blk, h * HEAD_DIM:(h + 1) * HEAD_DIM] = (
                    o[j * blk:(j + 1) * blk].astype(BF16))
        s2 = s2_next


def _attn_prompt_call(l, sinks, q, k, v, w_in, batch, seq, tq):
    nt = seq // tq
    blocks_per_tile = tq // WINDOW
    gate_w = 2 * D_MODEL // (batch * nt)
    assert gate_w % LANES == 0 and GATE_OFF % gate_w == 0
    cur = lambda b, i: (b * nt + i, 0)
    prev = lambda b, i: (b * (seq // WINDOW) + jnp.maximum(i * blocks_per_tile - 1, 0), 0)
    return pl.pallas_call(
        functools.partial(_attn_prompt_kernel, l=l),
        grid=(batch, nt),
        in_specs=[
            pl.BlockSpec(memory_space=pltpu.SMEM),
            pl.BlockSpec((tq, ATTN_W), cur),
            pl.BlockSpec((WINDOW, KV_W), prev),
            pl.BlockSpec((tq, KV_W), cur),
            pl.BlockSpec((WINDOW, KV_W), prev),
            pl.BlockSpec((tq, KV_W), cur),
            pl.BlockSpec((None, D_MODEL, gate_w), lambda b, i: (l, 0, GATE_OFF // gate_w + b * nt + i)),
        ],
        out_specs=[pl.BlockSpec((tq, ATTN_W), cur),
                   pl.BlockSpec((D_MODEL, gate_w), lambda b, i: (0, b * nt + i))],
        out_shape=[jax.ShapeDtypeStruct((batch * seq, ATTN_W), BF16),
                   jax.ShapeDtypeStruct((D_MODEL, 2 * D_MODEL), BF16)],
        compiler_params=_params("parallel", "parallel"),
        name="attn_prompt",
    )(sinks, q, k, k, v, v, w_in)


def _attn_sample_kernel(sink_ref, q_ref, kn_ref, vn_ref, ck_ref, cv_ref, o_ref, nk_ref, nv_ref, *, l, seq):
    nseq = q_ref.shape[0] // seq
    nq = nseq * seq
    rows = Q_GROUP * nq
    ri = lax.broadcasted_iota(jnp.int32, (rows, nseq * WINDOW), 0) % nq
    ci = lax.broadcasted_iota(jnp.int32, (rows, nseq * WINDOW), 1)
    mask_c = (ri // seq == ci // WINDOW) & (ci % WINDOW > ri % seq)
    ri = lax.broadcasted_iota(jnp.int32, (rows, nq), 0) % nq
    ci = lax.broadcasted_iota(jnp.int32, (rows, nq), 1)
    mask_n = (ri // seq == ci // seq) & (ci % seq <= ri % seq)
    dn = (((1,), (1,)), ((), ()))
    for g in range(N_KV_HEADS):
        cols = slice(g * HEAD_DIM, (g + 1) * HEAD_DIM)
        heads = range(g * Q_GROUP, (g + 1) * Q_GROUP)
        qg = jnp.concatenate([q_ref[:, h * HEAD_DIM:(h + 1) * HEAD_DIM] for h in heads], axis=0)
        sink = jnp.concatenate([jnp.full((nq, 1), sink_ref[l, h], F32) for h in heads], axis=0)
        s_c = lax.dot_general(qg, ck_ref[:, cols].astype(BF16), dn, preferred_element_type=F32)
        s_n = lax.dot_general(qg, kn_ref[:, cols].astype(BF16), dn, preferred_element_type=F32)
        s_c = jnp.where(mask_c, s_c, NEG_INF)
        s_n = jnp.where(mask_n, s_n, NEG_INF)
        m = jnp.maximum(jnp.maximum(jnp.max(s_c, axis=-1, keepdims=True),
                                    jnp.max(s_n, axis=-1, keepdims=True)), sink)
        p_c = jnp.exp(s_c - m)
        p_n = jnp.exp(s_n - m)
        denom = (jnp.sum(p_c, axis=-1, keepdims=True) + jnp.sum(p_n, axis=-1, keepdims=True)
                 + jnp.exp(sink - m))
        o = (_dot((p_c / denom).astype(BF16), cv_ref[:, cols].astype(BF16))
             + _dot((p_n / denom).astype(BF16), vn_ref[:, cols].astype(BF16)))
        for j, h in enumerate(heads):
            o_ref[:, h * HEAD_DIM:(h + 1) * HEAD_DIM] = o[j * nq:(j + 1) * nq].astype(BF16)
    for b in range(nseq):
        nk_ref[b * WINDOW:(b + 1) * WINDOW - seq, :] = ck_ref[b * WINDOW + seq:(b + 1) * WINDOW, :]
        nv_ref[b * WINDOW:(b + 1) * WINDOW - seq, :] = cv_ref[b * WINDOW + seq:(b + 1) * WINDOW, :]
        nk_ref[(b + 1) * WINDOW - seq:(b + 1) * WINDOW, :] = kn_ref[b * seq:(b + 1) * seq, :]
        nv_ref[(b + 1) * WINDOW - seq:(b + 1) * WINDOW, :] = vn_ref[b * seq:(b + 1) * seq, :]


def _attn_sample_call(l, sinks, q, k, v, ck, cv, seq, seq_block):
    T = q.shape[0]
    nseq = T // seq
    row = lambda i: (i, 0)
    cache = lambda i: (l, i, 0)
    return pl.pallas_call(
        functools.partial(_attn_sample_kernel, l=l, seq=seq),
        grid=(nseq // seq_block,),
        in_specs=[
            pl.BlockSpec(memory_space=pltpu.SMEM),
            pl.BlockSpec((seq_block * seq, ATTN_W), row),
            pl.BlockSpec((seq_block * seq, KV_W), row),
            pl.BlockSpec((seq_block * seq, KV_W), row),
            pl.BlockSpec((None, seq_block * WINDOW, KV_W), cache),
            pl.BlockSpec((None, seq_block * WINDOW, KV_W), cache),
        ],
        out_specs=[
            pl.BlockSpec((seq_block * seq, ATTN_W), row),
            pl.BlockSpec((seq_block * WINDOW, KV_W), row),
            pl.BlockSpec((seq_block * WINDOW, KV_W), row),
        ],
        out_shape=[
            jax.ShapeDtypeStruct((T, ATTN_W), BF16),
            jax.ShapeDtypeStruct((nseq * WINDOW, KV_W), F32),
            jax.ShapeDtypeStruct((nseq * WINDOW, KV_W), F32),
        ],
        compiler_params=_params("parallel"),
        name="attn_sample",
    )(sinks, q, k, v, ck, cv)


def _merge_kernel(x_ref, xn_ref, a_ref, m_ref, wga_ref, wgm_ref, wau_ref, wsu_ref, wout_ref,
                  h_ref, merged_scr):
    j = pl.program_id(1)
    xn = xn_ref[...]
    ga = jax.nn.sigmoid(_dot(xn, wga_ref[...]))
    gm = jax.nn.sigmoid(_dot(xn, wgm_ref[...]))
    merged = ga * _dot(a_ref[...], wau_ref[...]) + gm * _dot(m_ref[...], wsu_ref[...])
    merged_scr[j] = merged.astype(BF16)

    @pl.when(j == pl.num_programs(1) - 1)
    def _():
        tn = merged_scr.shape[2]
        acc = x_ref[...]
        for c in range(merged_scr.shape[0]):
            acc = acc + _dot(merged_scr[c], wout_ref[c * tn:(c + 1) * tn, :])
        h_ref[...] = acc


def _merge_call(x, xn, a, m, w_gate, w_au, w_su, w_out, tm, tn):
    T = x.shape[0]
    nj = D_MODEL // tn
    row = lambda i, j: (i, 0)
    col = lambda i, j: (0, j)
    return pl.pallas_call(
        _merge_kernel,
        grid=(T // tm, nj),
        in_specs=[
            pl.BlockSpec((tm, D_MODEL), row),
            pl.BlockSpec((tm, D_MODEL), row),
            pl.BlockSpec((tm, ATTN_W), row),
            pl.BlockSpec((tm, SGU_W), row),
            pl.BlockSpec((D_MODEL, tn), col),
            pl.BlockSpec((D_MODEL, tn), lambda i, j: (0, nj + j)),
            pl.BlockSpec((ATTN_W, tn), col),
            pl.BlockSpec((SGU_W, tn), col),
            pl.BlockSpec((D_MODEL, D_MODEL), lambda i, j: (0, 0)),
        ],
        out_specs=pl.BlockSpec((tm, D_MODEL), row),
        out_shape=jax.ShapeDtypeStruct((T, D_MODEL), F32),
        scratch_shapes=[pltpu.VMEM((nj, tm, tn), BF16)],
        compiler_params=_params("parallel", "arbitrary"),
        name="merge",
    )(x, xn, a, m, w_gate, w_gate, w_au, w_su, w_out)


def _ffn_kernel(h_ref, hs_ref, g2_ref, w1_ref, w2_ref, y_ref, ys_ref, hn_scr):
    tm = h_ref.shape[0]

    @pl.when(pl.program_id(1) == 0)
    def _():
        h = h_ref[...]
        hn_scr[:tm, :] = _rms_rows(h, g2_ref[...]).astype(BF16)
        y_ref[...] = h
        hs = hs_ref[...]
        hn_scr[tm:, :] = _rms_rows(hs, g2_ref[...]).astype(BF16)
        ys_ref[...] = hs

    half = tm // 2
    w1 = w1_ref[...].astype(BF16)
    w2 = w2_ref[...].astype(BF16)
    acts = []
    for rows in (slice(0, half), slice(half, hn_scr.shape[0])):
        a = jnp.maximum(_dot(hn_scr[rows, :], w1), 0.0)
        acts.append((a * a).astype(BF16))
    y_ref[:half, :] += _dot(acts[0], w2)
    second = _dot(acts[1], w2)
    y_ref[half:, :] += second[:half]
    ys_ref[...] += second[half:]


def _ffn_call(l, h, hs, g2, w1, w2, tm, tf):
    T = h.shape[0]
    n_tiles = T // tm
    ts = hs.shape[0] // n_tiles
    assert hs.shape[0] == ts * n_tiles and ts % 16 == 0
    row = lambda i, f: (i, 0)
    return pl.pallas_call(
        _ffn_kernel,
        grid=(n_tiles, D_FF // tf),
        in_specs=[
            pl.BlockSpec((tm, D_MODEL), row),
            pl.BlockSpec((ts, D_MODEL), row),
            pl.BlockSpec((None, 1, D_MODEL), lambda i, f: (l, 0, 0)),
            pl.BlockSpec((None, D_MODEL, tf), lambda i, f: (l, 0, f)),
            pl.BlockSpec((None, tf, D_MODEL), lambda i, f: (l, f, 0)),
        ],
        out_specs=[pl.BlockSpec((tm, D_MODEL), row), pl.BlockSpec((ts, D_MODEL), row)],
        out_shape=[jax.ShapeDtypeStruct((T, D_MODEL), F32), jax.ShapeDtypeStruct(hs.shape, F32)],
        scratch_shapes=[pltpu.VMEM((tm + ts, D_MODEL), BF16)],
        compiler_params=_params("parallel", "arbitrary"),
        name="ffn",
    )(h, hs, g2, w1, w2)


def _rope_tables(pos):
    half = HEAD_DIM // 2
    inv = jnp.power(jnp.float32(ROPE_THETA), -jnp.arange(half, dtype=jnp.float32) / half)
    ang = pos.astype(jnp.float32)[:, None] * inv[None, :]
    cos = jnp.cos(ang)
    sin = jnp.sin(ang)
    return jnp.tile(cos, (1, 2 * N_KV_HEADS)), jnp.tile(jnp.concatenate([-sin, sin], axis=-1), (1, N_KV_HEADS))


def kernel(x_prompt, x_sample, cache_k, cache_v, norm1_g, w_in, q_norm_g, k_norm_g, attn_sinks,
           sgu_ln_g, sgu_ln_b, sgu_w, sgu_b, w_attn_up, w_sgu_up, w_out, norm2_g, w_ff1, w_ff2):
    batch, seq, _ = x_prompt.shape
    dec_batch, dec_seq, _ = x_sample.shape
    tp = batch * seq
    ts = dec_batch * dec_seq
    tm = 512
    tn = 512
    tq = 512
    assert seq % tm == 0 and tm % SGU_CHUNK == 0 and seq % tq == 0 and tq % WINDOW == 0

    cos_p, sin_p = _rope_tables(jnp.arange(seq))
    cos_s, sin_s = _rope_tables(jnp.tile(PAST_LEN + jnp.arange(dec_seq), dec_batch))

    g1 = norm1_g.reshape(DEPTH, 1, D_MODEL)
    g2 = norm2_g.reshape(DEPTH, 1, D_MODEL)
    qg = jnp.tile(q_norm_g, (1, N_KV_HEADS)).reshape(DEPTH, 1, KV_W)
    kg = jnp.tile(k_norm_g, (1, N_KV_HEADS)).reshape(DEPTH, 1, KV_W)
    lng = sgu_ln_g.reshape(DEPTH, 1, SGU_W)
    lnb = sgu_ln_b.reshape(DEPTH, 1, SGU_W)
    bs_t = jnp.swapaxes(sgu_b, 1, 2)
    ws_s = sgu_w[:, :, :dec_seq, :dec_seq].reshape(DEPTH, -1)
    bs_s = sgu_b[:, :, :dec_seq].reshape(DEPTH, -1)
    ck = cache_k.reshape(DEPTH, dec_batch * WINDOW, KV_W)
    cv = cache_v.reshape(DEPTH, dec_batch * WINDOW, KV_W)

    xp = x_prompt.reshape(tp, D_MODEL)
    xs = x_sample.reshape(ts, D_MODEL)
    kp_l, vp_l, ks_l, vs_l, sv_l = [], [], [], [], []
    for l in range(DEPTH):
        xn, q, k, v, xns, qs, ks, vs, w_au, w_su = _qkv_call(
            l, xp, xs, g1, w_in, cos_p, sin_p, cos_s, sin_s, qg, kg, w_attn_up, w_sgu_up, tm, seq // tm)
        m, ms, vsn, w_o = _sgu_call(l, xn, xns, w_in, lng, lnb, sgu_w, bs_t, ws_s, bs_s, w_out, tm, dec_seq)
        a, w_gate = _attn_prompt_call(l, attn_sinks, q, k, v, w_in, batch, seq, tq)
        hp = _merge_call(xp, xn, a, m, w_gate, w_au, w_su, w_o, tm, tn)
        a, nk, nv = _attn_sample_call(l, attn_sinks, qs, ks, vs, ck, cv, dec_seq, 8)
        hs = _merge_call(xs, xns, a, ms, w_gate, w_au, w_su, w_o, ts, tn)
        xp, xs = _ffn_call(l, hp, hs, g2, w_ff1, w_ff2, 2 * tm, tn)
        kp_l.append(k.reshape(batch, seq, KV_W)[:, -WINDOW:].reshape(batch, WINDOW, N_KV_HEADS, HEAD_DIM))
        vp_l.append(v.reshape(batch, seq, KV_W)[:, -WINDOW:].reshape(batch, WINDOW, N_KV_HEADS, HEAD_DIM))
        ks_l.append(nk.reshape(dec_batch, WINDOW, N_KV_HEADS, HEAD_DIM))
        vs_l.append(nv.reshape(dec_batch, WINDOW, N_KV_HEADS, HEAD_DIM))
        sv_l.append(vsn.reshape(dec_batch, dec_seq, SGU_GROUPS, SGU_CH))

    return (xp.reshape(batch, seq, D_MODEL), xs.reshape(dec_batch, dec_seq, D_MODEL),
            jnp.stack(kp_l), jnp.stack(vp_l), jnp.stack(ks_l), jnp.stack(vs_l), jnp.stack(sv_l))
```

```python
import functools
import math

import jax
import jax.numpy as jnp
import numpy as np
from jax import lax
from jax.experimental import pallas as pl
from jax.experimental.pallas import tpu as pltpu

D_MODEL = 2048
DEPTH = 4
PAST_LEN = 16384
HEAD_DIM = 64
N_HEADS = 16
N_KV_HEADS = 4
Q_GROUP = 4
ATTN_W = N_HEADS * HEAD_DIM
KV_W = N_KV_HEADS * HEAD_DIM
WINDOW = 128
ROPE_THETA = 10000.0
SGU_CHUNK = 128
SGU_W = D_MODEL // 2
SGU_GROUPS = 8
SGU_CH = SGU_W // SGU_GROUPS
D_FF = 4 * D_MODEL
RMS_EPS = 1e-6
LN_EPS = 1e-5
NEG_INF = -1e30

QKV_W = ATTN_W + 2 * KV_W
UV_OFF = QKV_W
GATE_OFF = QKV_W + 2 * SGU_W

LANES = 128
VMEM_LIMIT = 58 * 1024 * 1024

BF16 = jnp.bfloat16
F32 = jnp.float32


def _dot(a, b):
    return jnp.dot(a, b, preferred_element_type=F32)


def _rms_rows(x, g):
    ms = jnp.mean(x * x, axis=-1, keepdims=True)
    return x * lax.rsqrt(ms + RMS_EPS) * g


def _gelu(x):
    return 0.5 * x * (1.0 + lax.erf(x * np.float32(math.sqrt(0.5))))


def _params(*sem):
    return pltpu.CompilerParams(dimension_semantics=sem, vmem_limit_bytes=VMEM_LIMIT)


def _head_norm_rope(z, gain, cos, sin, gmat, first_half):
    width = z.shape[1]
    sq = z * z
    hi = sq.astype(BF16)
    lo = (sq - hi.astype(F32)).astype(BF16)
    ssum = _dot(jnp.concatenate([hi, lo], axis=1), gmat)
    zn = z * lax.rsqrt(ssum * (1.0 / HEAD_DIM) + RMS_EPS) * gain
    rot = jnp.where(first_half, pltpu.roll(zn, width - HEAD_DIM // 2, 1), pltpu.roll(zn, HEAD_DIM // 2, 1))
    return zn * cos + rot * sin


def _qkv_kernel(x_ref, xs_ref, g1_ref, w_ref, cos_ref, sin_ref, coss_ref, sins_ref, qg_ref, kg_ref,
                wau_ref, wsu_ref,
                xn_ref, q_ref, k_ref, v_ref, xns_ref, qs_ref, ks_ref, vs_ref, wau_bf_ref, wsu_bf_ref):
    wau_bf_ref[...] = wau_ref[...].astype(BF16)
    wsu_bf_ref[...] = wsu_ref[...].astype(BF16)
    tm = x_ref.shape[0]
    rows = tm + xs_ref.shape[0]
    cw = KV_W
    g1 = g1_ref[...]
    xn = jnp.concatenate([_rms_rows(x_ref[...], g1).astype(BF16), _rms_rows(xs_ref[...], g1).astype(BF16)],
                         axis=0)
    xn_ref[...] = xn[:tm]
    xns_ref[...] = xn[tm:]
    lane = lax.broadcasted_iota(jnp.int32, (rows, cw), 1)
    first_half = (lane % HEAD_DIM) < (HEAD_DIM // 2)
    r = (lax.broadcasted_iota(jnp.int32, (2 * cw, cw), 0) % cw) // HEAD_DIM
    c = lax.broadcasted_iota(jnp.int32, (2 * cw, cw), 1) // HEAD_DIM
    gmat = jnp.where(r == c, 1.0, 0.0).astype(BF16)
    cos = jnp.concatenate([cos_ref[...], coss_ref[...]], axis=0)
    sin = jnp.concatenate([sin_ref[...], sins_ref[...]], axis=0)
    scale = HEAD_DIM ** -0.5
    n_q = ATTN_W // cw

    def project(c):
        return _dot(xn, w_ref[:, c * cw:(c + 1) * cw].astype(BF16))

    z = project(0)
    for c in range(n_q + 1):
        z_next = project(c + 1)
        if c < n_q:
            q = (_head_norm_rope(z, qg_ref[...], cos, sin, gmat, first_half) * scale).astype(BF16)
            q_ref[:, c * cw:(c + 1) * cw] = q[:tm]
            qs_ref[:, c * cw:(c + 1) * cw] = q[tm:]
        else:
            k = _head_norm_rope(z, kg_ref[...], cos, sin, gmat, first_half)
            k_ref[...] = k[:tm]
            ks_ref[...] = k[tm:]
        z = z_next
    v_ref[...] = z[:tm]
    vs_ref[...] = z[tm:]


def _qkv_call(l, x, xs, g1, w_in, cos_t, sin_t, cos_s, sin_s, qg, kg, w_attn_up, w_sgu_up, tm, n_pos_blocks):
    T = x.shape[0]
    n_tiles = T // tm
    Ts = xs.shape[0]
    ts = Ts // n_tiles
    assert Ts == ts * n_tiles and ts % 16 == 0
    assert ATTN_W % (16 * n_tiles) == 0 and SGU_W % (16 * n_tiles) == 0
    row = lambda i: (i, 0)
    layer = lambda i: (l, 0, 0)
    pos = lambda i: (i % n_pos_blocks, 0)
    return pl.pallas_call(
        _qkv_kernel,
        grid=(n_tiles,),
        in_specs=[
            pl.BlockSpec((tm, D_MODEL), row),
            pl.BlockSpec((ts, D_MODEL), row),
            pl.BlockSpec((None, 1, D_MODEL), layer),
            pl.BlockSpec((None, D_MODEL, QKV_W), layer),
            pl.BlockSpec((tm, KV_W), pos),
            pl.BlockSpec((tm, KV_W), pos),
            pl.BlockSpec((ts, KV_W), row),
            pl.BlockSpec((ts, KV_W), row),
            pl.BlockSpec((None, 1, KV_W), layer),
            pl.BlockSpec((None, 1, KV_W), layer),
            pl.BlockSpec((None, ATTN_W // n_tiles, D_MODEL), lambda i: (l, i, 0)),
            pl.BlockSpec((None, SGU_W // n_tiles, D_MODEL), lambda i: (l, i, 0)),
        ],
        out_specs=[
            pl.BlockSpec((tm, D_MODEL), row),
            pl.BlockSpec((tm, ATTN_W), row),
            pl.BlockSpec((tm, KV_W), row),
            pl.BlockSpec((tm, KV_W), row),
            pl.BlockSpec((ts, D_MODEL), row),
            pl.BlockSpec((ts, ATTN_W), row),
            pl.BlockSpec((ts, KV_W), row),
            pl.BlockSpec((ts, KV_W), row),
            pl.BlockSpec((ATTN_W // n_tiles, D_MODEL), row),
            pl.BlockSpec((SGU_W // n_tiles, D_MODEL), row),
        ],
        out_shape=[
            jax.ShapeDtypeStruct((T, D_MODEL), BF16),
            jax.ShapeDtypeStruct((T, ATTN_W), BF16),
            jax.ShapeDtypeStruct((T, KV_W), F32),
            jax.ShapeDtypeStruct((T, KV_W), F32),
            jax.ShapeDtypeStruct((Ts, D_MODEL), BF16),
            jax.ShapeDtypeStruct((Ts, ATTN_W), BF16),
            jax.ShapeDtypeStruct((Ts, KV_W), F32),
            jax.ShapeDtypeStruct((Ts, KV_W), F32),
            jax.ShapeDtypeStruct((ATTN_W, D_MODEL), BF16),
            jax.ShapeDtypeStruct((SGU_W, D_MODEL), BF16),
        ],
        compiler_params=_params("parallel"),
        name="qkv",
    )(x, xs, g1, w_in, cos_t, sin_t, cos_s, sin_s, qg, kg, w_attn_up, w_sgu_up)


UV_BLOCK = 512
UV_BLOCKS = 2 * SGU_W // UV_BLOCK


def _uv_project(xn, w_refs, lng_ref, lnb_ref, u_scr, vs_scr):
    cw = 2 * LANES
    per_block = UV_BLOCK // cw
    n_u = SGU_W // cw
    for c in list(range(n_u, 2 * n_u)) + list(range(n_u)):
        w = w_refs[c // per_block][:, (c % per_block) * cw:(c % per_block + 1) * cw].astype(BF16)
        dst = u_scr if c < n_u else vs_scr
        c_dst = c % n_u
        dst[:, c_dst * cw:(c_dst + 1) * cw] = _gelu(_dot(xn, w))
    vs = vs_scr[...]
    mu = jnp.mean(vs, axis=-1, keepdims=True)
    d = vs - mu
    var = jnp.mean(d * d, axis=-1, keepdims=True)
    vs_scr[...] = d * lax.rsqrt(var + LN_EPS) * lng_ref[...] + lnb_ref[...]


def _sgu_kernel(wss_ref, bss_ref, xn_ref, xns_ref, *refs, l, seq):
    w_refs = refs[:UV_BLOCKS]
    (lng_ref, lnb_ref, ws_ref, bs_ref, wout_ref,
     m_ref, ms_ref, vsn_ref, wout_bf_ref, u_scr, vs_scr, g_scr) = refs[UV_BLOCKS:]
    wout_bf_ref[...] = wout_ref[...].astype(BF16)
    tm = xn_ref.shape[0]
    nseq = xns_ref.shape[0] // seq
    xn = jnp.concatenate([xn_ref[...], xns_ref[...]], axis=0)
    _uv_project(xn, w_refs, lng_ref, lnb_ref, u_scr, vs_scr)

    t = lax.broadcasted_iota(jnp.int32, (SGU_CHUNK, SGU_CHUNK), 0)
    s = lax.broadcasted_iota(jnp.int32, (SGU_CHUNK, SGU_CHUNK), 1)
    causal = t >= s
    for g in range(SGU_GROUPS):
        wg = jnp.where(causal, ws_ref[g], 0.0).astype(BF16)
        bias = bs_ref[:, g:g + 1]
        cols = slice(g * SGU_CH, (g + 1) * SGU_CH)
        for n in range(tm // SGU_CHUNK):
            rows = slice(n * SGU_CHUNK, (n + 1) * SGU_CHUNK)
            mixed = _dot(wg, vs_scr[rows, cols].astype(BF16)) + bias
            m_ref[rows, cols] = (u_scr[rows, cols] * mixed).astype(BF16)

    vsn_ref[...] = vs_scr[tm:, :]
    for g in range(SGU_GROUPS):
        cols = slice(g * SGU_CH, (g + 1) * SGU_CH)
        g_scr[0] = vs_scr[tm:, cols]
        g_scr[1] = u_scr[tm:, cols]
        for t in range(seq):
            step_t = pl.ds(t, nseq, stride=seq)
            acc = jnp.full((nseq, SGU_CH), bss_ref[l, g * seq + t], F32)
            for s in range(t + 1):
                acc = acc + wss_ref[l, (g * seq + t) * seq + s] * g_scr[0, pl.ds(s, nseq, stride=seq), :]
            g_scr[2, step_t, :] = g_scr[1, step_t, :] * acc
        ms_ref[:, cols] = g_scr[2].astype(BF16)


def _uv_weight_specs(l):
    return [pl.BlockSpec((None, D_MODEL, UV_BLOCK), functools.partial(lambda b, i: (l, 0, b), UV_OFF // UV_BLOCK + k))
            for k in range(UV_BLOCKS)]


def _sgu_call(l, xn, xns, w_in, lng, lnb, ws, bs_t, ws_sample, bs_sample, w_out, tm, seq):
    T = xn.shape[0]
    n_tiles = T // tm
    Ts = xns.shape[0]
    ts = Ts // n_tiles
    assert Ts == ts * n_tiles and ts % 16 == 0 and ts % seq == 0 and D_MODEL % (16 * n_tiles) == 0
    slab = D_MODEL // n_tiles
    row = lambda i: (i, 0)
    layer = lambda i: (l, 0, 0)
    smem = pl.BlockSpec(memory_space=pltpu.SMEM)
    return pl.pallas_call(
        functools.partial(_sgu_kernel, l=l, seq=seq),
        grid=(n_tiles,),
        in_specs=[
            smem,
            smem,
            pl.BlockSpec((tm, D_MODEL), row),
            pl.BlockSpec((ts, D_MODEL), row),
            *_uv_weight_specs(l),
            pl.BlockSpec((None, 1, SGU_W), layer),
            pl.BlockSpec((None, 1, SGU_W), layer),
            pl.BlockSpec((None, SGU_GROUPS, SGU_CHUNK, SGU_CHUNK), lambda i: (l, 0, 0, 0)),
            pl.BlockSpec((None, SGU_CHUNK, SGU_GROUPS), layer),
            pl.BlockSpec((None, slab, D_MODEL), lambda i: (l, i, 0)),
        ],
        out_specs=[pl.BlockSpec((tm, SGU_W), row), pl.BlockSpec((ts, SGU_W), row),
                   pl.BlockSpec((ts, SGU_W), row), pl.BlockSpec((slab, D_MODEL), row)],
        out_shape=[jax.ShapeDtypeStruct((T, SGU_W), BF16), jax.ShapeDtypeStruct((Ts, SGU_W), BF16),
                   jax.ShapeDtypeStruct((Ts, SGU_W), F32), jax.ShapeDtypeStruct((D_MODEL, D_MODEL), BF16)],
        scratch_shapes=[pltpu.VMEM((tm + ts, SGU_W), F32), pltpu.VMEM((tm + ts, SGU_W), F32),
                        pltpu.VMEM((3, ts, SGU_CH), F32)],
        compiler_params=_params("parallel"),
        name="sgu",
    )(ws_sample, bs_sample, xn, xns, *([w_in] * UV_BLOCKS), lng, lnb, ws, bs_t, w_out)


def _attn_prompt_kernel(sink_ref, q_ref, kp_ref, kc_ref, vp_ref, vc_ref, wg_ref, o_ref, wg_bf_ref, *, l):
    wg_bf_ref[...] = wg_ref[...].astype(BF16)
    blk = WINDOW
    nq = Q_GROUP * blk
    first_tile = pl.program_id(1) == 0
    c = lax.broadcasted_iota(jnp.int32, (blk, nq), 0)
    r = lax.broadcasted_iota(jnp.int32, (blk, nq), 1) % blk
    from_prev = c > r
    k_all = jnp.concatenate([kp_ref[...], kc_ref[...]], axis=0).astype(BF16)
    vt_all = jnp.concatenate([vp_ref[...], vc_ref[...]], axis=0).T.astype(BF16)
    sinks = [jnp.concatenate([jnp.full((1, blk), sink_ref[l, g * Q_GROUP + j], F32) for j in range(Q_GROUP)],
                             axis=1) for g in range(N_KV_HEADS)]

    def scores(jb):
        out = []
        for g in range(N_KV_HEADS):
            qg = jnp.concatenate([q_ref[jb * blk:(jb + 1) * blk, h * HEAD_DIM:(h + 1) * HEAD_DIM]
                                  for h in range(g * Q_GROUP, (g + 1) * Q_GROUP)], axis=0)
            kg = k_all[jb * blk:(jb + 2) * blk, g * HEAD_DIM:(g + 1) * HEAD_DIM]
            out.append(lax.dot_general(kg, qg, (((1,), (1,)), ((), ())), preferred_element_type=F32))
        return out

    nblk = q_ref.shape[0] // blk
    s2 = scores(0)
    for jb in range(nblk):
        s2_next = scores(jb + 1) if jb + 1 < nblk else None
        for g in range(N_KV_HEADS):
            s_prev = s2[g][:blk]
            if jb == 0:
                s_prev = jnp.where(first_tile, NEG_INF, s_prev)
            s = jnp.where(from_prev, s_prev, s2[g][blk:])
            m = jnp.maximum(jnp.max(s, axis=0, keepdims=True), sinks[g])
            p = jnp.exp(s - m)
            p = p / (jnp.sum(p, axis=0, keepdims=True) + jnp.exp(sinks[g] - m))
            p2 = jnp.concatenate([jnp.where(from_prev, p, 0.0), jnp.where(from_prev, 0.0, p)], axis=0)
            vt = vt_all[g * HEAD_DIM:(g + 1) * HEAD_DIM, jb * blk:(jb + 2) * blk]
            o = _dot(vt, p2.astype(BF16)).T
            for j in range(Q_GROUP):
                h = g * Q_GROUP + j
                o_ref[jb * blk:(jb + 1) * blk, h * HEAD_DIM:(h + 1) * HEAD_DIM] = (
                    o[j * blk:(j + 1) * blk].astype(BF16))
        s2 = s2_next


def _attn_prompt_call(l, sinks, q, k, v, w_in, batch, seq, tq):
    nt = seq // tq
    blocks_per_tile = tq // WINDOW
    gate_w = 2 * D_MODEL // (batch * nt)
    assert gate_w % LANES == 0 and GATE_OFF % gate_w == 0
    cur = lambda b, i: (b * nt + i, 0)
    prev = lambda b, i: (b * (seq // WINDOW) + jnp.maximum(i * blocks_per_tile - 1, 0), 0)
    return pl.pallas_call(
        functools.partial(_attn_prompt_kernel, l=l),
        grid=(batch, nt),
        in_specs=[
            pl.BlockSpec(memory_space=pltpu.SMEM),
            pl.BlockSpec((tq, ATTN_W), cur),
            pl.BlockSpec((WINDOW, KV_W), prev),
            pl.BlockSpec((tq, KV_W), cur),
            pl.BlockSpec((WINDOW, KV_W), prev),
            pl.BlockSpec((tq, KV_W), cur),
            pl.BlockSpec((None, D_MODEL, gate_w), lambda b, i: (l, 0, GATE_OFF // gate_w + b * nt + i)),
        ],
        out_specs=[pl.BlockSpec((tq, ATTN_W), cur),
                   pl.BlockSpec((D_MODEL, gate_w), lambda b, i: (0, b * nt + i))],
        out_shape=[jax.ShapeDtypeStruct((batch * seq, ATTN_W), BF16),
                   jax.ShapeDtypeStruct((D_MODEL, 2 * D_MODEL), BF16)],
        compiler_params=_params("parallel", "parallel"),
        name="attn_prompt",
    )(sinks, q, k, k, v, v, w_in)


def _attn_sample_kernel(sink_ref, q_ref, kn_ref, vn_ref, ck_ref, cv_ref, o_ref, nk_ref, nv_ref, *, l, seq):
    nseq = q_ref.shape[0] // seq
    nq = nseq * seq
    rows = Q_GROUP * nq
    ri = lax.broadcasted_iota(jnp.int32, (rows, nseq * WINDOW), 0) % nq
    ci = lax.broadcasted_iota(jnp.int32, (rows, nseq * WINDOW), 1)
    mask_c = (ri // seq == ci // WINDOW) & (ci % WINDOW > ri % seq)
    ri = lax.broadcasted_iota(jnp.int32, (rows, nq), 0) % nq
    ci = lax.broadcasted_iota(jnp.int32, (rows, nq), 1)
    mask_n = (ri // seq == ci // seq) & (ci % seq <= ri % seq)
    dn = (((1,), (1,)), ((), ()))
    scores = []
    for g in range(N_KV_HEADS):
        cols = slice(g * HEAD_DIM, (g + 1) * HEAD_DIM)
        qg = jnp.concatenate([q_ref[:, h * HEAD_DIM:(h + 1) * HEAD_DIM]
                              for h in range(g * Q_GROUP, (g + 1) * Q_GROUP)], axis=0)
        scores.append((lax.dot_general(qg, ck_ref[:, cols].astype(BF16), dn, preferred_element_type=F32),
                       lax.dot_general(qg, kn_ref[:, cols].astype(BF16), dn, preferred_element_type=F32)))
    for g in range(N_KV_HEADS):
        cols = slice(g * HEAD_DIM, (g + 1) * HEAD_DIM)
        heads = range(g * Q_GROUP, (g + 1) * Q_GROUP)
        sink = jnp.concatenate([jnp.full((nq, 1), sink_ref[l, h], F32) for h in heads], axis=0)
        s_c = jnp.where(mask_c, scores[g][0], NEG_INF)
        s_n = jnp.where(mask_n, scores[g][1], NEG_INF)
        m = jnp.maximum(jnp.maximum(jnp.max(s_c, axis=-1, keepdims=True),
                                    jnp.max(s_n, axis=-1, keepdims=True)), sink)
        p_c = jnp.exp(s_c - m)
        p_n = jnp.exp(s_n - m)
        denom = (jnp.sum(p_c, axis=-1, keepdims=True) + jnp.sum(p_n, axis=-1, keepdims=True)
                 + jnp.exp(sink - m))
        o = (_dot((p_c / denom).astype(BF16), cv_ref[:, cols].astype(BF16))
             + _dot((p_n / denom).astype(BF16), vn_ref[:, cols].astype(BF16)))
        for j, h in enumerate(heads):
            o_ref[:, h * HEAD_DIM:(h + 1) * HEAD_DIM] = o[j * nq:(j + 1) * nq].astype(BF16)
    for b in range(nseq):
        nk_ref[b * WINDOW:(b + 1) * WINDOW - seq, :] = ck_ref[b * WINDOW + seq:(b + 1) * WINDOW, :]
        nv_ref[b * WINDOW:(b + 1) * WINDOW - seq, :] = cv_ref[b * WINDOW + seq:(b + 1) * WINDOW, :]
        nk_ref[(b + 1) * WINDOW - seq:(b + 1) * WINDOW, :] = kn_ref[b * seq:(b + 1) * seq, :]
        nv_ref[(b + 1) * WINDOW - seq:(b + 1) * WINDOW, :] = vn_ref[b * seq:(b + 1) * seq, :]


def _attn_sample_call(l, sinks, q, k, v, ck, cv, seq, seq_block):
    T = q.shape[0]
    nseq = T // seq
    row = lambda i: (i, 0)
    cache = lambda i: (l, i, 0)
    return pl.pallas_call(
        functools.partial(_attn_sample_kernel, l=l, seq=seq),
        grid=(nseq // seq_block,),
        in_specs=[
            pl.BlockSpec(memory_space=pltpu.SMEM),
            pl.BlockSpec((seq_block * seq, ATTN_W), row),
            pl.BlockSpec((seq_block * seq, KV_W), row),
            pl.BlockSpec((seq_block * seq, KV_W), row),
            pl.BlockSpec((None, seq_block * WINDOW, KV_W), cache),
            pl.BlockSpec((None, seq_block * WINDOW, KV_W), cache),
        ],
        out_specs=[
            pl.BlockSpec((seq_block * seq, ATTN_W), row),
            pl.BlockSpec((seq_block * WINDOW, KV_W), row),
            pl.BlockSpec((seq_block * WINDOW, KV_W), row),
        ],
        out_shape=[
            jax.ShapeDtypeStruct((T, ATTN_W), BF16),
            jax.ShapeDtypeStruct((nseq * WINDOW, KV_W), F32),
            jax.ShapeDtypeStruct((nseq * WINDOW, KV_W), F32),
        ],
        compiler_params=_params("parallel"),
        name="attn_sample",
    )(sinks, q, k, v, ck, cv)


def _merge_kernel(x_ref, xn_ref, a_ref, m_ref, wga_ref, wgm_ref, wau_ref, wsu_ref, wout_ref,
                  h_ref, merged_scr):
    j = pl.program_id(1)
    xn = xn_ref[...]
    ga = jax.nn.sigmoid(_dot(xn, wga_ref[...]))
    gm = jax.nn.sigmoid(_dot(xn, wgm_ref[...]))
    merged = ga * _dot(a_ref[...], wau_ref[...]) + gm * _dot(m_ref[...], wsu_ref[...])
    merged_scr[j] = merged.astype(BF16)

    @pl.when(j == pl.num_programs(1) - 1)
    def _():
        tn = merged_scr.shape[2]
        acc = x_ref[...]
        for c in range(merged_scr.shape[0]):
            acc = acc + _dot(merged_scr[c], wout_ref[c * tn:(c + 1) * tn, :])
        h_ref[...] = acc


def _merge_call(x, xn, a, m, w_gate, w_au, w_su, w_out, tm, tn):
    T = x.shape[0]
    nj = D_MODEL // tn
    row = lambda i, j: (i, 0)
    col = lambda i, j: (0, j)
    x_row = lambda i, j: (jnp.where(j == 0, jnp.maximum(i - 1, 0), i), 0)
    return pl.pallas_call(
        _merge_kernel,
        grid=(T // tm, nj),
        in_specs=[
            pl.BlockSpec((tm, D_MODEL), x_row if nj > 1 else row),
            pl.BlockSpec((tm, D_MODEL), row),
            pl.BlockSpec((tm, ATTN_W), row),
            pl.BlockSpec((tm, SGU_W), row),
            pl.BlockSpec((D_MODEL, tn), col),
            pl.BlockSpec((D_MODEL, tn), lambda i, j: (0, nj + j)),
            pl.BlockSpec((ATTN_W, tn), col),
            pl.BlockSpec((SGU_W, tn), col),
            pl.BlockSpec((D_MODEL, D_MODEL), lambda i, j: (0, 0)),
        ],
        out_specs=pl.BlockSpec((tm, D_MODEL), row),
        out_shape=jax.ShapeDtypeStruct((T, D_MODEL), F32),
        scratch_shapes=[pltpu.VMEM((nj, tm, tn), BF16)],
        compiler_params=_params("parallel", "arbitrary"),
        name="merge",
    )(x, xn, a, m, w_gate, w_gate, w_au, w_su, w_out)


def _ffn_kernel(h_ref, hs_ref, g2_ref, w1_ref, w2_ref, y_ref, ys_ref, hn_scr):
    tm = h_ref.shape[0]

    @pl.when(pl.program_id(1) == 0)
    def _():
        h = h_ref[...]
        hn_scr[:tm, :] = _rms_rows(h, g2_ref[...]).astype(BF16)
        y_ref[...] = h
        hs = hs_ref[...]
        hn_scr[tm:, :] = _rms_rows(hs, g2_ref[...]).astype(BF16)
        ys_ref[...] = hs

    half = tm // 2
    w1 = w1_ref[...].astype(BF16)
    w2 = w2_ref[...].astype(BF16)
    acts = []
    for rows in (slice(0, half), slice(half, hn_scr.shape[0])):
        a = jnp.maximum(_dot(hn_scr[rows, :], w1), 0.0)
        acts.append((a * a).astype(BF16))
    y_ref[:half, :] += _dot(acts[0], w2)
    second = _dot(acts[1], w2)
    y_ref[half:, :] += second[:half]
    ys_ref[...] += second[half:]


def _ffn_call(l, h, hs, g2, w1, w2, tm, tf):
    T = h.shape[0]
    n_tiles = T // tm
    ts = hs.shape[0] // n_tiles
    assert hs.shape[0] == ts * n_tiles and ts % 16 == 0
    nf = D_FF // tf
    row = lambda i, f: (i, 0)
    h_row = lambda i, f: (jnp.minimum(i + (f >= nf // 2).astype(jnp.int32), n_tiles - 1), 0)
    return pl.pallas_call(
        _ffn_kernel,
        grid=(n_tiles, nf),
        in_specs=[
            pl.BlockSpec((tm, D_MODEL), h_row),
            pl.BlockSpec((ts, D_MODEL), h_row),
            pl.BlockSpec((None, 1, D_MODEL), lambda i, f: (l, 0, 0)),
            pl.BlockSpec((None, D_MODEL, tf), lambda i, f: (l, 0, f)),
            pl.BlockSpec((None, tf, D_MODEL), lambda i, f: (l, f, 0)),
        ],
        out_specs=[pl.BlockSpec((tm, D_MODEL), row), pl.BlockSpec((ts, D_MODEL), row)],
        out_shape=[jax.ShapeDtypeStruct((T, D_MODEL), F32), jax.ShapeDtypeStruct(hs.shape, F32)],
        scratch_shapes=[pltpu.VMEM((tm + ts, D_MODEL), BF16)],
        compiler_params=_params("parallel", "arbitrary"),
        name="ffn",
    )(h, hs, g2, w1, w2)


def _rope_tables(pos):
    half = HEAD_DIM // 2
    inv = jnp.power(jnp.float32(ROPE_THETA), -jnp.arange(half, dtype=jnp.float32) / half)
    ang = pos.astype(jnp.float32)[:, None] * inv[None, :]
    cos = jnp.cos(ang)
    sin = jnp.sin(ang)
    return jnp.tile(cos, (1, 2 * N_KV_HEADS)), jnp.tile(jnp.concatenate([-sin, sin], axis=-1), (1, N_KV_HEADS))


def kernel(x_prompt, x_sample, cache_k, cache_v, norm1_g, w_in, q_norm_g, k_norm_g, attn_sinks,
           sgu_ln_g, sgu_ln_b, sgu_w, sgu_b, w_attn_up, w_sgu_up, w_out, norm2_g, w_ff1, w_ff2):
    batch, seq, _ = x_prompt.shape
    dec_batch, dec_seq, _ = x_sample.shape
    tp = batch * seq
    ts = dec_batch * dec_seq
    tm = 512
    tn = 512
    tq = 512
    assert seq % tm == 0 and tm % SGU_CHUNK == 0 and seq % tq == 0 and tq % WINDOW == 0

    cos_p, sin_p = _rope_tables(jnp.arange(seq))
    cos_s, sin_s = _rope_tables(jnp.tile(PAST_LEN + jnp.arange(dec_seq), dec_batch))

    g1 = norm1_g.reshape(DEPTH, 1, D_MODEL)
    g2 = norm2_g.reshape(DEPTH, 1, D_MODEL)
    qg = jnp.tile(q_norm_g, (1, N_KV_HEADS)).reshape(DEPTH, 1, KV_W)
    kg = jnp.tile(k_norm_g, (1, N_KV_HEADS)).reshape(DEPTH, 1, KV_W)
    lng = sgu_ln_g.reshape(DEPTH, 1, SGU_W)
    lnb = sgu_ln_b.reshape(DEPTH, 1, SGU_W)
    bs_t = jnp.swapaxes(sgu_b, 1, 2)
    ws_s = sgu_w[:, :, :dec_seq, :dec_seq].reshape(DEPTH, -1)
    bs_s = sgu_b[:, :, :dec_seq].reshape(DEPTH, -1)
    ck = cache_k.reshape(DEPTH, dec_batch * WINDOW, KV_W)
    cv = cache_v.reshape(DEPTH, dec_batch * WINDOW, KV_W)

    xp = x_prompt.reshape(tp, D_MODEL)
    xs = x_sample.reshape(ts, D_MODEL)
    kp_l, vp_l, ks_l, vs_l, sv_l = [], [], [], [], []
    for l in range(DEPTH):
        xn, q, k, v, xns, qs, ks, vs, w_au, w_su = _qkv_call(
            l, xp, xs, g1, w_in, cos_p, sin_p, cos_s, sin_s, qg, kg, w_attn_up, w_sgu_up, tm, seq // tm)
        m, ms, vsn, w_o = _sgu_call(l, xn, xns, w_in, lng, lnb, sgu_w, bs_t, ws_s, bs_s, w_out, tm, dec_seq)
        a, w_gate = _attn_prompt_call(l, attn_sinks, q, k, v, w_in, batch, seq, tq)
        hp = _merge_call(xp, xn, a, m, w_gate, w_au, w_su, w_o, tm, tn)
        a, nk, nv = _attn_sample_call(l, attn_sinks, qs, ks, vs, ck, cv, dec_seq, 8)
        hs = _merge_call(xs, xns, a, ms, w_gate, w_au, w_su, w_o, ts, tn)
        xp, xs = _ffn_call(l, hp, hs, g2, w_ff1, w_ff2, 2 * tm, tn)
        kp_l.append(k.reshape(batch, seq, KV_W)[:, -WINDOW:].reshape(batch, WINDOW, N_KV_HEADS, HEAD_DIM))
        vp_l.append(v.reshape(batch, seq, KV_W)[:, -WINDOW:].reshape(batch, WINDOW, N_KV_HEADS, HEAD_DIM))
        ks_l.append(nk.reshape(dec_batch, WINDOW, N_KV_HEADS, HEAD_DIM))
        vs_l.append(nv.reshape(dec_batch, WINDOW, N_KV_HEADS, HEAD_DIM))
        sv_l.append(vsn.reshape(dec_batch, dec_seq, SGU_GROUPS, SGU_CH))

    return (xp.reshape(batch, seq, D_MODEL), xs.reshape(dec_batch, dec_seq, D_MODEL),
            jnp.stack(kp_l), jnp.stack(vp_l), jnp.stack(ks_l), jnp.stack(vs_l), jnp.stack(sv_l))
```

```python
import functools
import math

import jax
import jax.numpy as jnp
import numpy as np
from jax import lax
from jax.experimental import pallas as pl
from jax.experimental.pallas import tpu as pltpu

D_MODEL = 2048
DEPTH = 4
PAST_LEN = 16384
HEAD_DIM = 64
N_HEADS = 16
N_KV_HEADS = 4
Q_GROUP = 4
ATTN_W = N_HEADS * HEAD_DIM
KV_W = N_KV_HEADS * HEAD_DIM
WINDOW = 128
ROPE_THETA = 10000.0
SGU_CHUNK = 128
SGU_W = D_MODEL // 2
SGU_GROUPS = 8
SGU_CH = SGU_W // SGU_GROUPS
D_FF = 4 * D_MODEL
RMS_EPS = 1e-6
LN_EPS = 1e-5
NEG_INF = -1e30

QKV_W = ATTN_W + 2 * KV_W
UV_OFF = QKV_W
GATE_OFF = QKV_W + 2 * SGU_W

MERGE_OUT_BLOCKS = 4
LANES = 128
VMEM_LIMIT = 58 * 1024 * 1024

BF16 = jnp.bfloat16
F32 = jnp.float32


def _dot(a, b):
    return jnp.dot(a, b, preferred_element_type=F32)


def _rms_rows(x, g):
    ms = jnp.mean(x * x, axis=-1, keepdims=True)
    return x * lax.rsqrt(ms + RMS_EPS) * g


def _gelu(x):
    return 0.5 * x * (1.0 + lax.erf(x * np.float32(math.sqrt(0.5))))


def _params(*sem):
    return pltpu.CompilerParams(dimension_semantics=sem, vmem_limit_bytes=VMEM_LIMIT)


def _head_norm_rope(z, gain, cos, sin, gmat, first_half):
    width = z.shape[1]
    sq = z * z
    hi = sq.astype(BF16)
    lo = (sq - hi.astype(F32)).astype(BF16)
    ssum = _dot(jnp.concatenate([hi, lo], axis=1), gmat)
    zn = z * lax.rsqrt(ssum * (1.0 / HEAD_DIM) + RMS_EPS) * gain
    rot = jnp.where(first_half, pltpu.roll(zn, width - HEAD_DIM // 2, 1), pltpu.roll(zn, HEAD_DIM // 2, 1))
    return zn * cos + rot * sin


def _qkv_kernel(x_ref, xs_ref, g1_ref, w_ref, cos_ref, sin_ref, coss_ref, sins_ref, qg_ref, kg_ref,
                wau_ref, wsu_ref,
                xn_ref, q_ref, k_ref, v_ref, xns_ref, qs_ref, ks_ref, vs_ref, wau_bf_ref, wsu_bf_ref):
    wau_bf_ref[...] = wau_ref[...].astype(BF16)
    wsu_bf_ref[...] = wsu_ref[...].astype(BF16)
    tm = x_ref.shape[0]
    rows = tm + xs_ref.shape[0]
    cw = KV_W
    g1 = g1_ref[...]
    xn = jnp.concatenate([_rms_rows(x_ref[...], g1).astype(BF16), _rms_rows(xs_ref[...], g1).astype(BF16)],
                         axis=0)
    xn_ref[...] = xn[:tm]
    xns_ref[...] = xn[tm:]
    lane = lax.broadcasted_iota(jnp.int32, (rows, cw), 1)
    first_half = (lane % HEAD_DIM) < (HEAD_DIM // 2)
    r = (lax.broadcasted_iota(jnp.int32, (2 * cw, cw), 0) % cw) // HEAD_DIM
    c = lax.broadcasted_iota(jnp.int32, (2 * cw, cw), 1) // HEAD_DIM
    gmat = jnp.where(r == c, 1.0, 0.0).astype(BF16)
    cos = jnp.concatenate([cos_ref[...], coss_ref[...]], axis=0)
    sin = jnp.concatenate([sin_ref[...], sins_ref[...]], axis=0)
    scale = HEAD_DIM ** -0.5
    n_q = ATTN_W // cw

    def project(c):
        return _dot(xn, w_ref[:, c * cw:(c + 1) * cw].astype(BF16))

    z = project(0)
    for c in range(n_q + 1):
        z_next = project(c + 1)
        if c < n_q:
            q = (_head_norm_rope(z, qg_ref[...], cos, sin, gmat, first_half) * scale).astype(BF16)
            q_ref[:, c * cw:(c + 1) * cw] = q[:tm]
            qs_ref[:, c * cw:(c + 1) * cw] = q[tm:]
        else:
            k = _head_norm_rope(z, kg_ref[...], cos, sin, gmat, first_half)
            k_ref[...] = k[:tm]
            ks_ref[...] = k[tm:]
        z = z_next
    v_ref[...] = z[:tm]
    vs_ref[...] = z[tm:]


def _qkv_call(l, x, xs, g1, w_in, cos_t, sin_t, cos_s, sin_s, qg, kg, w_attn_up, w_sgu_up, tm, n_pos_blocks):
    T = x.shape[0]
    n_tiles = T // tm
    Ts = xs.shape[0]
    ts = Ts // n_tiles
    assert Ts == ts * n_tiles and ts % 16 == 0
    assert ATTN_W % (16 * n_tiles) == 0 and SGU_W % (16 * n_tiles) == 0
    row = lambda i: (i, 0)
    layer = lambda i: (l, 0, 0)
    pos = lambda i: (i % n_pos_blocks, 0)
    return pl.pallas_call(
        _qkv_kernel,
        grid=(n_tiles,),
        in_specs=[
            pl.BlockSpec((tm, D_MODEL), row),
            pl.BlockSpec((ts, D_MODEL), row),
            pl.BlockSpec((None, 1, D_MODEL), layer),
            pl.BlockSpec((None, D_MODEL, QKV_W), layer),
            pl.BlockSpec((tm, KV_W), pos),
            pl.BlockSpec((tm, KV_W), pos),
            pl.BlockSpec((ts, KV_W), row),
            pl.BlockSpec((ts, KV_W), row),
            pl.BlockSpec((None, 1, KV_W), layer),
            pl.BlockSpec((None, 1, KV_W), layer),
            pl.BlockSpec((None, ATTN_W // n_tiles, D_MODEL), lambda i: (l, i, 0)),
            pl.BlockSpec((None, SGU_W // n_tiles, D_MODEL), lambda i: (l, i, 0)),
        ],
        out_specs=[
            pl.BlockSpec((tm, D_MODEL), row),
            pl.BlockSpec((tm, ATTN_W), row),
            pl.BlockSpec((tm, KV_W), row),
            pl.BlockSpec((tm, KV_W), row),
            pl.BlockSpec((ts, D_MODEL), row),
            pl.BlockSpec((ts, ATTN_W), row),
            pl.BlockSpec((ts, KV_W), row),
            pl.BlockSpec((ts, KV_W), row),
            pl.BlockSpec((ATTN_W // n_tiles, D_MODEL), row),
            pl.BlockSpec((SGU_W // n_tiles, D_MODEL), row),
        ],
        out_shape=[
            jax.ShapeDtypeStruct((T, D_MODEL), BF16),
            jax.ShapeDtypeStruct((T, ATTN_W), BF16),
            jax.ShapeDtypeStruct((T, KV_W), F32),
            jax.ShapeDtypeStruct((T, KV_W), F32),
            jax.ShapeDtypeStruct((Ts, D_MODEL), BF16),
            jax.ShapeDtypeStruct((Ts, ATTN_W), BF16),
            jax.ShapeDtypeStruct((Ts, KV_W), F32),
            jax.ShapeDtypeStruct((Ts, KV_W), F32),
            jax.ShapeDtypeStruct((ATTN_W, D_MODEL), BF16),
            jax.ShapeDtypeStruct((SGU_W, D_MODEL), BF16),
        ],
        compiler_params=_params("parallel"),
        name="qkv",
    )(x, xs, g1, w_in, cos_t, sin_t, cos_s, sin_s, qg, kg, w_attn_up, w_sgu_up)


UV_BLOCK = 512
UV_BLOCKS = 2 * SGU_W // UV_BLOCK


def _uv_project(xn, w_refs, lng_ref, lnb_ref, u_scr, vs_scr):
    cw = 2 * LANES
    per_block = UV_BLOCK // cw
    n_u = SGU_W // cw
    for c in list(range(n_u, 2 * n_u)) + list(range(n_u)):
        w = w_refs[c // per_block][:, (c % per_block) * cw:(c % per_block + 1) * cw].astype(BF16)
        dst = u_scr if c < n_u else vs_scr
        c_dst = c % n_u
        dst[:, c_dst * cw:(c_dst + 1) * cw] = _gelu(_dot(xn, w))
    vs = vs_scr[...]
    mu = jnp.mean(vs, axis=-1, keepdims=True)
    d = vs - mu
    var = jnp.mean(d * d, axis=-1, keepdims=True)
    vs_scr[...] = d * lax.rsqrt(var + LN_EPS) * lng_ref[...] + lnb_ref[...]


def _sgu_kernel(wss_ref, bss_ref, xn_ref, xns_ref, *refs, l, seq):
    w_refs = refs[:UV_BLOCKS]
    (lng_ref, lnb_ref, ws_ref, bs_ref, wout_ref,
     m_ref, ms_ref, vsn_ref, wout_bf_ref, u_scr, vs_scr, g_scr) = refs[UV_BLOCKS:]
    n_out, _, tno = wout_bf_ref.shape
    for n in range(n_out):
        wout_bf_ref[n] = wout_ref[:, n * tno:(n + 1) * tno].astype(BF16)
    tm = xn_ref.shape[0]
    nseq = xns_ref.shape[0] // seq
    xn = jnp.concatenate([xn_ref[...], xns_ref[...]], axis=0)
    _uv_project(xn, w_refs, lng_ref, lnb_ref, u_scr, vs_scr)

    t = lax.broadcasted_iota(jnp.int32, (SGU_CHUNK, SGU_CHUNK), 0)
    s = lax.broadcasted_iota(jnp.int32, (SGU_CHUNK, SGU_CHUNK), 1)
    causal = t >= s
    for g in range(SGU_GROUPS):
        wg = jnp.where(causal, ws_ref[g], 0.0).astype(BF16)
        bias = bs_ref[:, g:g + 1]
        cols = slice(g * SGU_CH, (g + 1) * SGU_CH)
        for n in range(tm // SGU_CHUNK):
            rows = slice(n * SGU_CHUNK, (n + 1) * SGU_CHUNK)
            mixed = _dot(wg, vs_scr[rows, cols].astype(BF16)) + bias
            m_ref[rows, cols] = (u_scr[rows, cols] * mixed).astype(BF16)

    vsn_ref[...] = vs_scr[tm:, :]
    for g in range(SGU_GROUPS):
        cols = slice(g * SGU_CH, (g + 1) * SGU_CH)
        g_scr[0] = vs_scr[tm:, cols]
        g_scr[1] = u_scr[tm:, cols]
        for t in range(seq):
            step_t = pl.ds(t, nseq, stride=seq)
            acc = jnp.full((nseq, SGU_CH), bss_ref[l, g * seq + t], F32)
            for s in range(t + 1):
                acc = acc + wss_ref[l, (g * seq + t) * seq + s] * g_scr[0, pl.ds(s, nseq, stride=seq), :]
            g_scr[2, step_t, :] = g_scr[1, step_t, :] * acc
        ms_ref[:, cols] = g_scr[2].astype(BF16)


def _uv_weight_specs(l):
    return [pl.BlockSpec((None, D_MODEL, UV_BLOCK), functools.partial(lambda b, i: (l, 0, b), UV_OFF // UV_BLOCK + k))
            for k in range(UV_BLOCKS)]


def _sgu_call(l, xn, xns, w_in, lng, lnb, ws, bs_t, ws_sample, bs_sample, w_out, tm, seq):
    T = xn.shape[0]
    n_tiles = T // tm
    Ts = xns.shape[0]
    ts = Ts // n_tiles
    assert Ts == ts * n_tiles and ts % 16 == 0 and ts % seq == 0 and D_MODEL % (16 * n_tiles) == 0
    slab = D_MODEL // n_tiles
    row = lambda i: (i, 0)
    layer = lambda i: (l, 0, 0)
    smem = pl.BlockSpec(memory_space=pltpu.SMEM)
    return pl.pallas_call(
        functools.partial(_sgu_kernel, l=l, seq=seq),
        grid=(n_tiles,),
        in_specs=[
            smem,
            smem,
            pl.BlockSpec((tm, D_MODEL), row),
            pl.BlockSpec((ts, D_MODEL), row),
            *_uv_weight_specs(l),
            pl.BlockSpec((None, 1, SGU_W), layer),
            pl.BlockSpec((None, 1, SGU_W), layer),
            pl.BlockSpec((None, SGU_GROUPS, SGU_CHUNK, SGU_CHUNK), lambda i: (l, 0, 0, 0)),
            pl.BlockSpec((None, SGU_CHUNK, SGU_GROUPS), layer),
            pl.BlockSpec((None, slab, D_MODEL), lambda i: (l, i, 0)),
        ],
        out_specs=[pl.BlockSpec((tm, SGU_W), row), pl.BlockSpec((ts, SGU_W), row),
                   pl.BlockSpec((ts, SGU_W), row),
                   pl.BlockSpec((MERGE_OUT_BLOCKS, slab, D_MODEL // MERGE_OUT_BLOCKS), lambda i: (0, i, 0))],
        out_shape=[jax.ShapeDtypeStruct((T, SGU_W), BF16), jax.ShapeDtypeStruct((Ts, SGU_W), BF16),
                   jax.ShapeDtypeStruct((Ts, SGU_W), F32),
                   jax.ShapeDtypeStruct((MERGE_OUT_BLOCKS, D_MODEL, D_MODEL // MERGE_OUT_BLOCKS), BF16)],
        scratch_shapes=[pltpu.VMEM((tm + ts, SGU_W), F32), pltpu.VMEM((tm + ts, SGU_W), F32),
                        pltpu.VMEM((3, ts, SGU_CH), F32)],
        compiler_params=_params("parallel"),
        name="sgu",
    )(ws_sample, bs_sample, xn, xns, *([w_in] * UV_BLOCKS), lng, lnb, ws, bs_t, w_out)


def _attn_prompt_kernel(sink_ref, q_ref, kp_ref, kc_ref, vp_ref, vc_ref, wg_ref, o_ref, wg_bf_ref, *, l):
    wg_bf_ref[...] = wg_ref[...].astype(BF16)
    blk = WINDOW
    nq = Q_GROUP * blk
    first_tile = pl.program_id(1) == 0
    c = lax.broadcasted_iota(jnp.int32, (blk, nq), 0)
    r = lax.broadcasted_iota(jnp.int32, (blk, nq), 1) % blk
    from_prev = c > r
    k_all = jnp.concatenate([kp_ref[...], kc_ref[...]], axis=0).astype(BF16)
    vt_all = jnp.concatenate([vp_ref[...], vc_ref[...]], axis=0).T.astype(BF16)
    sinks = [jnp.concatenate([jnp.full((1, blk), sink_ref[l, g * Q_GROUP + j], F32) for j in range(Q_GROUP)],
                             axis=1) for g in range(N_KV_HEADS)]

    def scores(jb):
        out = []
        for g in range(N_KV_HEADS):
            qg = jnp.concatenate([q_ref[jb * blk:(jb + 1) * blk, h * HEAD_DIM:(h + 1) * HEAD_DIM]
                                  for h in range(g * Q_GROUP, (g + 1) * Q_GROUP)], axis=0)
            kg = k_all[jb * blk:(jb + 2) * blk, g * HEAD_DIM:(g + 1) * HEAD_DIM]
            out.append(lax.dot_general(kg, qg, (((1,), (1,)), ((), ())), preferred_element_type=F32))
        return out

    nblk = q_ref.shape[0] // blk
    s2 = scores(0)
    for jb in range(nblk):
        s2_next = scores(jb + 1) if jb + 1 < nblk else None
        for g in range(N_KV_HEADS):
            s_prev = s2[g][:blk]
            if jb == 0:
                s_prev = jnp.where(first_tile, NEG_INF, s_prev)
            s = jnp.where(from_prev, s_prev, s2[g][blk:])
            m = jnp.maximum(jnp.max(s, axis=0, keepdims=True), sinks[g])
            p = jnp.exp(s - m)
            p = p / (jnp.sum(p, axis=0, keepdims=True) + jnp.exp(sinks[g] - m))
            p2 = jnp.concatenate([jnp.where(from_prev, p, 0.0), jnp.where(from_prev, 0.0, p)], axis=0)
            vt = vt_all[g * HEAD_DIM:(g + 1) * HEAD_DIM, jb * blk:(jb + 2) * blk]
            o = _dot(vt, p2.astype(BF16)).T
            for j in range(Q_GROUP):
                h = g * Q_GROUP + j
                o_ref[jb * blk:(jb + 1) * blk, h * HEAD_DIM:(h + 1) * HEAD_DIM] = (
                    o[j * blk:(j + 1) * blk].astype(BF16))
        s2 = s2_next


def _attn_prompt_call(l, sinks, q, k, v, w_in, batch, seq, tq):
    nt = seq // tq
    blocks_per_tile = tq // WINDOW
    gate_w = 2 * D_MODEL // (batch * nt)
    assert gate_w % LANES == 0 and GATE_OFF % gate_w == 0
    cur = lambda b, i: (b * nt + i, 0)
    prev = lambda b, i: (b * (seq // WINDOW) + jnp.maximum(i * blocks_per_tile - 1, 0), 0)
    return pl.pallas_call(
        functools.partial(_attn_prompt_kernel, l=l),
        grid=(batch, nt),
        in_specs=[
            pl.BlockSpec(memory_space=pltpu.SMEM),
            pl.BlockSpec((tq, ATTN_W), cur),
            pl.BlockSpec((WINDOW, KV_W), prev),
            pl.BlockSpec((tq, KV_W), cur),
            pl.BlockSpec((WINDOW, KV_W), prev),
            pl.BlockSpec((tq, KV_W), cur),
            pl.BlockSpec((None, D_MODEL, gate_w), lambda b, i: (l, 0, GATE_OFF // gate_w + b * nt + i)),
        ],
        out_specs=[pl.BlockSpec((tq, ATTN_W), cur),
                   pl.BlockSpec((D_MODEL, gate_w), lambda b, i: (0, b * nt + i))],
        out_shape=[jax.ShapeDtypeStruct((batch * seq, ATTN_W), BF16),
                   jax.ShapeDtypeStruct((D_MODEL, 2 * D_MODEL), BF16)],
        compiler_params=_params("parallel", "parallel"),
        name="attn_prompt",
    )(sinks, q, k, k, v, v, w_in)


def _attn_sample_kernel(sink_ref, q_ref, kn_ref, vn_ref, ck_ref, cv_ref, o_ref, nk_ref, nv_ref, *, l, seq):
    nseq = q_ref.shape[0] // seq
    nq = nseq * seq
    rows = Q_GROUP * nq
    ri = lax.broadcasted_iota(jnp.int32, (rows, nseq * WINDOW), 0) % nq
    ci = lax.broadcasted_iota(jnp.int32, (rows, nseq * WINDOW), 1)
    mask_c = (ri // seq == ci // WINDOW) & (ci % WINDOW > ri % seq)
    ri = lax.broadcasted_iota(jnp.int32, (rows, nq), 0) % nq
    ci = lax.broadcasted_iota(jnp.int32, (rows, nq), 1)
    mask_n = (ri // seq == ci // seq) & (ci % seq <= ri % seq)
    dn = (((1,), (1,)), ((), ()))
    scores = []
    for g in range(N_KV_HEADS):
        cols = slice(g * HEAD_DIM, (g + 1) * HEAD_DIM)
        qg = jnp.concatenate([q_ref[:, h * HEAD_DIM:(h + 1) * HEAD_DIM]
                              for h in range(g * Q_GROUP, (g + 1) * Q_GROUP)], axis=0)
        scores.append((lax.dot_general(qg, ck_ref[:, cols].astype(BF16), dn, preferred_element_type=F32),
                       lax.dot_general(qg, kn_ref[:, cols].astype(BF16), dn, preferred_element_type=F32)))
    for g in range(N_KV_HEADS):
        cols = slice(g * HEAD_DIM, (g + 1) * HEAD_DIM)
        heads = range(g * Q_GROUP, (g + 1) * Q_GROUP)
        sink = jnp.concatenate([jnp.full((nq, 1), sink_ref[l, h], F32) for h in heads], axis=0)
        s_c = jnp.where(mask_c, scores[g][0], NEG_INF)
        s_n = jnp.where(mask_n, scores[g][1], NEG_INF)
        m = jnp.maximum(jnp.maximum(jnp.max(s_c, axis=-1, keepdims=True),
                                    jnp.max(s_n, axis=-1, keepdims=True)), sink)
        p_c = jnp.exp(s_c - m)
        p_n = jnp.exp(s_n - m)
        denom = (jnp.sum(p_c, axis=-1, keepdims=True) + jnp.sum(p_n, axis=-1, keepdims=True)
                 + jnp.exp(sink - m))
        o = (_dot((p_c / denom).astype(BF16), cv_ref[:, cols].astype(BF16))
             + _dot((p_n / denom).astype(BF16), vn_ref[:, cols].astype(BF16)))
        for j, h in enumerate(heads):
            o_ref[:, h * HEAD_DIM:(h + 1) * HEAD_DIM] = o[j * nq:(j + 1) * nq].astype(BF16)
    for b in range(nseq):
        nk_ref[b * WINDOW:(b + 1) * WINDOW - seq, :] = ck_ref[b * WINDOW + seq:(b + 1) * WINDOW, :]
        nv_ref[b * WINDOW:(b + 1) * WINDOW - seq, :] = cv_ref[b * WINDOW + seq:(b + 1) * WINDOW, :]
        nk_ref[(b + 1) * WINDOW - seq:(b + 1) * WINDOW, :] = kn_ref[b * seq:(b + 1) * seq, :]
        nv_ref[(b + 1) * WINDOW - seq:(b + 1) * WINDOW, :] = vn_ref[b * seq:(b + 1) * seq, :]


def _attn_sample_call(l, sinks, q, k, v, ck, cv, seq, seq_block):
    T = q.shape[0]
    nseq = T // seq
    row = lambda i: (i, 0)
    cache = lambda i: (l, i, 0)
    return pl.pallas_call(
        functools.partial(_attn_sample_kernel, l=l, seq=seq),
        grid=(nseq // seq_block,),
        in_specs=[
            pl.BlockSpec(memory_space=pltpu.SMEM),
            pl.BlockSpec((seq_block * seq, ATTN_W), row),
            pl.BlockSpec((seq_block * seq, KV_W), row),
            pl.BlockSpec((seq_block * seq, KV_W), row),
            pl.BlockSpec((None, seq_block * WINDOW, KV_W), cache),
            pl.BlockSpec((None, seq_block * WINDOW, KV_W), cache),
        ],
        out_specs=[
            pl.BlockSpec((seq_block * seq, ATTN_W), row),
            pl.BlockSpec((seq_block * WINDOW, KV_W), row),
            pl.BlockSpec((seq_block * WINDOW, KV_W), row),
        ],
        out_shape=[
            jax.ShapeDtypeStruct((T, ATTN_W), BF16),
            jax.ShapeDtypeStruct((nseq * WINDOW, KV_W), F32),
            jax.ShapeDtypeStruct((nseq * WINDOW, KV_W), F32),
        ],
        compiler_params=_params("parallel"),
        name="attn_sample",
    )(sinks, q, k, v, ck, cv)


def _merge_kernel(xn_ref, a_ref, m_ref, wga_ref, wgm_ref, wau_ref, wsu_ref, wout_ref, x_ref,
                  h_ref, merged_scr):
    nj, _, tn = merged_scr.shape
    s = pl.program_id(1)

    @pl.when(s < nj)
    def _():
        xn = xn_ref[...]
        ga = jax.nn.sigmoid(_dot(xn, wga_ref[...]))
        gm = jax.nn.sigmoid(_dot(xn, wgm_ref[...]))
        merged = ga * _dot(a_ref[...], wau_ref[...]) + gm * _dot(m_ref[...], wsu_ref[...])
        merged_scr[s] = merged.astype(BF16)

    @pl.when(s >= nj)
    def _():
        n = s - nj
        acc = x_ref[...]
        for c in range(nj):
            acc = acc + _dot(merged_scr[c], wout_ref[n, c * tn:(c + 1) * tn, :])
        h_ref[...] = acc


def _merge_call(x, xn, a, m, w_gate, w_au, w_su, w_out, tm, tn):
    T = x.shape[0]
    n_tiles = T // tm
    nj = D_MODEL // tn
    n_out, _, tno = w_out.shape
    merging = lambda s: s < nj
    row = lambda i, s: (jnp.minimum(i + jnp.where(merging(s), 0, 1), n_tiles - 1), 0)
    gate_col = lambda s: jnp.where(merging(s), s, 0)
    out_col = lambda s: jnp.maximum(s - nj, 0)
    return pl.pallas_call(
        _merge_kernel,
        grid=(n_tiles, nj + n_out),
        in_specs=[
            pl.BlockSpec((tm, D_MODEL), row),
            pl.BlockSpec((tm, ATTN_W), row),
            pl.BlockSpec((tm, SGU_W), row),
            pl.BlockSpec((D_MODEL, tn), lambda i, s: (0, gate_col(s))),
            pl.BlockSpec((D_MODEL, tn), lambda i, s: (0, nj + gate_col(s))),
            pl.BlockSpec((ATTN_W, tn), lambda i, s: (0, gate_col(s))),
            pl.BlockSpec((SGU_W, tn), lambda i, s: (0, gate_col(s))),
            pl.BlockSpec((n_out, D_MODEL, tno), lambda i, s: (0, 0, 0)),
            pl.BlockSpec((tm, tno), lambda i, s: (i, out_col(s))),
        ],
        out_specs=pl.BlockSpec((tm, tno), lambda i, s: (i, out_col(s))),
        out_shape=jax.ShapeDtypeStruct((T, D_MODEL), F32),
        scratch_shapes=[pltpu.VMEM((nj, tm, tn), BF16)],
        compiler_params=_params("parallel", "arbitrary"),
        name="merge",
    )(xn, a, m, w_gate, w_gate, w_au, w_su, w_out, x)


def _ffn_kernel(h_ref, hs_ref, g2_ref, w1_ref, w2_ref, y_ref, ys_ref, hn_scr):
    tm = h_ref.shape[0]

    @pl.when(pl.program_id(1) == 0)
    def _():
        h = h_ref[...]
        hn_scr[:tm, :] = _rms_rows(h, g2_ref[...]).astype(BF16)
        y_ref[...] = h
        hs = hs_ref[...]
        hn_scr[tm:, :] = _rms_rows(hs, g2_ref[...]).astype(BF16)
        ys_ref[...] = hs

    half = tm // 2
    w1 = w1_ref[...].astype(BF16)
    w2 = w2_ref[...].astype(BF16)
    acts = []
    for rows in (slice(0, half), slice(half, hn_scr.shape[0])):
        a = jnp.maximum(_dot(hn_scr[rows, :], w1), 0.0)
        acts.append((a * a).astype(BF16))
    y_ref[:half, :] += _dot(acts[0], w2)
    second = _dot(acts[1], w2)
    y_ref[half:, :] += second[:half]
    ys_ref[...] += second[half:]


def _ffn_call(l, h, hs, g2, w1, w2, tm, tf):
    T = h.shape[0]
    n_tiles = T // tm
    ts = hs.shape[0] // n_tiles
    assert hs.shape[0] == ts * n_tiles and ts % 16 == 0
    row = lambda i, f: (i, 0)
    return pl.pallas_call(
        _ffn_kernel,
        grid=(n_tiles, D_FF // tf),
        in_specs=[
            pl.BlockSpec((tm, D_MODEL), row),
            pl.BlockSpec((ts, D_MODEL), row),
            pl.BlockSpec((None, 1, D_MODEL), lambda i, f: (l, 0, 0)),
            pl.BlockSpec((None, D_MODEL, tf), lambda i, f: (l, 0, f)),
            pl.BlockSpec((None, tf, D_MODEL), lambda i, f: (l, f, 0)),
        ],
        out_specs=[pl.BlockSpec((tm, D_MODEL), row), pl.BlockSpec((ts, D_MODEL), row)],
        out_shape=[jax.ShapeDtypeStruct((T, D_MODEL), F32), jax.ShapeDtypeStruct(hs.shape, F32)],
        scratch_shapes=[pltpu.VMEM((tm + ts, D_MODEL), BF16)],
        compiler_params=_params("parallel", "arbitrary"),
        name="ffn",
    )(h, hs, g2, w1, w2)


def _rope_tables(pos):
    half = HEAD_DIM // 2
    inv = jnp.power(jnp.float32(ROPE_THETA), -jnp.arange(half, dtype=jnp.float32) / half)
    ang = pos.astype(jnp.float32)[:, None] * inv[None, :]
    cos = jnp.cos(ang)
    sin = jnp.sin(ang)
    return jnp.tile(cos, (1, 2 * N_KV_HEADS)), jnp.tile(jnp.concatenate([-sin, sin], axis=-1), (1, N_KV_HEADS))


def kernel(x_prompt, x_sample, cache_k, cache_v, norm1_g, w_in, q_norm_g, k_norm_g, attn_sinks,
           sgu_ln_g, sgu_ln_b, sgu_w, sgu_b, w_attn_up, w_sgu_up, w_out, norm2_g, w_ff1, w_ff2):
    batch, seq, _ = x_prompt.shape
    dec_batch, dec_seq, _ = x_sample.shape
    tp = batch * seq
    ts = dec_batch * dec_seq
    tm = 512
    tn = 512
    tq = 512
    assert seq % tm == 0 and tm % SGU_CHUNK == 0 and seq % tq == 0 and tq % WINDOW == 0

    cos_p, sin_p = _rope_tables(jnp.arange(seq))
    cos_s, sin_s = _rope_tables(jnp.tile(PAST_LEN + jnp.arange(dec_seq), dec_batch))

    g1 = norm1_g.reshape(DEPTH, 1, D_MODEL)
    g2 = norm2_g.reshape(DEPTH, 1, D_MODEL)
    qg = jnp.tile(q_norm_g, (1, N_KV_HEADS)).reshape(DEPTH, 1, KV_W)
    kg = jnp.tile(k_norm_g, (1, N_KV_HEADS)).reshape(DEPTH, 1, KV_W)
    lng = sgu_ln_g.reshape(DEPTH, 1, SGU_W)
    lnb = sgu_ln_b.reshape(DEPTH, 1, SGU_W)
    bs_t = jnp.swapaxes(sgu_b, 1, 2)
    ws_s = sgu_w[:, :, :dec_seq, :dec_seq].reshape(DEPTH, -1)
    bs_s = sgu_b[:, :, :dec_seq].reshape(DEPTH, -1)
    ck = cache_k.reshape(DEPTH, dec_batch * WINDOW, KV_W)
    cv = cache_v.reshape(DEPTH, dec_batch * WINDOW, KV_W)

    xp = x_prompt.reshape(tp, D_MODEL)
    xs = x_sample.reshape(ts, D_MODEL)
    kp_l, vp_l, ks_l, vs_l, sv_l = [], [], [], [], []
    for l in range(DEPTH):
        xn, q, k, v, xns, qs, ks, vs, w_au, w_su = _qkv_call(
            l, xp, xs, g1, w_in, cos_p, sin_p, cos_s, sin_s, qg, kg, w_attn_up, w_sgu_up, tm, seq // tm)
        m, ms, vsn, w_o = _sgu_call(l, xn, xns, w_in, lng, lnb, sgu_w, bs_t, ws_s, bs_s, w_out, tm, dec_seq)
        a, w_gate = _attn_prompt_call(l, attn_sinks, q, k, v, w_in, batch, seq, tq)
        hp = _merge_call(xp, xn, a, m, w_gate, w_au, w_su, w_o, 2 * tm, tn // 2)
        a, nk, nv = _attn_sample_call(l, attn_sinks, qs, ks, vs, ck, cv, dec_seq, 8)
        hs = _merge_call(xs, xns, a, ms, w_gate, w_au, w_su, w_o, ts, tn)
        xp, xs = _ffn_call(l, hp, hs, g2, w_ff1, w_ff2, 2 * tm, tn)
        kp_l.append(k.reshape(batch, seq, KV_W)[:, -WINDOW:].reshape(batch, WINDOW, N_KV_HEADS, HEAD_DIM))
        vp_l.append(v.reshape(batch, seq, KV_W)[:, -WINDOW:].reshape(batch, WINDOW, N_KV_HEADS, HEAD_DIM))
        ks_l.append(nk.reshape(dec_batch, WINDOW, N_KV_HEADS, HEAD_DIM))
        vs_l.append(nv.reshape(dec_batch, WINDOW, N_KV_HEADS, HEAD_DIM))
        sv_l.append(vsn.reshape(dec_batch, dec_seq, SGU_GROUPS, SGU_CH))

    return (xp.reshape(batch, seq, D_MODEL), xs.reshape(dec_batch, dec_seq, D_MODEL),
            jnp.stack(kp_l), jnp.stack(vp_l), jnp.stack(ks_l), jnp.stack(vs_l), jnp.stack(sv_l))
```

```python
import functools
import math

import jax
import jax.numpy as jnp
import numpy as np
from jax import lax
from jax.experimental import pallas as pl
from jax.experimental.pallas import tpu as pltpu

D_MODEL = 2048
DEPTH = 4
PAST_LEN = 16384
HEAD_DIM = 64
N_HEADS = 16
N_KV_HEADS = 4
Q_GROUP = 4
ATTN_W = N_HEADS * HEAD_DIM
KV_W = N_KV_HEADS * HEAD_DIM
WINDOW = 128
ROPE_THETA = 10000.0
SGU_CHUNK = 128
SGU_W = D_MODEL // 2
SGU_GROUPS = 8
SGU_CH = SGU_W // SGU_GROUPS
D_FF = 4 * D_MODEL
RMS_EPS = 1e-6
LN_EPS = 1e-5
NEG_INF = -1e30

QKV_W = ATTN_W + 2 * KV_W
UV_OFF = QKV_W
GATE_OFF = QKV_W + 2 * SGU_W

LANES = 128
VMEM_LIMIT = 58 * 1024 * 1024

BF16 = jnp.bfloat16
F32 = jnp.float32


def _dot(a, b):
    return jnp.dot(a, b, preferred_element_type=F32)


def _rms_rows(x, g):
    ms = jnp.mean(x * x, axis=-1, keepdims=True)
    return x * lax.rsqrt(ms + RMS_EPS) * g


def _gelu(x):
    return 0.5 * x * (1.0 + lax.erf(x * np.float32(math.sqrt(0.5))))


def _params(*sem):
    return pltpu.CompilerParams(dimension_semantics=sem, vmem_limit_bytes=VMEM_LIMIT)


def _head_norm_rope(z, gain, cos, sin, gmat, first_half):
    width = z.shape[1]
    sq = z * z
    hi = sq.astype(BF16)
    lo = (sq - hi.astype(F32)).astype(BF16)
    ssum = _dot(jnp.concatenate([hi, lo], axis=1), gmat)
    zn = z * lax.rsqrt(ssum * (1.0 / HEAD_DIM) + RMS_EPS) * gain
    rot = jnp.where(first_half, pltpu.roll(zn, width - HEAD_DIM // 2, 1), pltpu.roll(zn, HEAD_DIM // 2, 1))
    return zn * cos + rot * sin


def _qkv_kernel(x_ref, xs_ref, g1_ref, w_ref, cos_ref, sin_ref, coss_ref, sins_ref, qg_ref, kg_ref,
                wau_ref, wsu_ref,
                xn_ref, q_ref, k_ref, v_ref, xns_ref, qs_ref, ks_ref, vs_ref, wau_bf_ref, wsu_bf_ref):
    wau_bf_ref[...] = wau_ref[...].astype(BF16)
    wsu_bf_ref[...] = wsu_ref[...].astype(BF16)
    tm = x_ref.shape[0]
    rows = tm + xs_ref.shape[0]
    cw = KV_W
    g1 = g1_ref[...]
    xn = jnp.concatenate([_rms_rows(x_ref[...], g1).astype(BF16), _rms_rows(xs_ref[...], g1).astype(BF16)],
                         axis=0)
    xn_ref[...] = xn[:tm]
    xns_ref[...] = xn[tm:]
    lane = lax.broadcasted_iota(jnp.int32, (rows, cw), 1)
    first_half = (lane % HEAD_DIM) < (HEAD_DIM // 2)
    r = (lax.broadcasted_iota(jnp.int32, (2 * cw, cw), 0) % cw) // HEAD_DIM
    c = lax.broadcasted_iota(jnp.int32, (2 * cw, cw), 1) // HEAD_DIM
    gmat = jnp.where(r == c, 1.0, 0.0).astype(BF16)
    cos = jnp.concatenate([cos_ref[...], coss_ref[...]], axis=0)
    sin = jnp.concatenate([sin_ref[...], sins_ref[...]], axis=0)
    scale = HEAD_DIM ** -0.5
    n_q = ATTN_W // cw

    def project(c):
        return _dot(xn, w_ref[:, c * cw:(c + 1) * cw].astype(BF16))

    z = project(0)
    for c in range(n_q + 1):
        z_next = project(c + 1)
        if c < n_q:
            q = (_head_norm_rope(z, qg_ref[...], cos, sin, gmat, first_half) * scale).astype(BF16)
            q_ref[:, c * cw:(c + 1) * cw] = q[:tm]
            qs_ref[:, c * cw:(c + 1) * cw] = q[tm:]
        else:
            k = _head_norm_rope(z, kg_ref[...], cos, sin, gmat, first_half)
            k_ref[...] = k[:tm]
            ks_ref[...] = k[tm:]
        z = z_next
    v_ref[...] = z[:tm]
    vs_ref[...] = z[tm:]


def _qkv_call(l, x, xs, g1, w_in, cos_t, sin_t, cos_s, sin_s, qg, kg, w_attn_up, w_sgu_up, tm, n_pos_blocks):
    T = x.shape[0]
    n_tiles = T // tm
    Ts = xs.shape[0]
    ts = Ts // n_tiles
    assert Ts == ts * n_tiles and ts % 16 == 0
    assert ATTN_W % (16 * n_tiles) == 0 and SGU_W % (16 * n_tiles) == 0
    row = lambda i: (i, 0)
    layer = lambda i: (l, 0, 0)
    pos = lambda i: (i % n_pos_blocks, 0)
    return pl.pallas_call(
        _qkv_kernel,
        grid=(n_tiles,),
        in_specs=[
            pl.BlockSpec((tm, D_MODEL), row),
            pl.BlockSpec((ts, D_MODEL), row),
            pl.BlockSpec((None, 1, D_MODEL), layer),
            pl.BlockSpec((None, D_MODEL, QKV_W), layer),
            pl.BlockSpec((tm, KV_W), pos),
            pl.BlockSpec((tm, KV_W), pos),
            pl.BlockSpec((ts, KV_W), row),
            pl.BlockSpec((ts, KV_W), row),
            pl.BlockSpec((None, 1, KV_W), layer),
            pl.BlockSpec((None, 1, KV_W), layer),
            pl.BlockSpec((None, ATTN_W // n_tiles, D_MODEL), lambda i: (l, i, 0)),
            pl.BlockSpec((None, SGU_W // n_tiles, D_MODEL), lambda i: (l, i, 0)),
        ],
        out_specs=[
            pl.BlockSpec((tm, D_MODEL), row),
            pl.BlockSpec((tm, ATTN_W), row),
            pl.BlockSpec((tm, KV_W), row),
            pl.BlockSpec((tm, KV_W), row),
            pl.BlockSpec((ts, D_MODEL), row),
            pl.BlockSpec((ts, ATTN_W), row),
            pl.BlockSpec((ts, KV_W), row),
            pl.BlockSpec((ts, KV_W), row),
            pl.BlockSpec((ATTN_W // n_tiles, D_MODEL), row),
            pl.BlockSpec((SGU_W // n_tiles, D_MODEL), row),
        ],
        out_shape=[
            jax.ShapeDtypeStruct((T, D_MODEL), BF16),
            jax.ShapeDtypeStruct((T, ATTN_W), BF16),
            jax.ShapeDtypeStruct((T, KV_W), F32),
            jax.ShapeDtypeStruct((T, KV_W), F32),
            jax.ShapeDtypeStruct((Ts, D_MODEL), BF16),
            jax.ShapeDtypeStruct((Ts, ATTN_W), BF16),
            jax.ShapeDtypeStruct((Ts, KV_W), F32),
            jax.ShapeDtypeStruct((Ts, KV_W), F32),
            jax.ShapeDtypeStruct((ATTN_W, D_MODEL), BF16),
            jax.ShapeDtypeStruct((SGU_W, D_MODEL), BF16),
        ],
        compiler_params=_params("parallel"),
        name="qkv",
    )(x, xs, g1, w_in, cos_t, sin_t, cos_s, sin_s, qg, kg, w_attn_up, w_sgu_up)


UV_BLOCK = 512
UV_BLOCKS = 2 * SGU_W // UV_BLOCK


def _uv_project(xn, w_refs, lng_ref, lnb_ref, u_scr, vs_scr):
    cw = 2 * LANES
    per_block = UV_BLOCK // cw
    n_u = SGU_W // cw
    for c in list(range(n_u, 2 * n_u)) + list(range(n_u)):
        w = w_refs[c // per_block][:, (c % per_block) * cw:(c % per_block + 1) * cw].astype(BF16)
        dst = u_scr if c < n_u else vs_scr
        c_dst = c % n_u
        dst[:, c_dst * cw:(c_dst + 1) * cw] = _gelu(_dot(xn, w))
    vs = vs_scr[...]
    mu = jnp.mean(vs, axis=-1, keepdims=True)
    d = vs - mu
    var = jnp.mean(d * d, axis=-1, keepdims=True)
    vs_scr[...] = d * lax.rsqrt(var + LN_EPS) * lng_ref[...] + lnb_ref[...]


def _sgu_kernel(wss_ref, bss_ref, xn_ref, xns_ref, *refs, l, seq):
    w_refs = refs[:UV_BLOCKS]
    (lng_ref, lnb_ref, ws_ref, bs_ref, wout_ref,
     m_ref, ms_ref, vsn_ref, wout_bf_ref, u_scr, vs_scr, g_scr) = refs[UV_BLOCKS:]
    wout_bf_ref[...] = wout_ref[...].astype(BF16)
    tm = xn_ref.shape[0]
    nseq = xns_ref.shape[0] // seq
    xn = jnp.concatenate([xn_ref[...], xns_ref[...]], axis=0)
    _uv_project(xn, w_refs, lng_ref, lnb_ref, u_scr, vs_scr)

    t = lax.broadcasted_iota(jnp.int32, (SGU_CHUNK, SGU_CHUNK), 0)
    s = lax.broadcasted_iota(jnp.int32, (SGU_CHUNK, SGU_CHUNK), 1)
    causal = t >= s
    for g in range(SGU_GROUPS):
        wg = jnp.where(causal, ws_ref[g], 0.0).astype(BF16)
        bias = bs_ref[:, g:g + 1]
        cols = slice(g * SGU_CH, (g + 1) * SGU_CH)
        for n in range(tm // SGU_CHUNK):
            rows = slice(n * SGU_CHUNK, (n + 1) * SGU_CHUNK)
            mixed = _dot(wg, vs_scr[rows, cols].astype(BF16)) + bias
            m_ref[rows, cols] = (u_scr[rows, cols] * mixed).astype(BF16)

    vsn_ref[...] = vs_scr[tm:, :]
    for g in range(SGU_GROUPS):
        cols = slice(g * SGU_CH, (g + 1) * SGU_CH)
        g_scr[0] = vs_scr[tm:, cols]
        g_scr[1] = u_scr[tm:, cols]
        for t in range(seq):
            step_t = pl.ds(t, nseq, stride=seq)
            acc = jnp.full((nseq, SGU_CH), bss_ref[l, g * seq + t], F32)
            for s in range(t + 1):
                acc = acc + wss_ref[l, (g * seq + t) * seq + s] * g_scr[0, pl.ds(s, nseq, stride=seq), :]
            g_scr[2, step_t, :] = g_scr[1, step_t, :] * acc
        ms_ref[:, cols] = g_scr[2].astype(BF16)


def _uv_weight_specs(l):
    return [pl.BlockSpec((None, D_MODEL, UV_BLOCK), functools.partial(lambda b, i: (l, 0, b), UV_OFF // UV_BLOCK + k))
            for k in range(UV_BLOCKS)]


def _sgu_call(l, xn, xns, w_in, lng, lnb, ws, bs_t, ws_sample, bs_sample, w_out, tm, seq):
    T = xn.shape[0]
    n_tiles = T // tm
    Ts = xns.shape[0]
    ts = Ts // n_tiles
    assert Ts == ts * n_tiles and ts % 16 == 0 and ts % seq == 0 and D_MODEL % (16 * n_tiles) == 0
    slab = D_MODEL // n_tiles
    row = lambda i: (i, 0)
    layer = lambda i: (l, 0, 0)
    smem = pl.BlockSpec(memory_space=pltpu.SMEM)
    return pl.pallas_call(
        functools.partial(_sgu_kernel, l=l, seq=seq),
        grid=(n_tiles,),
        in_specs=[
            smem,
            smem,
            pl.BlockSpec((tm, D_MODEL), row),
            pl.BlockSpec((ts, D_MODEL), row),
            *_uv_weight_specs(l),
            pl.BlockSpec((None, 1, SGU_W), layer),
            pl.BlockSpec((None, 1, SGU_W), layer),
            pl.BlockSpec((None, SGU_GROUPS, SGU_CHUNK, SGU_CHUNK), lambda i: (l, 0, 0, 0)),
            pl.BlockSpec((None, SGU_CHUNK, SGU_GROUPS), layer),
            pl.BlockSpec((None, slab, D_MODEL), lambda i: (l, i, 0)),
        ],
        out_specs=[pl.BlockSpec((tm, SGU_W), row), pl.BlockSpec((ts, SGU_W), row),
                   pl.BlockSpec((ts, SGU_W), row), pl.BlockSpec((slab, D_MODEL), row)],
        out_shape=[jax.ShapeDtypeStruct((T, SGU_W), BF16), jax.ShapeDtypeStruct((Ts, SGU_W), BF16),
                   jax.ShapeDtypeStruct((Ts, SGU_W), F32), jax.ShapeDtypeStruct((D_MODEL, D_MODEL), BF16)],
        scratch_shapes=[pltpu.VMEM((tm + ts, SGU_W), F32), pltpu.VMEM((tm + ts, SGU_W), F32),
                        pltpu.VMEM((3, ts, SGU_CH), F32)],
        compiler_params=_params("parallel"),
        name="sgu",
    )(ws_sample, bs_sample, xn, xns, *([w_in] * UV_BLOCKS), lng, lnb, ws, bs_t, w_out)


def _attn_prompt_kernel(sink_ref, q_ref, kp_ref, kc_ref, vp_ref, vc_ref, wg_ref, o_ref, wg_bf_ref, *, l):
    wg_bf_ref[...] = wg_ref[...].astype(BF16)
    blk = WINDOW
    nq = Q_GROUP * blk
    first_tile = pl.program_id(1) == 0
    c = lax.broadcasted_iota(jnp.int32, (blk, nq), 0)
    r = lax.broadcasted_iota(jnp.int32, (blk, nq), 1) % blk
    from_prev = c > r
    k_all = jnp.concatenate([kp_ref[...], kc_ref[...]], axis=0).astype(BF16)
    vt_all = jnp.concatenate([vp_ref[...], vc_ref[...]], axis=0).T.astype(BF16)
    sinks = [jnp.concatenate([jnp.full((1, blk), sink_ref[l, g * Q_GROUP + j], F32) for j in range(Q_GROUP)],
                             axis=1) for g in range(N_KV_HEADS)]

    def scores(jb):
        out = []
        for g in range(N_KV_HEADS):
            qg = jnp.concatenate([q_ref[jb * blk:(jb + 1) * blk, h * HEAD_DIM:(h + 1) * HEAD_DIM]
                                  for h in range(g * Q_GROUP, (g + 1) * Q_GROUP)], axis=0)
            kg = k_all[jb * blk:(jb + 2) * blk, g * HEAD_DIM:(g + 1) * HEAD_DIM]
            out.append(lax.dot_general(kg, qg, (((1,), (1,)), ((), ())), preferred_element_type=F32))
        return out

    nblk = q_ref.shape[0] // blk
    s2 = scores(0)
    for jb in range(nblk):
        s2_next = scores(jb + 1) if jb + 1 < nblk else None
        for g in range(N_KV_HEADS):
            s_prev = s2[g][:blk]
            if jb == 0:
                s_prev = jnp.where(first_tile, NEG_INF, s_prev)
            s = jnp.where(from_prev, s_prev, s2[g][blk:])
            m = jnp.maximum(jnp.max(s, axis=0, keepdims=True), sinks[g])
            p = jnp.exp(s - m)
            p = p / (jnp.sum(p, axis=0, keepdims=True) + jnp.exp(sinks[g] - m))
            p2 = jnp.concatenate([jnp.where(from_prev, p, 0.0), jnp.where(from_prev, 0.0, p)], axis=0)
            vt = vt_all[g * HEAD_DIM:(g + 1) * HEAD_DIM, jb * blk:(jb + 2) * blk]
            o = _dot(vt, p2.astype(BF16)).T
            for j in range(Q_GROUP):
                h = g * Q_GROUP + j
                o_ref[jb * blk:(jb + 1) * blk, h * HEAD_DIM:(h + 1) * HEAD_DIM] = (
                    o[j * blk:(j + 1) * blk].astype(BF16))
        s2 = s2_next


def _attn_prompt_call(l, sinks, q, k, v, w_in, batch, seq, tq):
    nt = seq // tq
    blocks_per_tile = tq // WINDOW
    gate_w = 2 * D_MODEL // (batch * nt)
    assert gate_w % LANES == 0 and GATE_OFF % gate_w == 0
    cur = lambda b, i: (b * nt + i, 0)
    prev = lambda b, i: (b * (seq // WINDOW) + jnp.maximum(i * blocks_per_tile - 1, 0), 0)
    return pl.pallas_call(
        functools.partial(_attn_prompt_kernel, l=l),
        grid=(batch, nt),
        in_specs=[
            pl.BlockSpec(memory_space=pltpu.SMEM),
            pl.BlockSpec((tq, ATTN_W), cur),
            pl.BlockSpec((WINDOW, KV_W), prev),
            pl.BlockSpec((tq, KV_W), cur),
            pl.BlockSpec((WINDOW, KV_W), prev),
            pl.BlockSpec((tq, KV_W), cur),
            pl.BlockSpec((None, D_MODEL, gate_w), lambda b, i: (l, 0, GATE_OFF // gate_w + b * nt + i)),
        ],
        out_specs=[pl.BlockSpec((tq, ATTN_W), cur),
                   pl.BlockSpec((D_MODEL, gate_w), lambda b, i: (0, b * nt + i))],
        out_shape=[jax.ShapeDtypeStruct((batch * seq, ATTN_W), BF16),
                   jax.ShapeDtypeStruct((D_MODEL, 2 * D_MODEL), BF16)],
        compiler_params=_params("parallel", "parallel"),
        name="attn_prompt",
    )(sinks, q, k, k, v, v, w_in)


def _attn_sample_kernel(sink_ref, q_ref, kn_ref, vn_ref, ck_ref, cv_ref, o_ref, nk_ref, nv_ref, *, l, seq):
    nseq = q_ref.shape[0] // seq
    nq = nseq * seq
    rows = Q_GROUP * nq
    ri = lax.broadcasted_iota(jnp.int32, (rows, nseq * WINDOW), 0) % nq
    ci = lax.broadcasted_iota(jnp.int32, (rows, nseq * WINDOW), 1)
    mask_c = (ri // seq == ci // WINDOW) & (ci % WINDOW > ri % seq)
    ri = lax.broadcasted_iota(jnp.int32, (rows, nq), 0) % nq
    ci = lax.broadcasted_iota(jnp.int32, (rows, nq), 1)
    mask_n = (ri // seq == ci // seq) & (ci % seq <= ri % seq)
    dn = (((1,), (1,)), ((), ()))
    scores = []
    for g in range(N_KV_HEADS):
        cols = slice(g * HEAD_DIM, (g + 1) * HEAD_DIM)
        qg = jnp.concatenate([q_ref[:, h * HEAD_DIM:(h + 1) * HEAD_DIM]
                              for h in range(g * Q_GROUP, (g + 1) * Q_GROUP)], axis=0)
        scores.append((lax.dot_general(qg, ck_ref[:, cols].astype(BF16), dn, preferred_element_type=F32),
                       lax.dot_general(qg, kn_ref[:, cols].astype(BF16), dn, preferred_element_type=F32)))
    for g in range(N_KV_HEADS):
        cols = slice(g * HEAD_DIM, (g + 1) * HEAD_DIM)
        heads = range(g * Q_GROUP, (g + 1) * Q_GROUP)
        sink = jnp.concatenate([jnp.full((nq, 1), sink_ref[l, h], F32) for h in heads], axis=0)
        s_c = jnp.where(mask_c, scores[g][0], NEG_INF)
        s_n = jnp.where(mask_n, scores[g][1], NEG_INF)
        m = jnp.maximum(jnp.maximum(jnp.max(s_c, axis=-1, keepdims=True),
                                    jnp.max(s_n, axis=-1, keepdims=True)), sink)
        p_c = jnp.exp(s_c - m)
        p_n = jnp.exp(s_n - m)
        denom = (jnp.sum(p_c, axis=-1, keepdims=True) + jnp.sum(p_n, axis=-1, keepdims=True)
                 + jnp.exp(sink - m))
        o = (_dot((p_c / denom).astype(BF16), cv_ref[:, cols].astype(BF16))
             + _dot((p_n / denom).astype(BF16), vn_ref[:, cols].astype(BF16)))
        for j, h in enumerate(heads):
            o_ref[:, h * HEAD_DIM:(h + 1) * HEAD_DIM] = o[j * nq:(j + 1) * nq].astype(BF16)
    for b in range(nseq):
        nk_ref[b * WINDOW:(b + 1) * WINDOW - seq, :] = ck_ref[b * WINDOW + seq:(b + 1) * WINDOW, :]
        nv_ref[b * WINDOW:(b + 1) * WINDOW - seq, :] = cv_ref[b * WINDOW + seq:(b + 1) * WINDOW, :]
        nk_ref[(b + 1) * WINDOW - seq:(b + 1) * WINDOW, :] = kn_ref[b * seq:(b + 1) * seq, :]
        nv_ref[(b + 1) * WINDOW - seq:(b + 1) * WINDOW, :] = vn_ref[b * seq:(b + 1) * seq, :]


def _attn_sample_call(l, sinks, q, k, v, ck, cv, seq, seq_block):
    T = q.shape[0]
    nseq = T // seq
    row = lambda i: (i, 0)
    cache = lambda i: (l, i, 0)
    return pl.pallas_call(
        functools.partial(_attn_sample_kernel, l=l, seq=seq),
        grid=(nseq // seq_block,),
        in_specs=[
            pl.BlockSpec(memory_space=pltpu.SMEM),
            pl.BlockSpec((seq_block * seq, ATTN_W), row),
            pl.BlockSpec((seq_block * seq, KV_W), row),
            pl.BlockSpec((seq_block * seq, KV_W), row),
            pl.BlockSpec((None, seq_block * WINDOW, KV_W), cache),
            pl.BlockSpec((None, seq_block * WINDOW, KV_W), cache),
        ],
        out_specs=[
            pl.BlockSpec((seq_block * seq, ATTN_W), row),
            pl.BlockSpec((seq_block * WINDOW, KV_W), row),
            pl.BlockSpec((seq_block * WINDOW, KV_W), row),
        ],
        out_shape=[
            jax.ShapeDtypeStruct((T, ATTN_W), BF16),
            jax.ShapeDtypeStruct((nseq * WINDOW, KV_W), F32),
            jax.ShapeDtypeStruct((nseq * WINDOW, KV_W), F32),
        ],
        compiler_params=_params("parallel"),
        name="attn_sample",
    )(sinks, q, k, v, ck, cv)


def _merge_kernel(x_ref, xn_ref, a_ref, m_ref, wga_ref, wgm_ref, wau_ref, wsu_ref, wout_ref,
                  h_ref, merged_scr):
    j = pl.program_id(1)
    xn = xn_ref[...]
    ga = jax.nn.sigmoid(_dot(xn, wga_ref[...]))
    gm = jax.nn.sigmoid(_dot(xn, wgm_ref[...]))
    merged = ga * _dot(a_ref[...], wau_ref[...]) + gm * _dot(m_ref[...], wsu_ref[...])
    merged_scr[j] = merged.astype(BF16)

    @pl.when(j == pl.num_programs(1) - 1)
    def _():
        tn = merged_scr.shape[2]
        acc = x_ref[...]
        for c in range(merged_scr.shape[0]):
            acc = acc + _dot(merged_scr[c], wout_ref[c * tn:(c + 1) * tn, :])
        h_ref[...] = acc


def _merge_call(x, xn, a, m, w_gate, w_au, w_su, w_out, tm, tn):
    T = x.shape[0]
    nj = D_MODEL // tn
    row = lambda i, j: (i, 0)
    col = lambda i, j: (0, j)
    return pl.pallas_call(
        _merge_kernel,
        grid=(T // tm, nj),
        in_specs=[
            pl.BlockSpec((tm, D_MODEL), row),
            pl.BlockSpec((tm, D_MODEL), row),
            pl.BlockSpec((tm, ATTN_W), row),
            pl.BlockSpec((tm, SGU_W), row),
            pl.BlockSpec((D_MODEL, tn), col),
            pl.BlockSpec((D_MODEL, tn), lambda i, j: (0, nj + j)),
            pl.BlockSpec((ATTN_W, tn), col),
            pl.BlockSpec((SGU_W, tn), col),
            pl.BlockSpec((D_MODEL, D_MODEL), lambda i, j: (0, 0)),
        ],
        out_specs=pl.BlockSpec((tm, D_MODEL), row),
        out_shape=jax.ShapeDtypeStruct((T, D_MODEL), F32),
        scratch_shapes=[pltpu.VMEM((nj, tm, tn), BF16)],
        compiler_params=_params("parallel", "arbitrary"),
        name="merge",
    )(x, xn, a, m, w_gate, w_gate, w_au, w_su, w_out)


def _ffn_kernel(h_ref, hs_ref, g2_ref, w1_ref, w2_ref, y_ref, ys_ref, hn_scr):
    tm = h_ref.shape[0]

    @pl.when(pl.program_id(1) == 0)
    def _():
        h = h_ref[...]
        hn_scr[:tm, :] = _rms_rows(h, g2_ref[...]).astype(BF16)
        y_ref[...] = h
        hs = hs_ref[...]
        hn_scr[tm:, :] = _rms_rows(hs, g2_ref[...]).astype(BF16)
        ys_ref[...] = hs

    half = tm // 2
    w1 = w1_ref[...].astype(BF16)
    w2 = w2_ref[...].astype(BF16)
    acts = []
    for rows in (slice(0, half), slice(half, hn_scr.shape[0])):
        a = jnp.maximum(_dot(hn_scr[rows, :], w1), 0.0)
        acts.append((a * a).astype(BF16))
    y_ref[:half, :] += _dot(acts[0], w2)
    second = _dot(acts[1], w2)
    y_ref[half:, :] += second[:half]
    ys_ref[...] += second[half:]


def _ffn_call(l, h, hs, g2, w1, w2, tm, tf):
    T = h.shape[0]
    n_tiles = T // tm
    ts = hs.shape[0] // n_tiles
    assert hs.shape[0] == ts * n_tiles and ts % 16 == 0
    row = lambda i, f: (i, 0)
    return pl.pallas_call(
        _ffn_kernel,
        grid=(n_tiles, D_FF // tf),
        in_specs=[
            pl.BlockSpec((tm, D_MODEL), row),
            pl.BlockSpec((ts, D_MODEL), row),
            pl.BlockSpec((None, 1, D_MODEL), lambda i, f: (l, 0, 0)),
            pl.BlockSpec((None, D_MODEL, tf), lambda i, f: (l, 0, f)),
            pl.BlockSpec((None, tf, D_MODEL), lambda i, f: (l, f, 0)),
        ],
        out_specs=[pl.BlockSpec((tm, D_MODEL), row), pl.BlockSpec((ts, D_MODEL), row)],
        out_shape=[jax.ShapeDtypeStruct((T, D_MODEL), F32), jax.ShapeDtypeStruct(hs.shape, F32)],
        scratch_shapes=[pltpu.VMEM((tm + ts, D_MODEL), BF16)],
        compiler_params=_params("parallel", "arbitrary"),
        name="ffn",
    )(h, hs, g2, w1, w2)


def _rope_tables(pos):
    half = HEAD_DIM // 2
    inv = jnp.power(jnp.float32(ROPE_THETA), -jnp.arange(half, dtype=jnp.float32) / half)
    ang = pos.astype(jnp.float32)[:, None] * inv[None, :]
    cos = jnp.cos(ang)
    sin = jnp.sin(ang)
    return jnp.tile(cos, (1, 2 * N_KV_HEADS)), jnp.tile(jnp.concatenate([-sin, sin], axis=-1), (1, N_KV_HEADS))


def kernel(x_prompt, x_sample, cache_k, cache_v, norm1_g, w_in, q_norm_g, k_norm_g, attn_sinks,
           sgu_ln_g, sgu_ln_b, sgu_w, sgu_b, w_attn_up, w_sgu_up, w_out, norm2_g, w_ff1, w_ff2):
    batch, seq, _ = x_prompt.shape
    dec_batch, dec_seq, _ = x_sample.shape
    tp = batch * seq
    ts = dec_batch * dec_seq
    tm = 512
    tn = 512
    tq = 512
    assert seq % tm == 0 and tm % SGU_CHUNK == 0 and seq % tq == 0 and tq % WINDOW == 0

    cos_p, sin_p = _rope_tables(jnp.arange(seq))
    cos_s, sin_s = _rope_tables(jnp.tile(PAST_LEN + jnp.arange(dec_seq), dec_batch))

    g1 = norm1_g.reshape(DEPTH, 1, D_MODEL)
    g2 = norm2_g.reshape(DEPTH, 1, D_MODEL)
    qg = jnp.tile(q_norm_g, (1, N_KV_HEADS)).reshape(DEPTH, 1, KV_W)
    kg = jnp.tile(k_norm_g, (1, N_KV_HEADS)).reshape(DEPTH, 1, KV_W)
    lng = sgu_ln_g.reshape(DEPTH, 1, SGU_W)
    lnb = sgu_ln_b.reshape(DEPTH, 1, SGU_W)
    bs_t = jnp.swapaxes(sgu_b, 1, 2)
    ws_s = sgu_w[:, :, :dec_seq, :dec_seq].reshape(DEPTH, -1)
    bs_s = sgu_b[:, :, :dec_seq].reshape(DEPTH, -1)
    ck = cache_k.reshape(DEPTH, dec_batch * WINDOW, KV_W)
    cv = cache_v.reshape(DEPTH, dec_batch * WINDOW, KV_W)

    xp = x_prompt.reshape(tp, D_MODEL)
    xs = x_sample.reshape(ts, D_MODEL)
    kp_l, vp_l, ks_l, vs_l, sv_l = [], [], [], [], []
    for l in range(DEPTH):
        xn, q, k, v, xns, qs, ks, vs, w_au, w_su = _qkv_call(
            l, xp, xs, g1, w_in, cos_p, sin_p, cos_s, sin_s, qg, kg, w_attn_up, w_sgu_up, tm, seq // tm)
        m, ms, vsn, w_o = _sgu_call(l, xn, xns, w_in, lng, lnb, sgu_w, bs_t, ws_s, bs_s, w_out, tm, dec_seq)
        a, w_gate = _attn_prompt_call(l, attn_sinks, q, k, v, w_in, batch, seq, tq)
        hp = _merge_call(xp, xn, a, m, w_gate, w_au, w_su, w_o, tm, tn)
        a, nk, nv = _attn_sample_call(l, attn_sinks, qs, ks, vs, ck, cv, dec_seq, 8)
        hs = _merge_call(xs, xns, a, ms, w_gate, w_au, w_su, w_o, ts, tn)
        xp, xs = _ffn_call(l, hp, hs, g2, w_ff1, w_ff2, 2 * tm, tn)
        kp_l.append(k.reshape(batch, seq, KV_W)[:, -WINDOW:].reshape(batch, WINDOW, N_KV_HEADS, HEAD_DIM))
        vp_l.append(v.reshape(batch, seq, KV_W)[:, -WINDOW:].reshape(batch, WINDOW, N_KV_HEADS, HEAD_DIM))
        ks_l.append(nk.reshape(dec_batch, WINDOW, N_KV_HEADS, HEAD_DIM))
        vs_l.append(nv.reshape(dec_batch, WINDOW, N_KV_HEADS, HEAD_DIM))
        sv_l.append(vsn.reshape(dec_batch, dec_seq, SGU_GROUPS, SGU_CH))

    return (xp.reshape(batch, seq, D_MODEL), xs.reshape(dec_batch, dec_seq, D_MODEL),
            jnp.stack(kp_l), jnp.stack(vp_l), jnp.stack(ks_l), jnp.stack(vs_l), jnp.stack(sv_l))
```

```python
import functools
import math

import jax
import jax.numpy as jnp
import numpy as np
from jax import lax
from jax.experimental import pallas as pl
from jax.experimental.pallas import tpu as pltpu

D_MODEL = 2048
DEPTH = 4
PAST_LEN = 16384
HEAD_DIM = 64
N_HEADS = 16
N_KV_HEADS = 4
Q_GROUP = 4
ATTN_W = N_HEADS * HEAD_DIM
KV_W = N_KV_HEADS * HEAD_DIM
WINDOW = 128
ROPE_THETA = 10000.0
SGU_CHUNK = 128
SGU_W = D_MODEL // 2
SGU_GROUPS = 8
SGU_CH = SGU_W // SGU_GROUPS
D_FF = 4 * D_MODEL
RMS_EPS = 1e-6
LN_EPS = 1e-5
NEG_INF = -1e30

QKV_W = ATTN_W + 2 * KV_W
UV_OFF = QKV_W
GATE_OFF = QKV_W + 2 * SGU_W

LANES = 128
VMEM_LIMIT = 58 * 1024 * 1024

BF16 = jnp.bfloat16
F32 = jnp.float32


def _dot(a, b):
    return jnp.dot(a, b, preferred_element_type=F32)


def _rms_rows(x, g):
    ms = jnp.mean(x * x, axis=-1, keepdims=True)
    return x * lax.rsqrt(ms + RMS_EPS) * g


def _gelu(x):
    return 0.5 * x * (1.0 + lax.erf(x * np.float32(math.sqrt(0.5))))


def _params(*sem):
    return pltpu.CompilerParams(dimension_semantics=sem, vmem_limit_bytes=VMEM_LIMIT)


def _head_norm_rope(z, gain, cos, sin, gmat, first_half):
    width = z.shape[1]
    sq = z * z
    hi = sq.astype(BF16)
    lo = (sq - hi.astype(F32)).astype(BF16)
    ssum = _dot(jnp.concatenate([hi, lo], axis=1), gmat)
    zn = z * lax.rsqrt(ssum * (1.0 / HEAD_DIM) + RMS_EPS) * gain
    rot = jnp.where(first_half, pltpu.roll(zn, width - HEAD_DIM // 2, 1), pltpu.roll(zn, HEAD_DIM // 2, 1))
    return zn * cos + rot * sin


def _qkv_kernel(x_ref, xs_ref, g1_ref, w_ref, cos_ref, sin_ref, coss_ref, sins_ref, qg_ref, kg_ref,
                wau_ref, wsu_ref,
                xn_ref, q_ref, k_ref, v_ref, xns_ref, qs_ref, ks_ref, vs_ref, wau_bf_ref, wsu_bf_ref):
    wau_bf_ref[...] = wau_ref[...].astype(BF16)
    wsu_bf_ref[...] = wsu_ref[...].astype(BF16)
    tm = x_ref.shape[0]
    rows = tm + xs_ref.shape[0]
    cw = KV_W
    g1 = g1_ref[...]
    xn = jnp.concatenate([_rms_rows(x_ref[...], g1).astype(BF16), _rms_rows(xs_ref[...], g1).astype(BF16)],
                         axis=0)
    xn_ref[...] = xn[:tm]
    xns_ref[...] = xn[tm:]
    lane = lax.broadcasted_iota(jnp.int32, (rows, cw), 1)
    first_half = (lane % HEAD_DIM) < (HEAD_DIM // 2)
    r = (lax.broadcasted_iota(jnp.int32, (2 * cw, cw), 0) % cw) // HEAD_DIM
    c = lax.broadcasted_iota(jnp.int32, (2 * cw, cw), 1) // HEAD_DIM
    gmat = jnp.where(r == c, 1.0, 0.0).astype(BF16)
    cos = jnp.concatenate([cos_ref[...], coss_ref[...]], axis=0)
    sin = jnp.concatenate([sin_ref[...], sins_ref[...]], axis=0)
    scale = HEAD_DIM ** -0.5
    n_q = ATTN_W // cw

    def project(c):
        return _dot(xn, w_ref[:, c * cw:(c + 1) * cw].astype(BF16))

    z = project(0)
    for c in range(n_q + 1):
        z_next = project(c + 1)
        if c < n_q:
            q = (_head_norm_rope(z, qg_ref[...], cos, sin, gmat, first_half) * scale).astype(BF16)
            q_ref[:, c * cw:(c + 1) * cw] = q[:tm]
            qs_ref[:, c * cw:(c + 1) * cw] = q[tm:]
        else:
            k = _head_norm_rope(z, kg_ref[...], cos, sin, gmat, first_half)
            k_ref[...] = k[:tm]
            ks_ref[...] = k[tm:]
        z = z_next
    v_ref[...] = z[:tm]
    vs_ref[...] = z[tm:]


def _qkv_call(l, x, xs, g1, w_in, cos_t, sin_t, cos_s, sin_s, qg, kg, w_attn_up, w_sgu_up, tm, n_pos_blocks):
    T = x.shape[0]
    n_tiles = T // tm
    Ts = xs.shape[0]
    ts = Ts // n_tiles
    assert Ts == ts * n_tiles and ts % 16 == 0
    assert ATTN_W % (16 * n_tiles) == 0 and SGU_W % (16 * n_tiles) == 0
    row = lambda i: (i, 0)
    layer = lambda i: (l, 0, 0)
    pos = lambda i: (i % n_pos_blocks, 0)
    return pl.pallas_call(
        _qkv_kernel,
        grid=(n_tiles,),
        in_specs=[
            pl.BlockSpec((tm, D_MODEL), row),
            pl.BlockSpec((ts, D_MODEL), row),
            pl.BlockSpec((None, 1, D_MODEL), layer),
            pl.BlockSpec((None, D_MODEL, QKV_W), layer),
            pl.BlockSpec((tm, KV_W), pos),
            pl.BlockSpec((tm, KV_W), pos),
            pl.BlockSpec((ts, KV_W), row),
            pl.BlockSpec((ts, KV_W), row),
            pl.BlockSpec((None, 1, KV_W), layer),
            pl.BlockSpec((None, 1, KV_W), layer),
            pl.BlockSpec((None, ATTN_W // n_tiles, D_MODEL), lambda i: (l, i, 0)),
            pl.BlockSpec((None, SGU_W // n_tiles, D_MODEL), lambda i: (l, i, 0)),
        ],
        out_specs=[
            pl.BlockSpec((tm, D_MODEL), row),
            pl.BlockSpec((tm, ATTN_W), row),
            pl.BlockSpec((tm, KV_W), row),
            pl.BlockSpec((tm, KV_W), row),
            pl.BlockSpec((ts, D_MODEL), row),
            pl.BlockSpec((ts, ATTN_W), row),
            pl.BlockSpec((ts, KV_W), row),
            pl.BlockSpec((ts, KV_W), row),
            pl.BlockSpec((ATTN_W // n_tiles, D_MODEL), row),
            pl.BlockSpec((SGU_W // n_tiles, D_MODEL), row),
        ],
        out_shape=[
            jax.ShapeDtypeStruct((T, D_MODEL), BF16),
            jax.ShapeDtypeStruct((T, ATTN_W), BF16),
            jax.ShapeDtypeStruct((T, KV_W), F32),
            jax.ShapeDtypeStruct((T, KV_W), F32),
            jax.ShapeDtypeStruct((Ts, D_MODEL), BF16),
            jax.ShapeDtypeStruct((Ts, ATTN_W), BF16),
            jax.ShapeDtypeStruct((Ts, KV_W), F32),
            jax.ShapeDtypeStruct((Ts, KV_W), F32),
            jax.ShapeDtypeStruct((ATTN_W, D_MODEL), BF16),
            jax.ShapeDtypeStruct((SGU_W, D_MODEL), BF16),
        ],
        compiler_params=_params("parallel"),
        name="qkv",
    )(x, xs, g1, w_in, cos_t, sin_t, cos_s, sin_s, qg, kg, w_attn_up, w_sgu_up)


UV_BLOCK = 512
UV_BLOCKS = 2 * SGU_W // UV_BLOCK


def _uv_project(xn, w_refs, lng_ref, lnb_ref, u_scr, vs_scr):
    cw = 2 * LANES
    per_block = UV_BLOCK // cw
    n_u = SGU_W // cw
    for c in list(range(n_u, 2 * n_u)) + list(range(n_u)):
        w = w_refs[c // per_block][:, (c % per_block) * cw:(c % per_block + 1) * cw].astype(BF16)
        dst = u_scr if c < n_u else vs_scr
        c_dst = c % n_u
        dst[:, c_dst * cw:(c_dst + 1) * cw] = _gelu(_dot(xn, w))
    vs = vs_scr[...]
    mu = jnp.mean(vs, axis=-1, keepdims=True)
    d = vs - mu
    var = jnp.mean(d * d, axis=-1, keepdims=True)
    vs_scr[...] = d * lax.rsqrt(var + LN_EPS) * lng_ref[...] + lnb_ref[...]


def _sgu_kernel(wss_ref, bss_ref, xn_ref, xns_ref, *refs, l, seq):
    w_refs = refs[:UV_BLOCKS]
    (lng_ref, lnb_ref, ws_ref, bs_ref, wout_ref,
     m_ref, ms_ref, vsn_ref, wout_bf_ref, u_scr, vs_scr, g_scr) = refs[UV_BLOCKS:]
    wout_bf_ref[...] = wout_ref[...].astype(BF16)
    tm = xn_ref.shape[0]
    nseq = xns_ref.shape[0] // seq
    xn = jnp.concatenate([xn_ref[...], xns_ref[...]], axis=0)
    _uv_project(xn, w_refs, lng_ref, lnb_ref, u_scr, vs_scr)

    t = lax.broadcasted_iota(jnp.int32, (SGU_CHUNK, SGU_CHUNK), 0)
    s = lax.broadcasted_iota(jnp.int32, (SGU_CHUNK, SGU_CHUNK), 1)
    causal = t >= s
    for g in range(SGU_GROUPS):
        wg = jnp.where(causal, ws_ref[g], 0.0).astype(BF16)
        bias = bs_ref[:, g:g + 1]
        cols = slice(g * SGU_CH, (g + 1) * SGU_CH)
        for n in range(tm // SGU_CHUNK):
            rows = slice(n * SGU_CHUNK, (n + 1) * SGU_CHUNK)
            mixed = _dot(wg, vs_scr[rows, cols].astype(BF16)) + bias
            m_ref[rows, cols] = (u_scr[rows, cols] * mixed).astype(BF16)

    vsn_ref[...] = vs_scr[tm:, :]
    for g in range(SGU_GROUPS):
        cols = slice(g * SGU_CH, (g + 1) * SGU_CH)
        g_scr[0] = vs_scr[tm:, cols]
        g_scr[1] = u_scr[tm:, cols]
        for t in range(seq):
            step_t = pl.ds(t, nseq, stride=seq)
            acc = jnp.full((nseq, SGU_CH), bss_ref[l, g * seq + t], F32)
            for s in range(t + 1):
                acc = acc + wss_ref[l, (g * seq + t) * seq + s] * g_scr[0, pl.ds(s, nseq, stride=seq), :]
            g_scr[2, step_t, :] = g_scr[1, step_t, :] * acc
        ms_ref[:, cols] = g_scr[2].astype(BF16)


def _uv_weight_specs(l):
    return [pl.BlockSpec((None, D_MODEL, UV_BLOCK), functools.partial(lambda b, i: (l, 0, b), UV_OFF // UV_BLOCK + k))
            for k in range(UV_BLOCKS)]


def _sgu_call(l, xn, xns, w_in, lng, lnb, ws, bs_t, ws_sample, bs_sample, w_out, tm, seq):
    T = xn.shape[0]
    n_tiles = T // tm
    Ts = xns.shape[0]
    ts = Ts // n_tiles
    assert Ts == ts * n_tiles and ts % 16 == 0 and ts % seq == 0 and D_MODEL % (16 * n_tiles) == 0
    slab = D_MODEL // n_tiles
    row = lambda i: (i, 0)
    layer = lambda i: (l, 0, 0)
    smem = pl.BlockSpec(memory_space=pltpu.SMEM)
    return pl.pallas_call(
        functools.partial(_sgu_kernel, l=l, seq=seq),
        grid=(n_tiles,),
        in_specs=[
            smem,
            smem,
            pl.BlockSpec((tm, D_MODEL), row),
            pl.BlockSpec((ts, D_MODEL), row),
            *_uv_weight_specs(l),
            pl.BlockSpec((None, 1, SGU_W), layer),
            pl.BlockSpec((None, 1, SGU_W), layer),
            pl.BlockSpec((None, SGU_GROUPS, SGU_CHUNK, SGU_CHUNK), lambda i: (l, 0, 0, 0)),
            pl.BlockSpec((None, SGU_CHUNK, SGU_GROUPS), layer),
            pl.BlockSpec((None, slab, D_MODEL), lambda i: (l, i, 0)),
        ],
        out_specs=[pl.BlockSpec((tm, SGU_W), row), pl.BlockSpec((ts, SGU_W), row),
                   pl.BlockSpec((ts, SGU_W), row), pl.BlockSpec((slab, D_MODEL), row)],
        out_shape=[jax.ShapeDtypeStruct((T, SGU_W), BF16), jax.ShapeDtypeStruct((Ts, SGU_W), BF16),
                   jax.ShapeDtypeStruct((Ts, SGU_W), F32), jax.ShapeDtypeStruct((D_MODEL, D_MODEL), BF16)],
        scratch_shapes=[pltpu.VMEM((tm + ts, SGU_W), F32), pltpu.VMEM((tm + ts, SGU_W), F32),
                        pltpu.VMEM((3, ts, SGU_CH), F32)],
        compiler_params=_params("parallel"),
        name="sgu",
    )(ws_sample, bs_sample, xn, xns, *([w_in] * UV_BLOCKS), lng, lnb, ws, bs_t, w_out)


def _attn_prompt_kernel(sink_ref, q_ref, kp_ref, kc_ref, vp_ref, vc_ref, wg_ref, o_ref, wg_bf_ref, *, l):
    wg_bf_ref[...] = wg_ref[...].astype(BF16)
    blk = WINDOW
    nq = Q_GROUP * blk
    first_tile = pl.program_id(1) == 0
    c = lax.broadcasted_iota(jnp.int32, (blk, nq), 0)
    r = lax.broadcasted_iota(jnp.int32, (blk, nq), 1) % blk
    from_prev = c > r
    k_all = jnp.concatenate([kp_ref[...], kc_ref[...]], axis=0).astype(BF16)
    vt_all = jnp.concatenate([vp_ref[...], vc_ref[...]], axis=0).T.astype(BF16)
    sinks = [jnp.concatenate([jnp.full((1, blk), sink_ref[l, g * Q_GROUP + j], F32) for j in range(Q_GROUP)],
                             axis=1) for g in range(N_KV_HEADS)]

    def scores(jb):
        out = []
        for g in range(N_KV_HEADS):
            qg = jnp.concatenate([q_ref[jb * blk:(jb + 1) * blk, h * HEAD_DIM:(h + 1) * HEAD_DIM]
                                  for h in range(g * Q_GROUP, (g + 1) * Q_GROUP)], axis=0)
            kg = k_all[jb * blk:(jb + 2) * blk, g * HEAD_DIM:(g + 1) * HEAD_DIM]
            out.append(lax.dot_general(kg, qg, (((1,), (1,)), ((), ())), preferred_element_type=F32))
        return out

    nblk = q_ref.shape[0] // blk
    s2 = scores(0)
    for jb in range(nblk):
        s2_next = scores(jb + 1) if jb + 1 < nblk else None
        for g in range(N_KV_HEADS):
            s_prev = s2[g][:blk]
            if jb == 0:
                s_prev = jnp.where(first_tile, NEG_INF, s_prev)
            s = jnp.where(from_prev, s_prev, s2[g][blk:])
            m = jnp.maximum(jnp.max(s, axis=0, keepdims=True), sinks[g])
            p = jnp.exp(s - m)
            p = p / (jnp.sum(p, axis=0, keepdims=True) + jnp.exp(sinks[g] - m))
            p2 = jnp.concatenate([jnp.where(from_prev, p, 0.0), jnp.where(from_prev, 0.0, p)], axis=0)
            vt = vt_all[g * HEAD_DIM:(g + 1) * HEAD_DIM, jb * blk:(jb + 2) * blk]
            o = _dot(vt, p2.astype(BF16)).T
            for j in range(Q_GROUP):
                h = g * Q_GROUP + j
                o_ref[jb * blk:(jb + 1) * blk, h * HEAD_DIM:(h + 1) * HEAD_DIM] = (
                    o[j * blk:(j + 1) * blk].astype(BF16))
        s2 = s2_next


def _attn_prompt_call(l, sinks, q, k, v, w_in, batch, seq, tq):
    nt = seq // tq
    blocks_per_tile = tq // WINDOW
    gate_w = 2 * D_MODEL // (batch * nt)
    assert gate_w % LANES == 0 and GATE_OFF % gate_w == 0
    cur = lambda b, i: (b * nt + i, 0)
    prev = lambda b, i: (b * (seq // WINDOW) + jnp.maximum(i * blocks_per_tile - 1, 0), 0)
    return pl.pallas_call(
        functools.partial(_attn_prompt_kernel, l=l),
        grid=(batch, nt),
        in_specs=[
            pl.BlockSpec(memory_space=pltpu.SMEM),
            pl.BlockSpec((tq, ATTN_W), cur),
            pl.BlockSpec((WINDOW, KV_W), prev),
            pl.BlockSpec((tq, KV_W), cur),
            pl.BlockSpec((WINDOW, KV_W), prev),
            pl.BlockSpec((tq, KV_W), cur),
            pl.BlockSpec((None, D_MODEL, gate_w), lambda b, i: (l, 0, GATE_OFF // gate_w + b * nt + i)),
        ],
        out_specs=[pl.BlockSpec((tq, ATTN_W), cur),
                   pl.BlockSpec((D_MODEL, gate_w), lambda b, i: (0, b * nt + i))],
        out_shape=[jax.ShapeDtypeStruct((batch * seq, ATTN_W), BF16),
                   jax.ShapeDtypeStruct((D_MODEL, 2 * D_MODEL), BF16)],
        compiler_params=_params("parallel", "parallel"),
        name="attn_prompt",
    )(sinks, q, k, k, v, v, w_in)


def _attn_sample_kernel(sink_ref, q_ref, kn_ref, vn_ref, ck_ref, cv_ref, o_ref, nk_ref, nv_ref, *, l, seq):
    nseq = q_ref.shape[0] // seq
    nq = nseq * seq
    rows = Q_GROUP * nq
    ri = lax.broadcasted_iota(jnp.int32, (rows, nseq * WINDOW), 0) % nq
    ci = lax.broadcasted_iota(jnp.int32, (rows, nseq * WINDOW), 1)
    mask_c = (ri // seq == ci // WINDOW) & (ci % WINDOW > ri % seq)
    ri = lax.broadcasted_iota(jnp.int32, (rows, nq), 0) % nq
    ci = lax.broadcasted_iota(jnp.int32, (rows, nq), 1)
    mask_n = (ri // seq == ci // seq) & (ci % seq <= ri % seq)
    dn = (((1,), (1,)), ((), ()))
    scores = []
    for g in range(N_KV_HEADS):
        cols = slice(g * HEAD_DIM, (g + 1) * HEAD_DIM)
        qg = jnp.concatenate([q_ref[:, h * HEAD_DIM:(h + 1) * HEAD_DIM]
                              for h in range(g * Q_GROUP, (g + 1) * Q_GROUP)], axis=0)
        scores.append((lax.dot_general(qg, ck_ref[:, cols].astype(BF16), dn, preferred_element_type=F32),
                       lax.dot_general(qg, kn_ref[:, cols].astype(BF16), dn, preferred_element_type=F32)))
    for g in range(N_KV_HEADS):
        cols = slice(g * HEAD_DIM, (g + 1) * HEAD_DIM)
        heads = range(g * Q_GROUP, (g + 1) * Q_GROUP)
        sink = jnp.concatenate([jnp.full((nq, 1), sink_ref[l, h], F32) for h in heads], axis=0)
        s_c = jnp.where(mask_c, scores[g][0], NEG_INF)
        s_n = jnp.where(mask_n, scores[g][1], NEG_INF)
        m = jnp.maximum(jnp.maximum(jnp.max(s_c, axis=-1, keepdims=True),
                                    jnp.max(s_n, axis=-1, keepdims=True)), sink)
        p_c = jnp.exp(s_c - m)
        p_n = jnp.exp(s_n - m)
        denom = (jnp.sum(p_c, axis=-1, keepdims=True) + jnp.sum(p_n, axis=-1, keepdims=True)
                 + jnp.exp(sink - m))
        o = (_dot((p_c / denom).astype(BF16), cv_ref[:, cols].astype(BF16))
             + _dot((p_n / denom).astype(BF16), vn_ref[:, cols].astype(BF16)))
        for j, h in enumerate(heads):
            o_ref[:, h * HEAD_DIM:(h + 1) * HEAD_DIM] = o[j * nq:(j + 1) * nq].astype(BF16)
    for b in range(nseq):
        nk_ref[b * WINDOW:(b + 1) * WINDOW - seq, :] = ck_ref[b * WINDOW + seq:(b + 1) * WINDOW, :]
        nv_ref[b * WINDOW:(b + 1) * WINDOW - seq, :] = cv_ref[b * WINDOW + seq:(b + 1) * WINDOW, :]
        nk_ref[(b + 1) * WINDOW - seq:(b + 1) * WINDOW, :] = kn_ref[b * seq:(b + 1) * seq, :]
        nv_ref[(b + 1) * WINDOW - seq:(b + 1) * WINDOW, :] = vn_ref[b * seq:(b + 1) * seq, :]


def _attn_sample_call(l, sinks, q, k, v, ck, cv, seq, seq_block):
    T = q.shape[0]
    nseq = T // seq
    row = lambda i: (i, 0)
    cache = lambda i: (l, i, 0)
    return pl.pallas_call(
        functools.partial(_attn_sample_kernel, l=l, seq=seq),
        grid=(nseq // seq_block,),
        in_specs=[
            pl.BlockSpec(memory_space=pltpu.SMEM),
            pl.BlockSpec((seq_block * seq, ATTN_W), row),
            pl.BlockSpec((seq_block * seq, KV_W), row),
            pl.BlockSpec((seq_block * seq, KV_W), row),
            pl.BlockSpec((None, seq_block * WINDOW, KV_W), cache),
            pl.BlockSpec((None, seq_block * WINDOW, KV_W), cache),
        ],
        out_specs=[
            pl.BlockSpec((seq_block * seq, ATTN_W), row),
            pl.BlockSpec((seq_block * WINDOW, KV_W), row),
            pl.BlockSpec((seq_block * WINDOW, KV_W), row),
        ],
        out_shape=[
            jax.ShapeDtypeStruct((T, ATTN_W), BF16),
            jax.ShapeDtypeStruct((nseq * WINDOW, KV_W), F32),
            jax.ShapeDtypeStruct((nseq * WINDOW, KV_W), F32),
        ],
        compiler_params=_params("parallel"),
        name="attn_sample",
    )(sinks, q, k, v, ck, cv)


def _merge_kernel(x_ref, xn_ref, a_ref, m_ref, wga_ref, wgm_ref, wau_ref, wsu_ref, wout_ref,
                  h_ref, merged_scr):
    j = pl.program_id(1)
    xn = xn_ref[...]
    ga = jax.nn.sigmoid(_dot(xn, wga_ref[...]))
    gm = jax.nn.sigmoid(_dot(xn, wgm_ref[...]))
    merged = ga * _dot(a_ref[...], wau_ref[...]) + gm * _dot(m_ref[...], wsu_ref[...])
    merged_scr[j] = merged.astype(BF16)

    @pl.when(j == pl.num_programs(1) - 1)
    def _():
        tn = merged_scr.shape[2]
        acc = x_ref[...]
        for c in range(merged_scr.shape[0]):
            acc = acc + _dot(merged_scr[c], wout_ref[c * tn:(c + 1) * tn, :])
        h_ref[...] = acc


def _merge_call(x, xn, a, m, w_gate, w_au, w_su, w_out, tm, tn):
    T = x.shape[0]
    nj = D_MODEL // tn
    row = lambda i, j: (i, 0)
    col = lambda i, j: (0, j)
    return pl.pallas_call(
        _merge_kernel,
        grid=(T // tm, nj),
        in_specs=[
            pl.BlockSpec((tm, D_MODEL), row),
            pl.BlockSpec((tm, D_MODEL), row),
            pl.BlockSpec((tm, ATTN_W), row),
            pl.BlockSpec((tm, SGU_W), row),
            pl.BlockSpec((D_MODEL, tn), col),
            pl.BlockSpec((D_MODEL, tn), lambda i, j: (0, nj + j)),
            pl.BlockSpec((ATTN_W, tn), col),
            pl.BlockSpec((SGU_W, tn), col),
            pl.BlockSpec((D_MODEL, D_MODEL), lambda i, j: (0, 0)),
        ],
        out_specs=pl.BlockSpec((tm, D_MODEL), row),
        out_shape=jax.ShapeDtypeStruct((T, D_MODEL), F32),
        scratch_shapes=[pltpu.VMEM((nj, tm, tn), BF16)],
        compiler_params=_params("parallel", "arbitrary"),
        name="merge",
    )(x, xn, a, m, w_gate, w_gate, w_au, w_su, w_out)


def _ffn_kernel(h_ref, hs_ref, g2_ref, w1_ref, w2_ref, y_ref, ys_ref, hn_scr):
    tm = h_ref.shape[0]

    @pl.when(pl.program_id(1) == 0)
    def _():
        h = h_ref[...]
        hn_scr[:tm, :] = _rms_rows(h, g2_ref[...]).astype(BF16)
        y_ref[...] = h
        hs = hs_ref[...]
        hn_scr[tm:, :] = _rms_rows(hs, g2_ref[...]).astype(BF16)
        ys_ref[...] = hs

    half = tm // 2
    w1 = w1_ref[...].astype(BF16)
    w2 = w2_ref[...].astype(BF16)
    acts = []
    for rows in (slice(0, half), slice(half, hn_scr.shape[0])):
        a = jnp.maximum(_dot(hn_scr[rows, :], w1), 0.0)
        acts.append((a * a).astype(BF16))
    y_ref[:half, :] += _dot(acts[0], w2)
    second = _dot(acts[1], w2)
    y_ref[half:, :] += second[:half]
    ys_ref[...] += second[half:]


def _ffn_call(l, h, hs, g2, w1, w2, tm, tf):
    T = h.shape[0]
    n_tiles = T // tm
    ts = hs.shape[0] // n_tiles
    assert hs.shape[0] == ts * n_tiles and ts % 16 == 0
    row = lambda i, f: (i, 0)
    return pl.pallas_call(
        _ffn_kernel,
        grid=(n_tiles, D_FF // tf),
        in_specs=[
            pl.BlockSpec((tm, D_MODEL), row),
            pl.BlockSpec((ts, D_MODEL), row),
            pl.BlockSpec((None, 1, D_MODEL), lambda i, f: (l, 0, 0)),
            pl.BlockSpec((None, D_MODEL, tf), lambda i, f: (l, 0, f)),
            pl.BlockSpec((None, tf, D_MODEL), lambda i, f: (l, f, 0)),
        ],
        out_specs=[pl.BlockSpec((tm, D_MODEL), row), pl.BlockSpec((ts, D_MODEL), row)],
        out_shape=[jax.ShapeDtypeStruct((T, D_MODEL), F32), jax.ShapeDtypeStruct(hs.shape, F32)],
        scratch_shapes=[pltpu.VMEM((tm + ts, D_MODEL), BF16)],
        compiler_params=_params("parallel", "arbitrary"),
        name="ffn",
    )(h, hs, g2, w1, w2)


def _rope_tables(pos):
    half = HEAD_DIM // 2
    inv = jnp.power(jnp.float32(ROPE_THETA), -jnp.arange(half, dtype=jnp.float32) / half)
    ang = pos.astype(jnp.float32)[:, None] * inv[None, :]
    cos = jnp.cos(ang)
    sin = jnp.sin(ang)
    return jnp.tile(cos, (1, 2 * N_KV_HEADS)), jnp.tile(jnp.concatenate([-sin, sin], axis=-1), (1, N_KV_HEADS))


def kernel(x_prompt, x_sample, cache_k, cache_v, norm1_g, w_in, q_norm_g, k_norm_g, attn_sinks,
           sgu_ln_g, sgu_ln_b, sgu_w, sgu_b, w_attn_up, w_sgu_up, w_out, norm2_g, w_ff1, w_ff2):
    batch, seq, _ = x_prompt.shape
    dec_batch, dec_seq, _ = x_sample.shape
    tp = batch * seq
    ts = dec_batch * dec_seq
    tm = 512
    tn = 512
    tq = 512
    tp_tile = 2 * tm
    assert seq % tm == 0 and tm % SGU_CHUNK == 0 and seq % tq == 0 and tq % WINDOW == 0

    cos_p, sin_p = _rope_tables(jnp.arange(seq))
    cos_s, sin_s = _rope_tables(jnp.tile(PAST_LEN + jnp.arange(dec_seq), dec_batch))

    g1 = norm1_g.reshape(DEPTH, 1, D_MODEL)
    g2 = norm2_g.reshape(DEPTH, 1, D_MODEL)
    qg = jnp.tile(q_norm_g, (1, N_KV_HEADS)).reshape(DEPTH, 1, KV_W)
    kg = jnp.tile(k_norm_g, (1, N_KV_HEADS)).reshape(DEPTH, 1, KV_W)
    lng = sgu_ln_g.reshape(DEPTH, 1, SGU_W)
    lnb = sgu_ln_b.reshape(DEPTH, 1, SGU_W)
    bs_t = jnp.swapaxes(sgu_b, 1, 2)
    ws_s = sgu_w[:, :, :dec_seq, :dec_seq].reshape(DEPTH, -1)
    bs_s = sgu_b[:, :, :dec_seq].reshape(DEPTH, -1)
    ck = cache_k.reshape(DEPTH, dec_batch * WINDOW, KV_W)
    cv = cache_v.reshape(DEPTH, dec_batch * WINDOW, KV_W)

    xp = x_prompt.reshape(tp, D_MODEL)
    xs = x_sample.reshape(ts, D_MODEL)
    kp_l, vp_l, ks_l, vs_l, sv_l = [], [], [], [], []
    for l in range(DEPTH):
        xn, q, k, v, xns, qs, ks, vs, w_au, w_su = _qkv_call(
            l, xp, xs, g1, w_in, cos_p, sin_p, cos_s, sin_s, qg, kg, w_attn_up, w_sgu_up, tm, seq // tm)
        m, ms, vsn, w_o = _sgu_call(l, xn, xns, w_in, lng, lnb, sgu_w, bs_t, ws_s, bs_s, w_out, tp_tile, dec_seq)
        a, w_gate = _attn_prompt_call(l, attn_sinks, q, k, v, w_in, batch, seq, tq)
        hp = _merge_call(xp, xn, a, m, w_gate, w_au, w_su, w_o, tm, tn)
        a, nk, nv = _attn_sample_call(l, attn_sinks, qs, ks, vs, ck, cv, dec_seq, 8)
        hs = _merge_call(xs, xns, a, ms, w_gate, w_au, w_su, w_o, ts, tn)
        xp, xs = _ffn_call(l, hp, hs, g2, w_ff1, w_ff2, 2 * tm, tn)
        kp_l.append(k.reshape(batch, seq, KV_W)[:, -WINDOW:].reshape(batch, WINDOW, N_KV_HEADS, HEAD_DIM))
        vp_l.append(v.reshape(batch, seq, KV_W)[:, -WINDOW:].reshape(batch, WINDOW, N_KV_HEADS, HEAD_DIM))
        ks_l.append(nk.reshape(dec_batch, WINDOW, N_KV_HEADS, HEAD_DIM))
        vs_l.append(nv.reshape(dec_batch, WINDOW, N_KV_HEADS, HEAD_DIM))
        sv_l.append(vsn.reshape(dec_batch, dec_seq, SGU_GROUPS, SGU_CH))

    return (xp.reshape(batch, seq, D_MODEL), xs.reshape(dec_batch, dec_seq, D_MODEL),
            jnp.stack(kp_l), jnp.stack(vp_l), jnp.stack(ks_l), jnp.stack(vs_l), jnp.stack(sv_l))
```

```python
import functools
import math

import jax
import jax.numpy as jnp
import numpy as np
from jax import lax
from jax.experimental import pallas as pl
from jax.experimental.pallas import tpu as pltpu

D_MODEL = 2048
DEPTH = 4
PAST_LEN = 16384
HEAD_DIM = 64
N_HEADS = 16
N_KV_HEADS = 4
Q_GROUP = 4
ATTN_W = N_HEADS * HEAD_DIM
KV_W = N_KV_HEADS * HEAD_DIM
WINDOW = 128
ROPE_THETA = 10000.0
SGU_CHUNK = 128
SGU_W = D_MODEL // 2
SGU_GROUPS = 8
SGU_CH = SGU_W // SGU_GROUPS
D_FF = 4 * D_MODEL
RMS_EPS = 1e-6
LN_EPS = 1e-5
NEG_INF = -1e30

QKV_W = ATTN_W + 2 * KV_W
UV_OFF = QKV_W
GATE_OFF = QKV_W + 2 * SGU_W

LANES = 128
VMEM_LIMIT = 58 * 1024 * 1024

BF16 = jnp.bfloat16
F32 = jnp.float32


def _dot(a, b):
    return jnp.dot(a, b, preferred_element_type=F32)


def _rms_rows(x, g):
    ms = jnp.mean(x * x, axis=-1, keepdims=True)
    return x * lax.rsqrt(ms + RMS_EPS) * g


def _gelu(x):
    return 0.5 * x * (1.0 + lax.erf(x * np.float32(math.sqrt(0.5))))


def _params(*sem):
    return pltpu.CompilerParams(dimension_semantics=sem, vmem_limit_bytes=VMEM_LIMIT)


def _head_norm_rope(z, gain, cos, sin, gmat, first_half):
    width = z.shape[1]
    sq = z * z
    hi = sq.astype(BF16)
    lo = (sq - hi.astype(F32)).astype(BF16)
    ssum = _dot(jnp.concatenate([hi, lo], axis=1), gmat)
    zn = z * lax.rsqrt(ssum * (1.0 / HEAD_DIM) + RMS_EPS) * gain
    rot = jnp.where(first_half, pltpu.roll(zn, width - HEAD_DIM // 2, 1), pltpu.roll(zn, HEAD_DIM // 2, 1))
    return zn * cos + rot * sin


def _qkv_kernel(x_ref, xs_ref, g1_ref, w_ref, cos_ref, sin_ref, coss_ref, sins_ref, qg_ref, kg_ref,
                wau_ref, wsu_ref,
                xn_ref, q_ref, k_ref, v_ref, xns_ref, qs_ref, ks_ref, vs_ref, wup_bf_ref):
    wup_bf_ref[0] = wau_ref[...].astype(BF16)
    wup_bf_ref[1] = wsu_ref[...].astype(BF16)
    tm = x_ref.shape[0]
    rows = tm + xs_ref.shape[0]
    cw = KV_W
    g1 = g1_ref[...]
    xn = jnp.concatenate([_rms_rows(x_ref[...], g1).astype(BF16), _rms_rows(xs_ref[...], g1).astype(BF16)],
                         axis=0)
    xn_ref[...] = xn[:tm]
    xns_ref[...] = xn[tm:]
    lane = lax.broadcasted_iota(jnp.int32, (rows, cw), 1)
    first_half = (lane % HEAD_DIM) < (HEAD_DIM // 2)
    r = (lax.broadcasted_iota(jnp.int32, (2 * cw, cw), 0) % cw) // HEAD_DIM
    c = lax.broadcasted_iota(jnp.int32, (2 * cw, cw), 1) // HEAD_DIM
    gmat = jnp.where(r == c, 1.0, 0.0).astype(BF16)
    cos = jnp.concatenate([cos_ref[...], coss_ref[...]], axis=0)
    sin = jnp.concatenate([sin_ref[...], sins_ref[...]], axis=0)
    scale = HEAD_DIM ** -0.5
    n_q = ATTN_W // cw

    def project(c):
        return _dot(xn, w_ref[:, c * cw:(c + 1) * cw].astype(BF16))

    z = project(0)
    for c in range(n_q + 1):
        z_next = project(c + 1)
        if c < n_q:
            q = (_head_norm_rope(z, qg_ref[...], cos, sin, gmat, first_half) * scale).astype(BF16)
            q_ref[:, c * cw:(c + 1) * cw] = q[:tm]
            qs_ref[:, c * cw:(c + 1) * cw] = q[tm:]
        else:
            k = _head_norm_rope(z, kg_ref[...], cos, sin, gmat, first_half)
            k_ref[...] = k[:tm]
            ks_ref[...] = k[tm:]
        z = z_next
    v_ref[...] = z[:tm]
    vs_ref[...] = z[tm:]


def _qkv_call(l, x, xs, g1, w_in, cos_t, sin_t, cos_s, sin_s, qg, kg, w_attn_up, w_sgu_up, tm, n_pos_blocks):
    T = x.shape[0]
    n_tiles = T // tm
    Ts = xs.shape[0]
    ts = Ts // n_tiles
    assert Ts == ts * n_tiles and ts % 16 == 0
    assert ATTN_W == SGU_W and ATTN_W % (16 * n_tiles) == 0
    row = lambda i: (i, 0)
    layer = lambda i: (l, 0, 0)
    pos = lambda i: (i % n_pos_blocks, 0)
    return pl.pallas_call(
        _qkv_kernel,
        grid=(n_tiles,),
        in_specs=[
            pl.BlockSpec((tm, D_MODEL), row),
            pl.BlockSpec((ts, D_MODEL), row),
            pl.BlockSpec((None, 1, D_MODEL), layer),
            pl.BlockSpec((None, D_MODEL, QKV_W), layer),
            pl.BlockSpec((tm, KV_W), pos),
            pl.BlockSpec((tm, KV_W), pos),
            pl.BlockSpec((ts, KV_W), row),
            pl.BlockSpec((ts, KV_W), row),
            pl.BlockSpec((None, 1, KV_W), layer),
            pl.BlockSpec((None, 1, KV_W), layer),
            pl.BlockSpec((None, ATTN_W // n_tiles, D_MODEL), lambda i: (l, i, 0)),
            pl.BlockSpec((None, SGU_W // n_tiles, D_MODEL), lambda i: (l, i, 0)),
        ],
        out_specs=[
            pl.BlockSpec((tm, D_MODEL), row),
            pl.BlockSpec((tm, ATTN_W), row),
            pl.BlockSpec((tm, KV_W), row),
            pl.BlockSpec((tm, KV_W), row),
            pl.BlockSpec((ts, D_MODEL), row),
            pl.BlockSpec((ts, ATTN_W), row),
            pl.BlockSpec((ts, KV_W), row),
            pl.BlockSpec((ts, KV_W), row),
            pl.BlockSpec((2, ATTN_W // n_tiles, D_MODEL), lambda i: (0, i, 0)),
        ],
        out_shape=[
            jax.ShapeDtypeStruct((T, D_MODEL), BF16),
            jax.ShapeDtypeStruct((T, ATTN_W), BF16),
            jax.ShapeDtypeStruct((T, KV_W), F32),
            jax.ShapeDtypeStruct((T, KV_W), F32),
            jax.ShapeDtypeStruct((Ts, D_MODEL), BF16),
            jax.ShapeDtypeStruct((Ts, ATTN_W), BF16),
            jax.ShapeDtypeStruct((Ts, KV_W), F32),
            jax.ShapeDtypeStruct((Ts, KV_W), F32),
            jax.ShapeDtypeStruct((2, ATTN_W, D_MODEL), BF16),
        ],
        compiler_params=_params("parallel"),
        name="qkv",
    )(x, xs, g1, w_in, cos_t, sin_t, cos_s, sin_s, qg, kg, w_attn_up, w_sgu_up)


UV_BLOCK = 512
UV_BLOCKS = 2 * SGU_W // UV_BLOCK


def _uv_project(xn, w_refs, lng_ref, lnb_ref, u_scr, vs_scr):
    cw = 2 * LANES
    per_block = UV_BLOCK // cw
    n_u = SGU_W // cw
    for c in list(range(n_u, 2 * n_u)) + list(range(n_u)):
        w = w_refs[c // per_block][:, (c % per_block) * cw:(c % per_block + 1) * cw].astype(BF16)
        dst = u_scr if c < n_u else vs_scr
        c_dst = c % n_u
        dst[:, c_dst * cw:(c_dst + 1) * cw] = _gelu(_dot(xn, w))
    vs = vs_scr[...]
    mu = jnp.mean(vs, axis=-1, keepdims=True)
    d = vs - mu
    var = jnp.mean(d * d, axis=-1, keepdims=True)
    vs_scr[...] = d * lax.rsqrt(var + LN_EPS) * lng_ref[...] + lnb_ref[...]


def _sgu_kernel(wss_ref, bss_ref, xn_ref, xns_ref, *refs, l, seq):
    w_refs = refs[:UV_BLOCKS]
    (lng_ref, lnb_ref, ws_ref, bs_ref, wout_ref,
     m_ref, ms_ref, vsn_ref, wout_bf_ref, u_scr, vs_scr, g_scr) = refs[UV_BLOCKS:]
    wout_bf_ref[...] = wout_ref[...].astype(BF16)
    tm = xn_ref.shape[0]
    nseq = xns_ref.shape[0] // seq
    xn = jnp.concatenate([xn_ref[...], xns_ref[...]], axis=0)
    _uv_project(xn, w_refs, lng_ref, lnb_ref, u_scr, vs_scr)

    t = lax.broadcasted_iota(jnp.int32, (SGU_CHUNK, SGU_CHUNK), 0)
    s = lax.broadcasted_iota(jnp.int32, (SGU_CHUNK, SGU_CHUNK), 1)
    causal = t >= s
    for g in range(SGU_GROUPS):
        wg = jnp.where(causal, ws_ref[g], 0.0).astype(BF16)
        bias = bs_ref[:, g:g + 1]
        cols = slice(g * SGU_CH, (g + 1) * SGU_CH)
        for n in range(tm // SGU_CHUNK):
            rows = slice(n * SGU_CHUNK, (n + 1) * SGU_CHUNK)
            mixed = _dot(wg, vs_scr[rows, cols].astype(BF16)) + bias
            m_ref[rows, cols] = (u_scr[rows, cols] * mixed).astype(BF16)

    vsn_ref[...] = vs_scr[tm:, :]
    for g in range(SGU_GROUPS):
        cols = slice(g * SGU_CH, (g + 1) * SGU_CH)
        g_scr[0] = vs_scr[tm:, cols]
        g_scr[1] = u_scr[tm:, cols]
        for t in range(seq):
            step_t = pl.ds(t, nseq, stride=seq)
            acc = jnp.full((nseq, SGU_CH), bss_ref[l, g * seq + t], F32)
            for s in range(t + 1):
                acc = acc + wss_ref[l, (g * seq + t) * seq + s] * g_scr[0, pl.ds(s, nseq, stride=seq), :]
            g_scr[2, step_t, :] = g_scr[1, step_t, :] * acc
        ms_ref[:, cols] = g_scr[2].astype(BF16)


def _uv_weight_specs(l):
    return [pl.BlockSpec((None, D_MODEL, UV_BLOCK), functools.partial(lambda b, i: (l, 0, b), UV_OFF // UV_BLOCK + k))
            for k in range(UV_BLOCKS)]


def _sgu_call(l, xn, xns, w_in, lng, lnb, ws, bs_t, ws_sample, bs_sample, w_out, tm, seq):
    T = xn.shape[0]
    n_tiles = T // tm
    Ts = xns.shape[0]
    ts = Ts // n_tiles
    assert Ts == ts * n_tiles and ts % 16 == 0 and ts % seq == 0 and D_MODEL % (16 * n_tiles) == 0
    slab = D_MODEL // n_tiles
    row = lambda i: (i, 0)
    layer = lambda i: (l, 0, 0)
    smem = pl.BlockSpec(memory_space=pltpu.SMEM)
    return pl.pallas_call(
        functools.partial(_sgu_kernel, l=l, seq=seq),
        grid=(n_tiles,),
        in_specs=[
            smem,
            smem,
            pl.BlockSpec((tm, D_MODEL), row),
            pl.BlockSpec((ts, D_MODEL), row),
            *_uv_weight_specs(l),
            pl.BlockSpec((None, 1, SGU_W), layer),
            pl.BlockSpec((None, 1, SGU_W), layer),
            pl.BlockSpec((None, SGU_GROUPS, SGU_CHUNK, SGU_CHUNK), lambda i: (l, 0, 0, 0)),
            pl.BlockSpec((None, SGU_CHUNK, SGU_GROUPS), layer),
            pl.BlockSpec((None, slab, D_MODEL), lambda i: (l, i, 0)),
        ],
        out_specs=[pl.BlockSpec((tm, SGU_W), row), pl.BlockSpec((ts, SGU_W), row),
                   pl.BlockSpec((ts, SGU_W), row), pl.BlockSpec((slab, D_MODEL), row)],
        out_shape=[jax.ShapeDtypeStruct((T, SGU_W), BF16), jax.ShapeDtypeStruct((Ts, SGU_W), BF16),
                   jax.ShapeDtypeStruct((Ts, SGU_W), F32), jax.ShapeDtypeStruct((D_MODEL, D_MODEL), BF16)],
        scratch_shapes=[pltpu.VMEM((tm + ts, SGU_W), F32), pltpu.VMEM((tm + ts, SGU_W), F32),
                        pltpu.VMEM((3, ts, SGU_CH), F32)],
        compiler_params=_params("parallel"),
        name="sgu",
    )(ws_sample, bs_sample, xn, xns, *([w_in] * UV_BLOCKS), lng, lnb, ws, bs_t, w_out)


def _attn_prompt_kernel(sink_ref, q_ref, kp_ref, kc_ref, vp_ref, vc_ref, wg_ref, o_ref, wg_bf_ref, *, l):
    wg_bf_ref[...] = wg_ref[...].astype(BF16)
    blk = WINDOW
    nq = Q_GROUP * blk
    first_tile = pl.program_id(1) == 0
    c = lax.broadcasted_iota(jnp.int32, (blk, nq), 0)
    r = lax.broadcasted_iota(jnp.int32, (blk, nq), 1) % blk
    from_prev = c > r
    k_all = jnp.concatenate([kp_ref[...], kc_ref[...]], axis=0).astype(BF16)
    vt_all = jnp.concatenate([vp_ref[...], vc_ref[...]], axis=0).T.astype(BF16)
    sinks = [jnp.concatenate([jnp.full((1, blk), sink_ref[l, g * Q_GROUP + j], F32) for j in range(Q_GROUP)],
                             axis=1) for g in range(N_KV_HEADS)]

    def scores(jb):
        out = []
        for g in range(N_KV_HEADS):
            qg = jnp.concatenate([q_ref[jb * blk:(jb + 1) * blk, h * HEAD_DIM:(h + 1) * HEAD_DIM]
                                  for h in range(g * Q_GROUP, (g + 1) * Q_GROUP)], axis=0)
            kg = k_all[jb * blk:(jb + 2) * blk, g * HEAD_DIM:(g + 1) * HEAD_DIM]
            out.append(lax.dot_general(kg, qg, (((1,), (1,)), ((), ())), preferred_element_type=F32))
        return out

    nblk = q_ref.shape[0] // blk
    s2 = scores(0)
    for jb in range(nblk):
        s2_next = scores(jb + 1) if jb + 1 < nblk else None
        for g in range(N_KV_HEADS):
            s_prev = s2[g][:blk]
            if jb == 0:
                s_prev = jnp.where(first_tile, NEG_INF, s_prev)
            s = jnp.where(from_prev, s_prev, s2[g][blk:])
            m = jnp.maximum(jnp.max(s, axis=0, keepdims=True), sinks[g])
            p = jnp.exp(s - m)
            p = p / (jnp.sum(p, axis=0, keepdims=True) + jnp.exp(sinks[g] - m))
            p2 = jnp.concatenate([jnp.where(from_prev, p, 0.0), jnp.where(from_prev, 0.0, p)], axis=0)
            vt = vt_all[g * HEAD_DIM:(g + 1) * HEAD_DIM, jb * blk:(jb + 2) * blk]
            o = _dot(vt, p2.astype(BF16)).T
            for j in range(Q_GROUP):
                h = g * Q_GROUP + j
                o_ref[jb * blk:(jb + 1) * blk, h * HEAD_DIM:(h + 1) * HEAD_DIM] = (
                    o[j * blk:(j + 1) * blk].astype(BF16))
        s2 = s2_next


def _attn_prompt_call(l, sinks, q, k, v, w_in, batch, seq, tq, merge_tn):
    nt = seq // tq
    blocks_per_tile = tq // WINDOW
    gate_w = 2 * D_MODEL // (batch * nt)
    assert gate_w % LANES == 0 and GATE_OFF % gate_w == 0 and merge_tn % gate_w == 0
    per_block = merge_tn // gate_w
    per_gate = D_MODEL // gate_w

    def gate_dst(b, i):
        t = b * nt + i
        gate, r = t // per_gate, t % per_gate
        return (r // per_block) * (2 * per_block) + gate * per_block + r % per_block

    cur = lambda b, i: (b * nt + i, 0)
    prev = lambda b, i: (b * (seq // WINDOW) + jnp.maximum(i * blocks_per_tile - 1, 0), 0)
    return pl.pallas_call(
        functools.partial(_attn_prompt_kernel, l=l),
        grid=(batch, nt),
        in_specs=[
            pl.BlockSpec(memory_space=pltpu.SMEM),
            pl.BlockSpec((tq, ATTN_W), cur),
            pl.BlockSpec((WINDOW, KV_W), prev),
            pl.BlockSpec((tq, KV_W), cur),
            pl.BlockSpec((WINDOW, KV_W), prev),
            pl.BlockSpec((tq, KV_W), cur),
            pl.BlockSpec((None, D_MODEL, gate_w), lambda b, i: (l, 0, GATE_OFF // gate_w + b * nt + i)),
        ],
        out_specs=[pl.BlockSpec((tq, ATTN_W), cur),
                   pl.BlockSpec((D_MODEL, gate_w), lambda b, i: (0, gate_dst(b, i)))],
        out_shape=[jax.ShapeDtypeStruct((batch * seq, ATTN_W), BF16),
                   jax.ShapeDtypeStruct((D_MODEL, 2 * D_MODEL), BF16)],
        compiler_params=_params("parallel", "parallel"),
        name="attn_prompt",
    )(sinks, q, k, k, v, v, w_in)


def _attn_sample_kernel(sink_ref, q_ref, kn_ref, vn_ref, ck_ref, cv_ref, o_ref, nk_ref, nv_ref, *, l, seq):
    nseq = q_ref.shape[0] // seq
    nq = nseq * seq
    rows = Q_GROUP * nq
    ri = lax.broadcasted_iota(jnp.int32, (rows, nseq * WINDOW), 0) % nq
    ci = lax.broadcasted_iota(jnp.int32, (rows, nseq * WINDOW), 1)
    mask_c = (ri // seq == ci // WINDOW) & (ci % WINDOW > ri % seq)
    ri = lax.broadcasted_iota(jnp.int32, (rows, nq), 0) % nq
    ci = lax.broadcasted_iota(jnp.int32, (rows, nq), 1)
    mask_n = (ri // seq == ci // seq) & (ci % seq <= ri % seq)
    dn = (((1,), (1,)), ((), ()))
    scores = []
    for g in range(N_KV_HEADS):
        cols = slice(g * HEAD_DIM, (g + 1) * HEAD_DIM)
        qg = jnp.concatenate([q_ref[:, h * HEAD_DIM:(h + 1) * HEAD_DIM]
                              for h in range(g * Q_GROUP, (g + 1) * Q_GROUP)], axis=0)
        scores.append((lax.dot_general(qg, ck_ref[:, cols].astype(BF16), dn, preferred_element_type=F32),
                       lax.dot_general(qg, kn_ref[:, cols].astype(BF16), dn, preferred_element_type=F32)))
    for g in range(N_KV_HEADS):
        cols = slice(g * HEAD_DIM, (g + 1) * HEAD_DIM)
        heads = range(g * Q_GROUP, (g + 1) * Q_GROUP)
        sink = jnp.concatenate([jnp.full((nq, 1), sink_ref[l, h], F32) for h in heads], axis=0)
        s_c = jnp.where(mask_c, scores[g][0], NEG_INF)
        s_n = jnp.where(mask_n, scores[g][1], NEG_INF)
        m = jnp.maximum(jnp.maximum(jnp.max(s_c, axis=-1, keepdims=True),
                                    jnp.max(s_n, axis=-1, keepdims=True)), sink)
        p_c = jnp.exp(s_c - m)
        p_n = jnp.exp(s_n - m)
        denom = (jnp.sum(p_c, axis=-1, keepdims=True) + jnp.sum(p_n, axis=-1, keepdims=True)
                 + jnp.exp(sink - m))
        o = (_dot((p_c / denom).astype(BF16), cv_ref[:, cols].astype(BF16))
             + _dot((p_n / denom).astype(BF16), vn_ref[:, cols].astype(BF16)))
        for j, h in enumerate(heads):
            o_ref[:, h * HEAD_DIM:(h + 1) * HEAD_DIM] = o[j * nq:(j + 1) * nq].astype(BF16)
    for b in range(nseq):
        nk_ref[b * WINDOW:(b + 1) * WINDOW - seq, :] = ck_ref[b * WINDOW + seq:(b + 1) * WINDOW, :]
        nv_ref[b * WINDOW:(b + 1) * WINDOW - seq, :] = cv_ref[b * WINDOW + seq:(b + 1) * WINDOW, :]
        nk_ref[(b + 1) * WINDOW - seq:(b + 1) * WINDOW, :] = kn_ref[b * seq:(b + 1) * seq, :]
        nv_ref[(b + 1) * WINDOW - seq:(b + 1) * WINDOW, :] = vn_ref[b * seq:(b + 1) * seq, :]


def _attn_sample_call(l, sinks, q, k, v, ck, cv, seq, seq_block):
    T = q.shape[0]
    nseq = T // seq
    row = lambda i: (i, 0)
    cache = lambda i: (l, i, 0)
    return pl.pallas_call(
        functools.partial(_attn_sample_kernel, l=l, seq=seq),
        grid=(nseq // seq_block,),
        in_specs=[
            pl.BlockSpec(memory_space=pltpu.SMEM),
            pl.BlockSpec((seq_block * seq, ATTN_W), row),
            pl.BlockSpec((seq_block * seq, KV_W), row),
            pl.BlockSpec((seq_block * seq, KV_W), row),
            pl.BlockSpec((None, seq_block * WINDOW, KV_W), cache),
            pl.BlockSpec((None, seq_block * WINDOW, KV_W), cache),
        ],
        out_specs=[
            pl.BlockSpec((seq_block * seq, ATTN_W), row),
            pl.BlockSpec((seq_block * WINDOW, KV_W), row),
            pl.BlockSpec((seq_block * WINDOW, KV_W), row),
        ],
        out_shape=[
            jax.ShapeDtypeStruct((T, ATTN_W), BF16),
            jax.ShapeDtypeStruct((nseq * WINDOW, KV_W), F32),
            jax.ShapeDtypeStruct((nseq * WINDOW, KV_W), F32),
        ],
        compiler_params=_params("parallel"),
        name="attn_sample",
    )(sinks, q, k, v, ck, cv)


def _merge_kernel(x_ref, xn_ref, a_ref, m_ref, wg_ref, wup_ref, wout_ref, h_ref, merged_scr):
    j = pl.program_id(1)
    tn = merged_scr.shape[2]
    xn = xn_ref[...]
    ga = jax.nn.sigmoid(_dot(xn, wg_ref[:, :tn]))
    gm = jax.nn.sigmoid(_dot(xn, wg_ref[:, tn:]))
    merged = ga * _dot(a_ref[...], wup_ref[0]) + gm * _dot(m_ref[...], wup_ref[1])
    merged_scr[j] = merged.astype(BF16)

    @pl.when(j == pl.num_programs(1) - 1)
    def _():
        acc = x_ref[...]
        for c in range(merged_scr.shape[0]):
            acc = acc + _dot(merged_scr[c], wout_ref[c * tn:(c + 1) * tn, :])
        h_ref[...] = acc


def _merge_call(x, xn, a, m, w_gate, w_up, w_out, tm, tn):
    T = x.shape[0]
    nj = D_MODEL // tn
    row = lambda i, j: (i, 0)
    return pl.pallas_call(
        _merge_kernel,
        grid=(T // tm, nj),
        in_specs=[
            pl.BlockSpec((tm, D_MODEL), row),
            pl.BlockSpec((tm, D_MODEL), row),
            pl.BlockSpec((tm, ATTN_W), row),
            pl.BlockSpec((tm, SGU_W), row),
            pl.BlockSpec((D_MODEL, 2 * tn), lambda i, j: (0, j)),
            pl.BlockSpec((2, ATTN_W, tn), lambda i, j: (0, 0, j)),
            pl.BlockSpec((D_MODEL, D_MODEL), lambda i, j: (0, 0)),
        ],
        out_specs=pl.BlockSpec((tm, D_MODEL), row),
        out_shape=jax.ShapeDtypeStruct((T, D_MODEL), F32),
        scratch_shapes=[pltpu.VMEM((nj, tm, tn), BF16)],
        compiler_params=_params("parallel", "arbitrary"),
        name="merge",
    )(x, xn, a, m, w_gate, w_up, w_out)


def _ffn_kernel(h_ref, hs_ref, g2_ref, w1_ref, w2_ref, y_ref, ys_ref, hn_scr):
    tm = h_ref.shape[0]

    @pl.when(pl.program_id(1) == 0)
    def _():
        h = h_ref[...]
        hn_scr[:tm, :] = _rms_rows(h, g2_ref[...]).astype(BF16)
        y_ref[...] = h
        hs = hs_ref[...]
        hn_scr[tm:, :] = _rms_rows(hs, g2_ref[...]).astype(BF16)
        ys_ref[...] = hs

    half = tm // 2
    w1 = w1_ref[...].astype(BF16)
    w2 = w2_ref[...].astype(BF16)
    acts = []
    for rows in (slice(0, half), slice(half, hn_scr.shape[0])):
        a = jnp.maximum(_dot(hn_scr[rows, :], w1), 0.0)
        acts.append((a * a).astype(BF16))
    y_ref[:half, :] += _dot(acts[0], w2)
    second = _dot(acts[1], w2)
    y_ref[half:, :] += second[:half]
    ys_ref[...] += second[half:]


def _ffn_call(l, h, hs, g2, w1, w2, tm, tf):
    T = h.shape[0]
    n_tiles = T // tm
    ts = hs.shape[0] // n_tiles
    assert hs.shape[0] == ts * n_tiles and ts % 16 == 0
    row = lambda i, f: (i, 0)
    return pl.pallas_call(
        _ffn_kernel,
        grid=(n_tiles, D_FF // tf),
        in_specs=[
            pl.BlockSpec((tm, D_MODEL), row),
            pl.BlockSpec((ts, D_MODEL), row),
            pl.BlockSpec((None, 1, D_MODEL), lambda i, f: (l, 0, 0)),
            pl.BlockSpec((None, D_MODEL, tf), lambda i, f: (l, 0, f)),
            pl.BlockSpec((None, tf, D_MODEL), lambda i, f: (l, f, 0)),
        ],
        out_specs=[pl.BlockSpec((tm, D_MODEL), row), pl.BlockSpec((ts, D_MODEL), row)],
        out_shape=[jax.ShapeDtypeStruct((T, D_MODEL), F32), jax.ShapeDtypeStruct(hs.shape, F32)],
        scratch_shapes=[pltpu.VMEM((tm + ts, D_MODEL), BF16)],
        compiler_params=_params("parallel", "arbitrary"),
        name="ffn",
    )(h, hs, g2, w1, w2)


def _rope_tables(pos):
    half = HEAD_DIM // 2
    inv = jnp.power(jnp.float32(ROPE_THETA), -jnp.arange(half, dtype=jnp.float32) / half)
    ang = pos.astype(jnp.float32)[:, None] * inv[None, :]
    cos = jnp.cos(ang)
    sin = jnp.sin(ang)
    return jnp.tile(cos, (1, 2 * N_KV_HEADS)), jnp.tile(jnp.concatenate([-sin, sin], axis=-1), (1, N_KV_HEADS))


def kernel(x_prompt, x_sample, cache_k, cache_v, norm1_g, w_in, q_norm_g, k_norm_g, attn_sinks,
           sgu_ln_g, sgu_ln_b, sgu_w, sgu_b, w_attn_up, w_sgu_up, w_out, norm2_g, w_ff1, w_ff2):
    batch, seq, _ = x_prompt.shape
    dec_batch, dec_seq, _ = x_sample.shape
    tp = batch * seq
    ts = dec_batch * dec_seq
    tm = 512
    tn = 512
    tq = 512
    tp_tile = 2 * tm
    assert seq % tm == 0 and tm % SGU_CHUNK == 0 and seq % tq == 0 and tq % WINDOW == 0

    cos_p, sin_p = _rope_tables(jnp.arange(seq))
    cos_s, sin_s = _rope_tables(jnp.tile(PAST_LEN + jnp.arange(dec_seq), dec_batch))

    g1 = norm1_g.reshape(DEPTH, 1, D_MODEL)
    g2 = norm2_g.reshape(DEPTH, 1, D_MODEL)
    qg = jnp.tile(q_norm_g, (1, N_KV_HEADS)).reshape(DEPTH, 1, KV_W)
    kg = jnp.tile(k_norm_g, (1, N_KV_HEADS)).reshape(DEPTH, 1, KV_W)
    lng = sgu_ln_g.reshape(DEPTH, 1, SGU_W)
    lnb = sgu_ln_b.reshape(DEPTH, 1, SGU_W)
    bs_t = jnp.swapaxes(sgu_b, 1, 2)
    ws_s = sgu_w[:, :, :dec_seq, :dec_seq].reshape(DEPTH, -1)
    bs_s = sgu_b[:, :, :dec_seq].reshape(DEPTH, -1)
    ck = cache_k.reshape(DEPTH, dec_batch * WINDOW, KV_W)
    cv = cache_v.reshape(DEPTH, dec_batch * WINDOW, KV_W)

    xp = x_prompt.reshape(tp, D_MODEL)
    xs = x_sample.reshape(ts, D_MODEL)
    kp_l, vp_l, ks_l, vs_l, sv_l = [], [], [], [], []
    for l in range(DEPTH):
        xn, q, k, v, xns, qs, ks, vs, w_up = _qkv_call(
            l, xp, xs, g1, w_in, cos_p, sin_p, cos_s, sin_s, qg, kg, w_attn_up, w_sgu_up, tm, seq // tm)
        m, ms, vsn, w_o = _sgu_call(l, xn, xns, w_in, lng, lnb, sgu_w, bs_t, ws_s, bs_s, w_out, tp_tile, dec_seq)
        a, w_gate = _attn_prompt_call(l, attn_sinks, q, k, v, w_in, batch, seq, tq, tn)
        hp = _merge_call(xp, xn, a, m, w_gate, w_up, w_o, tm, tn)
        a, nk, nv = _attn_sample_call(l, attn_sinks, qs, ks, vs, ck, cv, dec_seq, 8)
        hs = _merge_call(xs, xns, a, ms, w_gate, w_up, w_o, ts, tn)
        xp, xs = _ffn_call(l, hp, hs, g2, w_ff1, w_ff2, 2 * tm, tn)
        kp_l.append(k.reshape(batch, seq, KV_W)[:, -WINDOW:].reshape(batch, WINDOW, N_KV_HEADS, HEAD_DIM))
        vp_l.append(v.reshape(batch, seq, KV_W)[:, -WINDOW:].reshape(batch, WINDOW, N_KV_HEADS, HEAD_DIM))
        ks_l.append(nk.reshape(dec_batch, WINDOW, N_KV_HEADS, HEAD_DIM))
        vs_l.append(nv.reshape(dec_batch, WINDOW, N_KV_HEADS, HEAD_DIM))
        sv_l.append(vsn.reshape(dec_batch, dec_seq, SGU_GROUPS, SGU_CH))

    return (xp.reshape(batch, seq, D_MODEL), xs.reshape(dec_batch, dec_seq, D_MODEL),
            jnp.stack(kp_l), jnp.stack(vp_l), jnp.stack(ks_l), jnp.stack(vs_l), jnp.stack(sv_l))
```

```python
import functools
import math

import jax
import jax.numpy as jnp
import numpy as np
from jax import lax
from jax.experimental import pallas as pl
from jax.experimental.pallas import tpu as pltpu

D_MODEL = 2048
DEPTH = 4
PAST_LEN = 16384
HEAD_DIM = 64
N_HEADS = 16
N_KV_HEADS = 4
Q_GROUP = 4
ATTN_W = N_HEADS * HEAD_DIM
KV_W = N_KV_HEADS * HEAD_DIM
WINDOW = 128
ROPE_THETA = 10000.0
SGU_CHUNK = 128
SGU_W = D_MODEL // 2
SGU_GROUPS = 8
SGU_CH = SGU_W // SGU_GROUPS
D_FF = 4 * D_MODEL
RMS_EPS = 1e-6
LN_EPS = 1e-5
NEG_INF = -1e30

QKV_W = ATTN_W + 2 * KV_W
UV_OFF = QKV_W
GATE_OFF = QKV_W + 2 * SGU_W

LANES = 128
VMEM_LIMIT = 58 * 1024 * 1024

BF16 = jnp.bfloat16
F32 = jnp.float32


def _dot(a, b):
    return jnp.dot(a, b, preferred_element_type=F32)


def _rms_rows(x, g):
    ms = jnp.mean(x * x, axis=-1, keepdims=True)
    return x * lax.rsqrt(ms + RMS_EPS) * g


def _gelu(x):
    return 0.5 * x * (1.0 + lax.erf(x * np.float32(math.sqrt(0.5))))


def _params(*sem):
    return pltpu.CompilerParams(dimension_semantics=sem, vmem_limit_bytes=VMEM_LIMIT)


def _head_norm_rope(z, gain, cos, sin, gmat, first_half):
    width = z.shape[1]
    sq = z * z
    hi = sq.astype(BF16)
    lo = (sq - hi.astype(F32)).astype(BF16)
    ssum = _dot(jnp.concatenate([hi, lo], axis=1), gmat)
    zn = z * lax.rsqrt(ssum * (1.0 / HEAD_DIM) + RMS_EPS) * gain
    rot = jnp.where(first_half, pltpu.roll(zn, width - HEAD_DIM // 2, 1), pltpu.roll(zn, HEAD_DIM // 2, 1))
    return zn * cos + rot * sin


def _qkv_kernel(x_ref, xs_ref, g1_ref, w_ref, cos_ref, sin_ref, coss_ref, sins_ref, qg_ref, kg_ref,
                wau_ref, wsu_ref,
                xn_ref, q_ref, k_ref, v_ref, xns_ref, qs_ref, ks_ref, vs_ref, wup_bf_ref):
    wup_bf_ref[0] = wau_ref[...].astype(BF16)
    wup_bf_ref[1] = wsu_ref[...].astype(BF16)
    tm = x_ref.shape[0]
    rows = tm + xs_ref.shape[0]
    cw = KV_W
    g1 = g1_ref[...]
    xn = jnp.concatenate([_rms_rows(x_ref[...], g1).astype(BF16), _rms_rows(xs_ref[...], g1).astype(BF16)],
                         axis=0)
    xn_ref[...] = xn[:tm]
    xns_ref[...] = xn[tm:]
    lane = lax.broadcasted_iota(jnp.int32, (rows, cw), 1)
    first_half = (lane % HEAD_DIM) < (HEAD_DIM // 2)
    r = (lax.broadcasted_iota(jnp.int32, (2 * cw, cw), 0) % cw) // HEAD_DIM
    c = lax.broadcasted_iota(jnp.int32, (2 * cw, cw), 1) // HEAD_DIM
    gmat = jnp.where(r == c, 1.0, 0.0).astype(BF16)
    cos = jnp.concatenate([cos_ref[...], coss_ref[...]], axis=0)
    sin = jnp.concatenate([sin_ref[...], sins_ref[...]], axis=0)
    scale = HEAD_DIM ** -0.5
    n_q = ATTN_W // cw

    def project(c):
        return _dot(xn, w_ref[:, c * cw:(c + 1) * cw].astype(BF16))

    z = project(0)
    for c in range(n_q + 1):
        z_next = project(c + 1)
        if c < n_q:
            q = (_head_norm_rope(z, qg_ref[...], cos, sin, gmat, first_half) * scale).astype(BF16)
            q_ref[:, c * cw:(c + 1) * cw] = q[:tm]
            qs_ref[:, c * cw:(c + 1) * cw] = q[tm:]
        else:
            k = _head_norm_rope(z, kg_ref[...], cos, sin, gmat, first_half)
            k_ref[...] = k[:tm]
            ks_ref[...] = k[tm:]
        z = z_next
    v_ref[...] = z[:tm]
    vs_ref[...] = z[tm:]


def _qkv_call(l, x, xs, g1, w_in, cos_t, sin_t, cos_s, sin_s, qg, kg, w_attn_up, w_sgu_up, tm, n_pos_blocks):
    T = x.shape[0]
    n_tiles = T // tm
    Ts = xs.shape[0]
    ts = Ts // n_tiles
    assert Ts == ts * n_tiles and ts % 16 == 0
    assert ATTN_W == SGU_W and ATTN_W % (16 * n_tiles) == 0
    row = lambda i: (i, 0)
    layer = lambda i: (l, 0, 0)
    pos = lambda i: (i % n_pos_blocks, 0)
    return pl.pallas_call(
        _qkv_kernel,
        grid=(n_tiles,),
        in_specs=[
            pl.BlockSpec((tm, D_MODEL), row),
            pl.BlockSpec((ts, D_MODEL), row),
            pl.BlockSpec((None, 1, D_MODEL), layer),
            pl.BlockSpec((None, D_MODEL, QKV_W), layer),
            pl.BlockSpec((tm, KV_W), pos),
            pl.BlockSpec((tm, KV_W), pos),
            pl.BlockSpec((ts, KV_W), row),
            pl.BlockSpec((ts, KV_W), row),
            pl.BlockSpec((None, 1, KV_W), layer),
            pl.BlockSpec((None, 1, KV_W), layer),
            pl.BlockSpec((None, ATTN_W // n_tiles, D_MODEL), lambda i: (l, i, 0)),
            pl.BlockSpec((None, SGU_W // n_tiles, D_MODEL), lambda i: (l, i, 0)),
        ],
        out_specs=[
            pl.BlockSpec((tm, D_MODEL), row),
            pl.BlockSpec((tm, ATTN_W), row),
            pl.BlockSpec((tm, KV_W), row),
            pl.BlockSpec((tm, KV_W), row),
            pl.BlockSpec((ts, D_MODEL), row),
            pl.BlockSpec((ts, ATTN_W), row),
            pl.BlockSpec((ts, KV_W), row),
            pl.BlockSpec((ts, KV_W), row),
            pl.BlockSpec((2, ATTN_W // n_tiles, D_MODEL), lambda i: (0, i, 0)),
        ],
        out_shape=[
            jax.ShapeDtypeStruct((T, D_MODEL), BF16),
            jax.ShapeDtypeStruct((T, ATTN_W), BF16),
            jax.ShapeDtypeStruct((T, KV_W), F32),
            jax.ShapeDtypeStruct((T, KV_W), F32),
            jax.ShapeDtypeStruct((Ts, D_MODEL), BF16),
            jax.ShapeDtypeStruct((Ts, ATTN_W), BF16),
            jax.ShapeDtypeStruct((Ts, KV_W), F32),
            jax.ShapeDtypeStruct((Ts, KV_W), F32),
            jax.ShapeDtypeStruct((2, ATTN_W, D_MODEL), BF16),
        ],
        compiler_params=_params("parallel"),
        name="qkv",
    )(x, xs, g1, w_in, cos_t, sin_t, cos_s, sin_s, qg, kg, w_attn_up, w_sgu_up)


UV_BLOCK = 512
UV_BLOCKS = 2 * SGU_W // UV_BLOCK


def _uv_project(xn, w_refs, lng_ref, lnb_ref, u_scr, vs_scr):
    cw = 2 * LANES
    per_block = UV_BLOCK // cw
    n_u = SGU_W // cw
    for c in list(range(n_u, 2 * n_u)) + list(range(n_u)):
        w = w_refs[c // per_block][:, (c % per_block) * cw:(c % per_block + 1) * cw].astype(BF16)
        dst = u_scr if c < n_u else vs_scr
        c_dst = c % n_u
        dst[:, c_dst * cw:(c_dst + 1) * cw] = _gelu(_dot(xn, w))
    vs = vs_scr[...]
    mu = jnp.mean(vs, axis=-1, keepdims=True)
    d = vs - mu
    var = jnp.mean(d * d, axis=-1, keepdims=True)
    vs_scr[...] = d * lax.rsqrt(var + LN_EPS) * lng_ref[...] + lnb_ref[...]


def _sgu_kernel(wss_ref, bss_ref, xn_ref, xns_ref, *refs, l, seq):
    w_refs = refs[:UV_BLOCKS]
    (lng_ref, lnb_ref, ws_ref, bs_ref, wout_ref,
     m_ref, ms_ref, vsn_ref, wout_bf_ref, u_scr, vs_scr, g_scr) = refs[UV_BLOCKS:]
    wout_bf_ref[...] = wout_ref[...].astype(BF16)
    tm = xn_ref.shape[0]
    nseq = xns_ref.shape[0] // seq
    xn = jnp.concatenate([xn_ref[...], xns_ref[...]], axis=0)
    _uv_project(xn, w_refs, lng_ref, lnb_ref, u_scr, vs_scr)

    t = lax.broadcasted_iota(jnp.int32, (SGU_CHUNK, SGU_CHUNK), 0)
    s = lax.broadcasted_iota(jnp.int32, (SGU_CHUNK, SGU_CHUNK), 1)
    causal = t >= s
    for g in range(SGU_GROUPS):
        wg = jnp.where(causal, ws_ref[g], 0.0).astype(BF16)
        bias = bs_ref[:, g:g + 1]
        cols = slice(g * SGU_CH, (g + 1) * SGU_CH)
        for n in range(tm // SGU_CHUNK):
            rows = slice(n * SGU_CHUNK, (n + 1) * SGU_CHUNK)
            mixed = _dot(wg, vs_scr[rows, cols].astype(BF16)) + bias
            m_ref[rows, cols] = (u_scr[rows, cols] * mixed).astype(BF16)

    vsn_ref[...] = vs_scr[tm:, :]
    for g in range(SGU_GROUPS):
        cols = slice(g * SGU_CH, (g + 1) * SGU_CH)
        g_scr[0] = vs_scr[tm:, cols]
        g_scr[1] = u_scr[tm:, cols]
        for t in range(seq):
            step_t = pl.ds(t, nseq, stride=seq)
            acc = jnp.full((nseq, SGU_CH), bss_ref[l, g * seq + t], F32)
            for s in range(t + 1):
                acc = acc + wss_ref[l, (g * seq + t) * seq + s] * g_scr[0, pl.ds(s, nseq, stride=seq), :]
            g_scr[2, step_t, :] = g_scr[1, step_t, :] * acc
        ms_ref[:, cols] = g_scr[2].astype(BF16)


def _uv_weight_specs(l):
    return [pl.BlockSpec((None, D_MODEL, UV_BLOCK), functools.partial(lambda b, i: (l, 0, b), UV_OFF // UV_BLOCK + k))
            for k in range(UV_BLOCKS)]


def _sgu_call(l, xn, xns, w_in, lng, lnb, ws, bs_t, ws_sample, bs_sample, w_out, tm, seq):
    T = xn.shape[0]
    n_tiles = T // tm
    Ts = xns.shape[0]
    ts = Ts // n_tiles
    assert Ts == ts * n_tiles and ts % 16 == 0 and ts % seq == 0 and D_MODEL % (16 * n_tiles) == 0
    slab = D_MODEL // n_tiles
    row = lambda i: (i, 0)
    layer = lambda i: (l, 0, 0)
    smem = pl.BlockSpec(memory_space=pltpu.SMEM)
    return pl.pallas_call(
        functools.partial(_sgu_kernel, l=l, seq=seq),
        grid=(n_tiles,),
        in_specs=[
            smem,
            smem,
            pl.BlockSpec((tm, D_MODEL), row),
            pl.BlockSpec((ts, D_MODEL), row),
            *_uv_weight_specs(l),
            pl.BlockSpec((None, 1, SGU_W), layer),
            pl.BlockSpec((None, 1, SGU_W), layer),
            pl.BlockSpec((None, SGU_GROUPS, SGU_CHUNK, SGU_CHUNK), lambda i: (l, 0, 0, 0)),
            pl.BlockSpec((None, SGU_CHUNK, SGU_GROUPS), layer),
            pl.BlockSpec((None, slab, D_MODEL), lambda i: (l, i, 0)),
        ],
        out_specs=[pl.BlockSpec((tm, SGU_W), row), pl.BlockSpec((ts, SGU_W), row),
                   pl.BlockSpec((ts, SGU_W), row), pl.BlockSpec((slab, D_MODEL), row)],
        out_shape=[jax.ShapeDtypeStruct((T, SGU_W), BF16), jax.ShapeDtypeStruct((Ts, SGU_W), BF16),
                   jax.ShapeDtypeStruct((Ts, SGU_W), F32), jax.ShapeDtypeStruct((D_MODEL, D_MODEL), BF16)],
        scratch_shapes=[pltpu.VMEM((tm + ts, SGU_W), F32), pltpu.VMEM((tm + ts, SGU_W), F32),
                        pltpu.VMEM((3, ts, SGU_CH), F32)],
        compiler_params=_params("parallel"),
        name="sgu",
    )(ws_sample, bs_sample, xn, xns, *([w_in] * UV_BLOCKS), lng, lnb, ws, bs_t, w_out)


def _attn_prompt_kernel(sink_ref, q_ref, kp_ref, kc_ref, vp_ref, vc_ref, wg_ref, o_ref, wg_bf_ref, *, l):
    wg_bf_ref[...] = wg_ref[...].astype(BF16)
    blk = WINDOW
    nq = Q_GROUP * blk
    first_tile = pl.program_id(1) == 0
    c = lax.broadcasted_iota(jnp.int32, (blk, nq), 0)
    r = lax.broadcasted_iota(jnp.int32, (blk, nq), 1) % blk
    from_prev = c > r
    k_all = jnp.concatenate([kp_ref[...], kc_ref[...]], axis=0).astype(BF16)
    vt_all = jnp.concatenate([vp_ref[...], vc_ref[...]], axis=0).T.astype(BF16)
    sinks = [jnp.concatenate([jnp.full((1, blk), sink_ref[l, g * Q_GROUP + j], F32) for j in range(Q_GROUP)],
                             axis=1) for g in range(N_KV_HEADS)]

    def scores(jb):
        out = []
        for g in range(N_KV_HEADS):
            qg = jnp.concatenate([q_ref[jb * blk:(jb + 1) * blk, h * HEAD_DIM:(h + 1) * HEAD_DIM]
                                  for h in range(g * Q_GROUP, (g + 1) * Q_GROUP)], axis=0)
            kg = k_all[jb * blk:(jb + 2) * blk, g * HEAD_DIM:(g + 1) * HEAD_DIM]
            out.append(lax.dot_general(kg, qg, (((1,), (1,)), ((), ())), preferred_element_type=F32))
        return out

    nblk = q_ref.shape[0] // blk
    s2 = scores(0)
    for jb in range(nblk):
        s2_next = scores(jb + 1) if jb + 1 < nblk else None
        for g in range(N_KV_HEADS):
            s_prev = s2[g][:blk]
            if jb == 0:
                s_prev = jnp.where(first_tile, NEG_INF, s_prev)
            s = jnp.where(from_prev, s_prev, s2[g][blk:])
            m = jnp.maximum(jnp.max(s, axis=0, keepdims=True), sinks[g])
            p = jnp.exp(s - m)
            p = p / (jnp.sum(p, axis=0, keepdims=True) + jnp.exp(sinks[g] - m))
            p2 = jnp.concatenate([jnp.where(from_prev, p, 0.0), jnp.where(from_prev, 0.0, p)], axis=0)
            vt = vt_all[g * HEAD_DIM:(g + 1) * HEAD_DIM, jb * blk:(jb + 2) * blk]
            o = _dot(vt, p2.astype(BF16)).T
            for j in range(Q_GROUP):
                h = g * Q_GROUP + j
                o_ref[jb * blk:(jb + 1) * blk, h * HEAD_DIM:(h + 1) * HEAD_DIM] = (
                    o[j * blk:(j + 1) * blk].astype(BF16))
        s2 = s2_next


def _attn_prompt_call(l, sinks, q, k, v, w_in, batch, seq, tq, merge_tn):
    nt = seq // tq
    blocks_per_tile = tq // WINDOW
    gate_w = 2 * D_MODEL // (batch * nt)
    assert gate_w % LANES == 0 and GATE_OFF % gate_w == 0 and merge_tn % gate_w == 0
    per_block = merge_tn // gate_w
    per_gate = D_MODEL // gate_w

    def gate_dst(b, i):
        t = b * nt + i
        gate, r = t // per_gate, t % per_gate
        return (r // per_block) * (2 * per_block) + gate * per_block + r % per_block

    cur = lambda b, i: (b * nt + i, 0)
    prev = lambda b, i: (b * (seq // WINDOW) + jnp.maximum(i * blocks_per_tile - 1, 0), 0)
    return pl.pallas_call(
        functools.partial(_attn_prompt_kernel, l=l),
        grid=(batch, nt),
        in_specs=[
            pl.BlockSpec(memory_space=pltpu.SMEM),
            pl.BlockSpec((tq, ATTN_W), cur),
            pl.BlockSpec((WINDOW, KV_W), prev),
            pl.BlockSpec((tq, KV_W), cur),
            pl.BlockSpec((WINDOW, KV_W), prev),
            pl.BlockSpec((tq, KV_W), cur),
            pl.BlockSpec((None, D_MODEL, gate_w), lambda b, i: (l, 0, GATE_OFF // gate_w + b * nt + i)),
        ],
        out_specs=[pl.BlockSpec((tq, ATTN_W), cur),
                   pl.BlockSpec((D_MODEL, gate_w), lambda b, i: (0, gate_dst(b, i)))],
        out_shape=[jax.ShapeDtypeStruct((batch * seq, ATTN_W), BF16),
                   jax.ShapeDtypeStruct((D_MODEL, 2 * D_MODEL), BF16)],
        compiler_params=_params("parallel", "parallel"),
        name="attn_prompt",
    )(sinks, q, k, k, v, v, w_in)


def _attn_sample_kernel(sink_ref, q_ref, kn_ref, vn_ref, ck_ref, cv_ref, o_ref, nk_ref, nv_ref, *, l, seq):
    nseq = q_ref.shape[0] // seq
    nq = nseq * seq
    rows = Q_GROUP * nq
    ri = lax.broadcasted_iota(jnp.int32, (rows, nseq * WINDOW), 0) % nq
    ci = lax.broadcasted_iota(jnp.int32, (rows, nseq * WINDOW), 1)
    mask_c = (ri // seq == ci // WINDOW) & (ci % WINDOW > ri % seq)
    ri = lax.broadcasted_iota(jnp.int32, (rows, nq), 0) % nq
    ci = lax.broadcasted_iota(jnp.int32, (rows, nq), 1)
    mask_n = (ri // seq == ci // seq) & (ci % seq <= ri % seq)
    dn = (((1,), (1,)), ((), ()))
    scores = []
    for g in range(N_KV_HEADS):
        cols = slice(g * HEAD_DIM, (g + 1) * HEAD_DIM)
        qg = jnp.concatenate([q_ref[:, h * HEAD_DIM:(h + 1) * HEAD_DIM]
                              for h in range(g * Q_GROUP, (g + 1) * Q_GROUP)], axis=0)
        scores.append((lax.dot_general(qg, ck_ref[:, cols].astype(BF16), dn, preferred_element_type=F32),
                       lax.dot_general(qg, kn_ref[:, cols].astype(BF16), dn, preferred_element_type=F32)))
    for g in range(N_KV_HEADS):
        cols = slice(g * HEAD_DIM, (g + 1) * HEAD_DIM)
        heads = range(g * Q_GROUP, (g + 1) * Q_GROUP)
        sink = jnp.concatenate([jnp.full((nq, 1), sink_ref[l, h], F32) for h in heads], axis=0)
        s_c = jnp.where(mask_c, scores[g][0], NEG_INF)
        s_n = jnp.where(mask_n, scores[g][1], NEG_INF)
        m = jnp.maximum(jnp.maximum(jnp.max(s_c, axis=-1, keepdims=True),
                                    jnp.max(s_n, axis=-1, keepdims=True)), sink)
        p_c = jnp.exp(s_c - m)
        p_n = jnp.exp(s_n - m)
        denom = (jnp.sum(p_c, axis=-1, keepdims=True) + jnp.sum(p_n, axis=-1, keepdims=True)
                 + jnp.exp(sink - m))
        o = (_dot((p_c / denom).astype(BF16), cv_ref[:, cols].astype(BF16))
             + _dot((p_n / denom).astype(BF16), vn_ref[:, cols].astype(BF16)))
        for j, h in enumerate(heads):
            o_ref[:, h * HEAD_DIM:(h + 1) * HEAD_DIM] = o[j * nq:(j + 1) * nq].astype(BF16)
    for b in range(nseq):
        nk_ref[b * WINDOW:(b + 1) * WINDOW - seq, :] = ck_ref[b * WINDOW + seq:(b + 1) * WINDOW, :]
        nv_ref[b * WINDOW:(b + 1) * WINDOW - seq, :] = cv_ref[b * WINDOW + seq:(b + 1) * WINDOW, :]
        nk_ref[(b + 1) * WINDOW - seq:(b + 1) * WINDOW, :] = kn_ref[b * seq:(b + 1) * seq, :]
        nv_ref[(b + 1) * WINDOW - seq:(b + 1) * WINDOW, :] = vn_ref[b * seq:(b + 1) * seq, :]


def _attn_sample_call(l, sinks, q, k, v, ck, cv, seq, seq_block):
    T = q.shape[0]
    nseq = T // seq
    row = lambda i: (i, 0)
    cache = lambda i: (l, i, 0)
    return pl.pallas_call(
        functools.partial(_attn_sample_kernel, l=l, seq=seq),
        grid=(nseq // seq_block,),
        in_specs=[
            pl.BlockSpec(memory_space=pltpu.SMEM),
            pl.BlockSpec((seq_block * seq, ATTN_W), row),
            pl.BlockSpec((seq_block * seq, KV_W), row),
            pl.BlockSpec((seq_block * seq, KV_W), row),
            pl.BlockSpec((None, seq_block * WINDOW, KV_W), cache),
            pl.BlockSpec((None, seq_block * WINDOW, KV_W), cache),
        ],
        out_specs=[
            pl.BlockSpec((seq_block * seq, ATTN_W), row),
            pl.BlockSpec((seq_block * WINDOW, KV_W), row),
            pl.BlockSpec((seq_block * WINDOW, KV_W), row),
        ],
        out_shape=[
            jax.ShapeDtypeStruct((T, ATTN_W), BF16),
            jax.ShapeDtypeStruct((nseq * WINDOW, KV_W), F32),
            jax.ShapeDtypeStruct((nseq * WINDOW, KV_W), F32),
        ],
        compiler_params=_params("parallel"),
        name="attn_sample",
    )(sinks, q, k, v, ck, cv)


GATE_SLOTS = 3


def _merge_kernel(x_ref, xn_ref, a_ref, m_ref, wg_hbm, wup_ref, wout_ref, h_ref, merged_scr, wg_ring, wg_sem,
                  *, n_steps):
    j = pl.program_id(1)
    nj, _, tn = merged_scr.shape
    s = pl.program_id(0) * nj + j
    ahead = GATE_SLOTS - 1

    def gate_copy(step):
        col = (step % nj) * (2 * tn)
        if not isinstance(col, int):
            col = pl.multiple_of(col, 2 * tn)
        slot = step % GATE_SLOTS
        return pltpu.make_async_copy(wg_hbm.at[:, pl.ds(col, 2 * tn)], wg_ring.at[slot], wg_sem.at[slot])

    @pl.when(s == 0)
    def _():
        for k in range(min(ahead, n_steps)):
            gate_copy(k).start()

    @pl.when(s + ahead < n_steps)
    def _():
        gate_copy(s + ahead).start()

    gate_copy(s).wait()
    slot = s % GATE_SLOTS
    xn = xn_ref[...]
    ga = jax.nn.sigmoid(_dot(xn, wg_ring[slot, :, :tn]))
    gm = jax.nn.sigmoid(_dot(xn, wg_ring[slot, :, tn:]))
    merged = ga * _dot(a_ref[...], wup_ref[0]) + gm * _dot(m_ref[...], wup_ref[1])
    merged_scr[j] = merged.astype(BF16)

    @pl.when(j == pl.num_programs(1) - 1)
    def _():
        acc = x_ref[...]
        for c in range(merged_scr.shape[0]):
            acc = acc + _dot(merged_scr[c], wout_ref[c * tn:(c + 1) * tn, :])
        h_ref[...] = acc


def _merge_call(x, xn, a, m, w_gate, w_up, w_out, tm, tn):
    T = x.shape[0]
    nj = D_MODEL // tn
    n_tiles = T // tm
    row = lambda i, j: (i, 0)
    return pl.pallas_call(
        functools.partial(_merge_kernel, n_steps=n_tiles * nj),
        grid=(n_tiles, nj),
        in_specs=[
            pl.BlockSpec((tm, D_MODEL), row),
            pl.BlockSpec((tm, D_MODEL), row),
            pl.BlockSpec((tm, ATTN_W), row),
            pl.BlockSpec((tm, SGU_W), row),
            pl.BlockSpec(memory_space=pl.ANY),
            pl.BlockSpec((2, ATTN_W, tn), lambda i, j: (0, 0, j)),
            pl.BlockSpec((D_MODEL, D_MODEL), lambda i, j: (0, 0)),
        ],
        out_specs=pl.BlockSpec((tm, D_MODEL), row),
        out_shape=jax.ShapeDtypeStruct((T, D_MODEL), F32),
        scratch_shapes=[pltpu.VMEM((nj, tm, tn), BF16),
                        pltpu.VMEM((GATE_SLOTS, D_MODEL, 2 * tn), BF16),
                        pltpu.SemaphoreType.DMA((GATE_SLOTS,))],
        compiler_params=_params("arbitrary", "arbitrary"),
        name="merge",
    )(x, xn, a, m, w_gate, w_up, w_out)


def _ffn_kernel(h_ref, hs_ref, g2_ref, w1_ref, w2_ref, y_ref, ys_ref, hn_scr):
    tm = h_ref.shape[0]

    @pl.when(pl.program_id(1) == 0)
    def _():
        h = h_ref[...]
        hn_scr[:tm, :] = _rms_rows(h, g2_ref[...]).astype(BF16)
        y_ref[...] = h
        hs = hs_ref[...]
        hn_scr[tm:, :] = _rms_rows(hs, g2_ref[...]).astype(BF16)
        ys_ref[...] = hs

    half = tm // 2
    w1 = w1_ref[...].astype(BF16)
    w2 = w2_ref[...].astype(BF16)
    acts = []
    for rows in (slice(0, half), slice(half, hn_scr.shape[0])):
        a = jnp.maximum(_dot(hn_scr[rows, :], w1), 0.0)
        acts.append((a * a).astype(BF16))
    y_ref[:half, :] += _dot(acts[0], w2)
    second = _dot(acts[1], w2)
    y_ref[half:, :] += second[:half]
    ys_ref[...] += second[half:]


def _ffn_call(l, h, hs, g2, w1, w2, tm, tf):
    T = h.shape[0]
    n_tiles = T // tm
    ts = hs.shape[0] // n_tiles
    assert hs.shape[0] == ts * n_tiles and ts % 16 == 0
    row = lambda i, f: (i, 0)
    return pl.pallas_call(
        _ffn_kernel,
        grid=(n_tiles, D_FF // tf),
        in_specs=[
            pl.BlockSpec((tm, D_MODEL), row),
            pl.BlockSpec((ts, D_MODEL), row),
            pl.BlockSpec((None, 1, D_MODEL), lambda i, f: (l, 0, 0)),
            pl.BlockSpec((None, D_MODEL, tf), lambda i, f: (l, 0, f)),
            pl.BlockSpec((None, tf, D_MODEL), lambda i, f: (l, f, 0)),
        ],
        out_specs=[pl.BlockSpec((tm, D_MODEL), row), pl.BlockSpec((ts, D_MODEL), row)],
        out_shape=[jax.ShapeDtypeStruct((T, D_MODEL), F32), jax.ShapeDtypeStruct(hs.shape, F32)],
        scratch_shapes=[pltpu.VMEM((tm + ts, D_MODEL), BF16)],
        compiler_params=_params("parallel", "arbitrary"),
        name="ffn",
    )(h, hs, g2, w1, w2)


def _rope_tables(pos):
    half = HEAD_DIM // 2
    inv = jnp.power(jnp.float32(ROPE_THETA), -jnp.arange(half, dtype=jnp.float32) / half)
    ang = pos.astype(jnp.float32)[:, None] * inv[None, :]
    cos = jnp.cos(ang)
    sin = jnp.sin(ang)
    return jnp.tile(cos, (1, 2 * N_KV_HEADS)), jnp.tile(jnp.concatenate([-sin, sin], axis=-1), (1, N_KV_HEADS))


def kernel(x_prompt, x_sample, cache_k, cache_v, norm1_g, w_in, q_norm_g, k_norm_g, attn_sinks,
           sgu_ln_g, sgu_ln_b, sgu_w, sgu_b, w_attn_up, w_sgu_up, w_out, norm2_g, w_ff1, w_ff2):
    batch, seq, _ = x_prompt.shape
    dec_batch, dec_seq, _ = x_sample.shape
    tp = batch * seq
    ts = dec_batch * dec_seq
    tm = 512
    tn = 512
    tq = 512
    tp_tile = 2 * tm
    assert seq % tm == 0 and tm % SGU_CHUNK == 0 and seq % tq == 0 and tq % WINDOW == 0

    cos_p, sin_p = _rope_tables(jnp.arange(seq))
    cos_s, sin_s = _rope_tables(jnp.tile(PAST_LEN + jnp.arange(dec_seq), dec_batch))

    g1 = norm1_g.reshape(DEPTH, 1, D_MODEL)
    g2 = norm2_g.reshape(DEPTH, 1, D_MODEL)
    qg = jnp.tile(q_norm_g, (1, N_KV_HEADS)).reshape(DEPTH, 1, KV_W)
    kg = jnp.tile(k_norm_g, (1, N_KV_HEADS)).reshape(DEPTH, 1, KV_W)
    lng = sgu_ln_g.reshape(DEPTH, 1, SGU_W)
    lnb = sgu_ln_b.reshape(DEPTH, 1, SGU_W)
    bs_t = jnp.swapaxes(sgu_b, 1, 2)
    ws_s = sgu_w[:, :, :dec_seq, :dec_seq].reshape(DEPTH, -1)
    bs_s = sgu_b[:, :, :dec_seq].reshape(DEPTH, -1)
    ck = cache_k.reshape(DEPTH, dec_batch * WINDOW, KV_W)
    cv = cache_v.reshape(DEPTH, dec_batch * WINDOW, KV_W)

    xp = x_prompt.reshape(tp, D_MODEL)
    xs = x_sample.reshape(ts, D_MODEL)
    kp_l, vp_l, ks_l, vs_l, sv_l = [], [], [], [], []
    for l in range(DEPTH):
        xn, q, k, v, xns, qs, ks, vs, w_up = _qkv_call(
            l, xp, xs, g1, w_in, cos_p, sin_p, cos_s, sin_s, qg, kg, w_attn_up, w_sgu_up, tm, seq // tm)
        m, ms, vsn, w_o = _sgu_call(l, xn, xns, w_in, lng, lnb, sgu_w, bs_t, ws_s, bs_s, w_out, tp_tile, dec_seq)
        a, w_gate = _attn_prompt_call(l, attn_sinks, q, k, v, w_in, batch, seq, tq, tn)
        hp = _merge_call(xp, xn, a, m, w_gate, w_up, w_o, tm, tn)
        a, nk, nv = _attn_sample_call(l, attn_sinks, qs, ks, vs, ck, cv, dec_seq, 8)
        hs = _merge_call(xs, xns, a, ms, w_gate, w_up, w_o, ts, tn)
        xp, xs = _ffn_call(l, hp, hs, g2, w_ff1, w_ff2, 2 * tm, tn)
        kp_l.append(k.reshape(batch, seq, KV_W)[:, -WINDOW:].reshape(batch, WINDOW, N_KV_HEADS, HEAD_DIM))
        vp_l.append(v.reshape(batch, seq, KV_W)[:, -WINDOW:].reshape(batch, WINDOW, N_KV_HEADS, HEAD_DIM))
        ks_l.append(nk.reshape(dec_batch, WINDOW, N_KV_HEADS, HEAD_DIM))
        vs_l.append(nv.reshape(dec_batch, WINDOW, N_KV_HEADS, HEAD_DIM))
        sv_l.append(vsn.reshape(dec_batch, dec_seq, SGU_GROUPS, SGU_CH))

    return (xp.reshape(batch, seq, D_MODEL), xs.reshape(dec_batch, dec_seq, D_MODEL),
            jnp.stack(kp_l), jnp.stack(vp_l), jnp.stack(ks_l), jnp.stack(vs_l), jnp.stack(sv_l))
```
